```python
import jax, jax.numpy as jnp
from jax import lax
import numpy as np

D_MODEL = 1024
BATCH = 8
SEQ = 2048
DEPTH = 1
DEC_BATCH = 32
DEC_SEQ = 1
PAST_LEN = 8192
PAGE_SIZE = 128

R_HEADS = 4
R_DK = 128
R_DV = 256
R_CHUNK = 128
A_HEADS = 8
A_DH = 128
MOBA_BLOCK = 256
MOBA_TOPK = 3
Q_CHUNK = 64
ROPE_THETA = 10000.0
N_GROUPS = 4
EXPERTS_PER_GROUP = 8
N_EXPERTS = N_GROUPS * EXPERTS_PER_GROUP
D_EXPERT = 256
TOP_K_INNER = 2
EPS = 1e-6

R_QK_W = R_HEADS * R_DK
R_V_W = R_HEADS * R_DV
A_W = A_HEADS * A_DH
D_IN = 2 * R_QK_W + 2 * R_V_W + 3 * A_W + 2 * D_MODEL

kernel_name = 'retention_moba_hier_moe_step'


def rmsnorm(x, g):
    xf = x.astype(jnp.float32)
    y = xf * lax.rsqrt(jnp.mean(xf * xf, axis=-1, keepdims=True) + EPS)
    return (y * g.astype(jnp.float32)).astype(x.dtype)


def rope(x, pos):
    half = x.shape[-1] // 2
    inv = ROPE_THETA ** (-jnp.arange(half, dtype=jnp.float32) / half)
    ang = pos.astype(jnp.float32)[:, None] * inv[None, :]
    cos = jnp.cos(ang)[None, :, None, :]
    sin = jnp.sin(ang)[None, :, None, :]
    xf = x.astype(jnp.float32)
    x1, x2 = xf[..., :half], xf[..., half:]
    return jnp.concatenate([x1 * cos - x2 * sin, x2 * cos + x1 * sin], axis=-1).astype(x.dtype)


def project(h, w_in):
    z = jnp.einsum('bld,de->ble', h, w_in)
    sizes = (R_QK_W, R_QK_W, R_V_W, R_V_W, A_W, A_W, A_W, D_MODEL, D_MODEL)
    points = [sum(sizes[:i]) for i in range(1, len(sizes))]
    return jnp.split(z, points, axis=-1)


def retention_log_decay():
    return jnp.log(1.0 - 2.0 ** (-5.0 - jnp.arange(R_HEADS, dtype=jnp.float32)))


def retention_chunk(s, qkv):
    q, k, v = qkv
    L = q.shape[1]
    ld = retention_log_decay()
    i = jnp.arange(L, dtype=jnp.float32)
    diff = i[:, None] - i[None, :]
    dmask = jnp.where(diff >= 0, jnp.exp(ld[:, None, None] * jnp.maximum(diff, 0.0)), 0.0)
    qf, kf, vf = q.astype(jnp.float32), k.astype(jnp.float32), v.astype(jnp.float32)
    scores = jnp.einsum('bihd,bjhd->bhij', qf, kf) * dmask[None]
    inner = jnp.einsum('bhij,bjhe->bihe', scores, vf)
    cross = jnp.einsum('bihd,bhde->bihe', qf, s) * jnp.exp(ld[None, None, :, None] * (i + 1.0)[None, :, None, None])
    k_dec = kf * jnp.exp(ld[None, None, :, None] * (L - 1.0 - i)[None, :, None, None])
    s_new = s * jnp.exp(ld * L)[None, :, None, None] + jnp.einsum('bjhd,bjhe->bhde', k_dec, vf)
    return s_new, inner + cross


def retention(q, k, v, s0, chunk):
    B, L = q.shape[:2]
    nc = L // chunk

    def chunks(t):
        return t.reshape(B, nc, chunk, *t.shape[2:]).swapaxes(0, 1)

    s_fin, o = lax.scan(retention_chunk, s0, (chunks(q), chunks(k), chunks(v)))
    return s_fin, o.swapaxes(0, 1).reshape(B, L, R_HEADS, R_DV)


def retention_output(o, g):
    B, L = o.shape[:2]
    mu = jnp.mean(o, axis=-1, keepdims=True)
    var = jnp.mean(jnp.square(o - mu), axis=-1, keepdims=True)
    y = ((o - mu) * lax.rsqrt(var + EPS)).reshape(B, L, R_V_W)
    return (jax.nn.silu(g.astype(jnp.float32)) * y).astype(g.dtype)


def to_blocks(t):
    B, T, H, dh = t.shape
    nb = -(-T // MOBA_BLOCK)
    t = jnp.pad(t, ((0, 0), (0, nb * MOBA_BLOCK - T), (0, 0), (0, 0)))
    return t.reshape(B, nb, MOBA_BLOCK, H, dh).transpose(0, 3, 1, 2, 4)


def gather_blocks(blocks, idx):
    return jax.vmap(jax.vmap(lambda t, i: t[i]))(blocks, idx)


def moba_attend(q, kb, vb, kmean, q_pos):
    nb = kb.shape[2]
    own = q_pos // MOBA_BLOCK
    qf = q.astype(jnp.float32)
    gate = jnp.einsum('bhqd,bhnd->bhqn', qf, kmean)
    past = jnp.arange(nb)[None, :] < own[:, None]
    gate = jnp.where(past[None, None], gate, -jnp.inf)
    gval, gidx = lax.top_k(gate, min(MOBA_TOPK, nb))
    own_b = jnp.broadcast_to(own[None, None, :, None], gidx.shape[:-1] + (1,)).astype(gidx.dtype)
    idx = jnp.concatenate([gidx, own_b], axis=-1)
    ok = jnp.concatenate([jnp.isfinite(gval), jnp.ones(own_b.shape, dtype=bool)], axis=-1)
    kg = gather_blocks(kb, idx).astype(jnp.float32)
    vg = gather_blocks(vb, idx).astype(jnp.float32)
    key_pos = idx[..., None] * MOBA_BLOCK + jnp.arange(MOBA_BLOCK)
    mask = ok[..., None] & (key_pos <= q_pos[None, None, :, None, None])
    s = jnp.einsum('bhqd,bhqnkd->bhqnk', qf, kg)
    s = jnp.where(mask, s, -jnp.inf)
    sh = s.shape
    p = jax.nn.softmax(s.reshape(sh[:3] + (-1,)), axis=-1).reshape(sh)
    return jnp.einsum('bhqnk,bhqnkd->bhqd', p, vg).astype(q.dtype)


def moba(q, k_all, v_all, q_pos, q_chunk):
    B, L, H, dh = q.shape
    kb = to_blocks(k_all)
    vb = to_blocks(v_all)
    kmean = jnp.mean(kb.astype(jnp.float32), axis=3)
    nq = L // q_chunk
    qc = q.reshape(B, nq, q_chunk, H, dh).transpose(1, 0, 3, 2, 4)
    pc = q_pos.reshape(nq, q_chunk)
    o = lax.map(lambda a: moba_attend(a[0], kb, vb, kmean, a[1]), (qc, pc))
    return o.transpose(1, 0, 3, 2, 4).reshape(B, L, H * dh)


def token_mixers(h, pos, s0, k_past, v_past, r_chunk, q_chunk, w_in, w_ret_o, w_att_o, w_o):
    B, L, _ = h.shape
    rq, rk, rv, rg, aq, ak, av, gr, ga = project(h, w_in)
    rq = rope(rq.reshape(B, L, R_HEADS, R_DK), pos) * (R_DK ** -0.5)
    rk = rope(rk.reshape(B, L, R_HEADS, R_DK), pos)
    rv = rv.reshape(B, L, R_HEADS, R_DV)
    r_state, r_out = retention(rq, rk, rv, s0, r_chunk)
    ret = retention_output(r_out, rg)
    aq = rope(aq.reshape(B, L, A_HEADS, A_DH), pos) * (A_DH ** -0.5)
    ak = rope(ak.reshape(B, L, A_HEADS, A_DH), pos)
    av = av.reshape(B, L, A_HEADS, A_DH)
    if k_past is None:
        k_all, v_all = ak, av
    else:
        k_all = jnp.concatenate([k_past, ak.astype(k_past.dtype)], axis=1)
        v_all = jnp.concatenate([v_past, av.astype(v_past.dtype)], axis=1)
    att = moba(aq, k_all, v_all, pos, q_chunk)
    r = jnp.einsum('ble,ed->bld', ret, w_ret_o)
    a = jnp.einsum('ble,ed->bld', att, w_att_o)
    merged = jax.nn.sigmoid(gr) * r + jax.nn.sigmoid(ga) * a
    return jnp.einsum('bld,de->ble', merged, w_o), ak, av, r_state


def hier_moe(h, w_rg, b_rg, w_re, b_re, w_gate, w_up, w_down):
    glog = jnp.einsum('bld,dg->blg', h, w_rg).astype(jnp.float32) + b_rg.astype(jnp.float32)
    gprob = jax.nn.softmax(glog, axis=-1)
    g_idx = jnp.argmax(glog, axis=-1)
    g_w = jnp.max(gprob, axis=-1)
    elog = jnp.einsum('bld,dge->blge', h, w_re).astype(jnp.float32) + b_re.astype(jnp.float32)
    elog_sel = jnp.einsum('blge,blg->ble', elog, jax.nn.one_hot(g_idx, N_GROUPS, dtype=jnp.float32))
    ev, ei = lax.top_k(elog_sel, TOP_K_INNER)
    ew = jax.nn.softmax(ev, axis=-1)
    flat = g_idx[..., None] * EXPERTS_PER_GROUP + ei
    combine = jnp.sum(jax.nn.one_hot(flat, N_EXPERTS, dtype=jnp.float32) * (g_w[..., None] * ew)[..., None], axis=-2)
    a = jnp.einsum('bld,edf->blef', h, w_gate)
    u = jnp.einsum('bld,edf->blef', h, w_up)
    act = jax.nn.silu(a) * u * combine.astype(h.dtype)[..., None]
    return jnp.einsum('blef,efd->bld', act, w_down)


def setup_inputs(seed: int = 0) -> dict:
    key = jax.random.key(seed)
    ks = jax.random.split(key, 20)
    n_pages = PAST_LEN // PAGE_SIZE
    n_used = DEC_BATCH * n_pages
    n_pool = (n_used * 5 + 3) // 4
    perm = jax.random.permutation(ks[0], n_pool)
    page_table = perm[:n_used].reshape(DEC_BATCH, n_pages).astype(jnp.int32)
    f32 = jnp.float32
    nrm = lambda k, shape, s: jax.random.normal(k, shape, f32) * s
    return {
        'x_prompt': nrm(ks[1], (BATCH, SEQ, D_MODEL), 1.0),
        'x_sample': nrm(ks[2], (DEC_BATCH, DEC_SEQ, D_MODEL), 1.0),
        'cache_k': nrm(ks[3], (DEPTH, n_pool, PAGE_SIZE, A_HEADS, A_DH), 1.0),
        'cache_v': nrm(ks[4], (DEPTH, n_pool, PAGE_SIZE, A_HEADS, A_DH), 1.0),
        'state_ret': nrm(ks[5], (DEPTH, DEC_BATCH, R_HEADS, R_DK, R_DV), 0.1),
        'page_table': page_table,
        'norm_mix': 1.0 + nrm(ks[6], (DEPTH, D_MODEL), 0.02),
        'w_in': nrm(ks[7], (DEPTH, D_MODEL, D_IN), D_MODEL ** -0.5),
        'w_ret_o': nrm(ks[8], (DEPTH, R_V_W, D_MODEL), R_V_W ** -0.5),
        'w_att_o': nrm(ks[9], (DEPTH, A_W, D_MODEL), A_W ** -0.5),
        'w_o': nrm(ks[10], (DEPTH, D_MODEL, D_MODEL), D_MODEL ** -0.5),
        'norm_ffn': 1.0 + nrm(ks[11], (DEPTH, D_MODEL), 0.02),
        'w_router_group': nrm(ks[12], (DEPTH, D_MODEL, N_GROUPS), D_MODEL ** -0.5),
        'b_router_group': nrm(ks[13], (DEPTH, N_GROUPS), 0.01),
        'w_router_expert': nrm(ks[14], (DEPTH, D_MODEL, N_GROUPS, EXPERTS_PER_GROUP), D_MODEL ** -0.5),
        'b_router_expert': nrm(ks[15], (DEPTH, N_GROUPS, EXPERTS_PER_GROUP), 0.01),
        'w_gate': nrm(ks[16], (DEPTH, N_EXPERTS, D_MODEL, D_EXPERT), D_MODEL ** -0.5),
        'w_up': nrm(ks[17], (DEPTH, N_EXPERTS, D_MODEL, D_EXPERT), D_MODEL ** -0.5),
        'w_down': nrm(ks[18], (DEPTH, N_EXPERTS, D_EXPERT, D_MODEL), D_EXPERT ** -0.5),
        'norm_final': 1.0 + nrm(ks[19], (D_MODEL,), 0.02),
    }


def reference(x_prompt, x_sample, cache_k, cache_v, state_ret, page_table, norm_mix, w_in, w_ret_o,
              w_att_o, w_o, norm_ffn, w_router_group, b_router_group, w_router_expert, b_router_expert,
              w_gate, w_up, w_down, norm_final):
    B, S, _ = x_prompt.shape
    DB, DS, _ = x_sample.shape
    past_len = page_table.shape[1] * PAGE_SIZE
    pos_p = jnp.arange(S, dtype=jnp.int32)
    pos_s = past_len + jnp.arange(DS, dtype=jnp.int32)
    yp, ys = x_prompt, x_sample
    kp_l, vp_l, rp_l, ks_l, vs_l, rs_l = [], [], [], [], [], []
    for l in range(DEPTH):
        hp = rmsnorm(yp, norm_mix[l])
        s0 = jnp.zeros((B, R_HEADS, R_DK, R_DV), jnp.float32)
        mp, kp, vp, rp = token_mixers(hp, pos_p, s0, None, None, R_CHUNK, Q_CHUNK,
                                      w_in[l], w_ret_o[l], w_att_o[l], w_o[l])
        yp = yp + mp
        yp = yp + hier_moe(rmsnorm(yp, norm_ffn[l]), w_router_group[l], b_router_group[l],
                           w_router_expert[l], b_router_expert[l], w_gate[l], w_up[l], w_down[l])
        k_past = cache_k[l][page_table].reshape(DB, past_len, A_HEADS, A_DH)
        v_past = cache_v[l][page_table].reshape(DB, past_len, A_HEADS, A_DH)
        hs = rmsnorm(ys, norm_mix[l])
        ms, kn, vn, rn = token_mixers(hs, pos_s, state_ret[l].astype(jnp.float32), k_past, v_past, DS, DS,
                                      w_in[l], w_ret_o[l], w_att_o[l], w_o[l])
        ys = ys + ms
        ys = ys + hier_moe(rmsnorm(ys, norm_ffn[l]), w_router_group[l], b_router_group[l],
                           w_router_expert[l], b_router_expert[l], w_gate[l], w_up[l], w_down[l])
        kp_l.append(kp)
        vp_l.append(vp)
        rp_l.append(rp.astype(x_prompt.dtype))
        ks_l.append(kn.astype(cache_k.dtype))
        vs_l.append(vn.astype(cache_v.dtype))
        rs_l.append(rn.astype(state_ret.dtype))
    y_prompt = rmsnorm(yp, norm_final)
    y_sample = rmsnorm(ys, norm_final)
    k_prompt = jnp.stack(kp_l)
    v_prompt = jnp.stack(vp_l)
    ret_prompt = jnp.stack(rp_l)
    k_sample = jnp.stack(ks_l)
    v_sample = jnp.stack(vs_l)
    ret_sample = jnp.stack(rs_l)
    return (y_prompt, y_sample, k_prompt, v_prompt, ret_prompt, k_sample, v_sample, ret_sample)
```

```python
import functools

import jax
import jax.numpy as jnp
from jax import lax
from jax.experimental import pallas as pl
from jax.experimental.pallas import tpu as pltpu

D_MODEL = 1024
PAGE_SIZE = 128
R_HEADS = 4
R_DK = 128
R_DV = 256
R_CHUNK = 128
A_HEADS = 8
A_DH = 128
MOBA_BLOCK = 256
MOBA_TOPK = 3
ROPE_THETA = 10000.0
N_GROUPS = 4
EXPERTS_PER_GROUP = 8
N_EXPERTS = N_GROUPS * EXPERTS_PER_GROUP
D_EXPERT = 256
EPS = 1e-6

R_QK_W = R_HEADS * R_DK
R_V_W = R_HEADS * R_DV
A_W = A_HEADS * A_DH
D_IN = 2 * R_QK_W + 2 * R_V_W + 3 * A_W + 2 * D_MODEL
OFF_RQ = 0
OFF_RK = OFF_RQ + R_QK_W
OFF_RV = OFF_RK + R_QK_W
OFF_RG = OFF_RV + R_V_W
OFF_AQ = OFF_RG + R_V_W
OFF_AK = OFF_AQ + A_W
OFF_AV = OFF_AK + A_W
OFF_GR = OFF_AV + A_W
OFF_GA = OFF_GR + D_MODEL

LANES = 128
ROUTER_LANES = LANES
EXPERT_LANE0 = N_GROUPS
PAGES_PER_STEP = 8
VMEM_LIMIT = 56 * 1024 * 1024

F32 = jnp.float32
BF16 = jnp.bfloat16
HI = lax.Precision.HIGHEST
NT = (((1,), (1,)), ((), ()))


def _params(sem, vmem=VMEM_LIMIT):
    return pltpu.CompilerParams(dimension_semantics=sem, vmem_limit_bytes=vmem)


def _rmsnorm(x, g):
    return x * lax.rsqrt(jnp.mean(x * x, axis=-1, keepdims=True) + EPS) * g


def _rope_head(z, cs, sn):
    return z * cs + pltpu.roll(z, A_DH // 2, 1) * sn


def _inproj_kernel(x_ref, g_ref, w_ref, cs_ref, sn_ref,
                   rq_ref, rk_ref, rv_ref, rg_ref, aq_ref, k_ref, v_ref, gr_ref, ga_ref):
    h = _rmsnorm(x_ref[...], g_ref[...]).astype(BF16)
    cs = cs_ref[...]
    sn = sn_ref[...]

    def z_cols(c0):
        return jnp.dot(h, w_ref[:, c0:c0 + 2 * LANES], preferred_element_type=F32)

    def plain(off, width, out_ref):
        for c in range(0, width, 2 * LANES):
            out_ref[:, c:c + 2 * LANES] = z_cols(off + c).astype(out_ref.dtype)

    def roped(off, width, out_ref, scale):
        for c in range(0, width, 2 * LANES):
            z = z_cols(off + c)
            for hh in range(2):
                r = _rope_head(z[:, hh * LANES:(hh + 1) * LANES], cs, sn)
                if scale is not None:
                    r = r * scale
                out_ref[:, c + hh * LANES:c + (hh + 1) * LANES] = r.astype(out_ref.dtype)

    roped(OFF_RQ, R_QK_W, rq_ref, R_DK ** -0.5)
    roped(OFF_RK, R_QK_W, rk_ref, None)
    plain(OFF_RV, R_V_W, rv_ref)
    plain(OFF_RG, R_V_W, rg_ref)
    roped(OFF_AQ, A_W, aq_ref, A_DH ** -0.5)
    roped(OFF_AK, A_W, k_ref, None)
    plain(OFF_AV, A_W, v_ref)
    plain(OFF_GR, D_MODEL, gr_ref)
    plain(OFF_GA, D_MODEL, ga_ref)


def _inproj(x, g, w_bf, cs, sn, seq, tm=256):
    t = x.shape[0]
    nseq = seq // tm
    row = lambda i: (i, 0)
    tab = lambda i: (i % nseq, 0)
    outs = [(R_QK_W, BF16), (R_QK_W, BF16), (R_V_W, BF16), (R_V_W, F32), (A_W, BF16),
            (A_W, F32), (A_W, F32), (D_MODEL, F32), (D_MODEL, F32)]
    return pl.pallas_call(
        _inproj_kernel,
        grid=(t // tm,),
        in_specs=[pl.BlockSpec((tm, D_MODEL), row),
                  pl.BlockSpec((1, D_MODEL), lambda i: (0, 0)),
                  pl.BlockSpec((D_MODEL, D_IN), lambda i: (0, 0), pipeline_mode=pl.Buffered(1)),
                  pl.BlockSpec((tm, LANES), tab),
                  pl.BlockSpec((tm, LANES), tab)],
        out_specs=[pl.BlockSpec((tm, w), row) for w, _ in outs],
        out_shape=[jax.ShapeDtypeStruct((t, w), d) for w, d in outs],
        compiler_params=_params(("parallel",)),
        name="inproj",
    )(x, g, w_bf, cs, sn)


def _retention_kernel(q_ref, k_ref, v_ref, g_ref, dm_ref, cd_ref, kd_ref, sd_ref, o_ref, st_ref, s_scr):
    nchunk = q_ref.shape[0] // R_CHUNK
    s_scr[...] = jnp.zeros_like(s_scr)
    dmask = dm_ref[0]
    cdec = cd_ref[0]
    kdec = kd_ref[0]
    sdec = sd_ref[0]

    def chunk(c, carry):
        r0 = pl.multiple_of(c * R_CHUNK, R_CHUNK)
        q = q_ref[pl.ds(r0, R_CHUNK), :]
        k = k_ref[pl.ds(r0, R_CHUNK), :]
        v = v_ref[pl.ds(r0, R_CHUNK), :]
        s = s_scr[...]
        scores = lax.dot_general(q, k, NT, preferred_element_type=F32) * dmask
        inner = jnp.dot(scores.astype(BF16), v, preferred_element_type=F32)
        cross = jnp.dot(q, s.astype(BF16), preferred_element_type=F32) * cdec
        k_dec_t = (k.astype(F32) * kdec).T.astype(BF16)
        s_scr[...] = s * sdec + jnp.dot(k_dec_t, v, preferred_element_type=F32)
        o = inner + cross
        mu = jnp.mean(o, axis=-1, keepdims=True)
        d = o - mu
        var = jnp.mean(d * d, axis=-1, keepdims=True)
        y = d * lax.rsqrt(var + EPS)
        gate = g_ref[pl.ds(r0, R_CHUNK), :]
        o_ref[pl.ds(r0, R_CHUNK), :] = (gate * jax.nn.sigmoid(gate) * y).astype(o_ref.dtype)
        return carry

    lax.fori_loop(0, nchunk, chunk, 0)
    st_ref[0, 0] = s_scr[...]


def _retention(rq, rk, rv, rg, tabs, nb, seq):
    dm, cd, kd, sd = tabs
    head_tab = lambda shp: pl.BlockSpec((1,) + shp, lambda b, h: (h, 0, 0))
    return pl.pallas_call(
        _retention_kernel,
        grid=(nb, R_HEADS),
        in_specs=[pl.BlockSpec((seq, R_DK), lambda b, h: (b, h)),
                  pl.BlockSpec((seq, R_DK), lambda b, h: (b, h)),
                  pl.BlockSpec((seq, R_DV), lambda b, h: (b, h)),
                  pl.BlockSpec((seq, R_DV), lambda b, h: (b, h)),
                  head_tab((R_CHUNK, R_CHUNK)), head_tab((R_CHUNK, R_DV)),
                  head_tab((R_CHUNK, R_DK)), head_tab((R_DK, R_DV))],
        out_specs=[pl.BlockSpec((seq, R_DV), lambda b, h: (b, h)),
                   pl.BlockSpec((1, 1, R_DK, R_DV), lambda b, h: (b, h, 0, 0))],
        out_shape=[jax.ShapeDtypeStruct((nb * seq, R_V_W), BF16),
                   jax.ShapeDtypeStruct((nb, R_HEADS, R_DK, R_DV), F32)],
        scratch_shapes=[pltpu.VMEM((R_DK, R_DV), F32)],
        compiler_params=_params(("parallel", "parallel")),
        name="retention",
    )(rq, rk, rv, rg, dm, cd, kd, sd)


def _retention_tables():
    ld = jnp.log(1.0 - 2.0 ** (-5.0 - jnp.arange(R_HEADS, dtype=F32)))[:, None, None]
    i = jnp.arange(R_CHUNK, dtype=F32)
    diff = i[:, None] - i[None, :]
    dmask = jnp.where(diff >= 0, jnp.exp(ld * jnp.maximum(diff, 0.0)), 0.0)
    cdec = jnp.broadcast_to(jnp.exp(ld * (i + 1.0)[None, :, None]), (R_HEADS, R_CHUNK, R_DV))
    kdec = jnp.broadcast_to(jnp.exp(ld * (R_CHUNK - 1.0 - i)[None, :, None]), (R_HEADS, R_CHUNK, R_DK))
    sdec = jnp.broadcast_to(jnp.exp(ld * R_CHUNK), (R_HEADS, R_DK, R_DV))
    return dmask, cdec, kdec, sdec


def _moba_kernel(q_ref, k_ref, v_ref, o_ref, kb_scr, vt_scr, km_scr, sel_scr):
    i = pl.program_id(2)
    nblk = kb_scr.shape[0]

    @pl.when(i == 0)
    def _():
        for n in range(nblk):
            kn = k_ref[n * MOBA_BLOCK:(n + 1) * MOBA_BLOCK, :]
            kb_scr[n] = kn.astype(BF16)
            km_scr[n:n + 1, :] = jnp.mean(kn, axis=0, keepdims=True)
            vt_scr[n] = v_ref[n * MOBA_BLOCK:(n + 1) * MOBA_BLOCK, :].T.astype(BF16)

    q = q_ref[...]
    gate = lax.dot_general(km_scr[...], q.astype(F32), NT, precision=HI, preferred_element_type=F32)
    blk = lax.broadcasted_iota(jnp.int32, gate.shape, 0)
    past = blk < i
    gate = jnp.where(past, gate, -jnp.inf)
    rank = jnp.zeros(gate.shape, jnp.int32)
    for m in range(nblk):
        gm = gate[m:m + 1, :]
        ahead = (gm > gate) | ((gm == gate) & (m < blk))
        rank = rank + ahead.astype(jnp.int32)
    sel_scr[...] = jnp.where(past & (rank < MOBA_TOPK), 1.0, 0.0)

    s = lax.dot_general(kb_scr[i], q, NT, preferred_element_type=F32)
    kpos = lax.broadcasted_iota(jnp.int32, s.shape, 0)
    qpos = lax.broadcasted_iota(jnp.int32, s.shape, 1)
    s = jnp.where(kpos <= qpos, s, -jnp.inf)
    m0 = jnp.max(s, axis=0, keepdims=True)
    p = jnp.exp(s - m0)
    l0 = jnp.sum(p, axis=0, keepdims=True)
    acc0 = jnp.dot(vt_scr[i], p.astype(BF16), preferred_element_type=F32)

    def past_block(n, carry):
        m_run, l_run, acc = carry
        sn = lax.dot_general(kb_scr[n], q, NT, preferred_element_type=F32)
        sn = jnp.where(sel_scr[pl.ds(n, 1), :] > 0.0, sn, -jnp.inf)
        m_new = jnp.maximum(m_run, jnp.max(sn, axis=0, keepdims=True))
        alpha = jnp.exp(m_run - m_new)
        pn = jnp.exp(sn - m_new)
        l_new = l_run * alpha + jnp.sum(pn, axis=0, keepdims=True)
        acc_new = acc * alpha + jnp.dot(vt_scr[n], pn.astype(BF16), preferred_element_type=F32)
        return m_new, l_new, acc_new

    _, l_fin, acc = lax.fori_loop(0, i, past_block, (m0, l0, acc0))
    o_ref[...] = (acc / l_fin).T.astype(o_ref.dtype)


def _moba(aq, k, v, nb, seq):
    nblk = seq // MOBA_BLOCK
    return pl.pallas_call(
        _moba_kernel,
        grid=(nb, A_HEADS, nblk),
        in_specs=[pl.BlockSpec((MOBA_BLOCK, A_DH), lambda b, h, i: (b * nblk + i, h)),
                  pl.BlockSpec((seq, A_DH), lambda b, h, i: (b, h)),
                  pl.BlockSpec((seq, A_DH), lambda b, h, i: (b, h))],
        out_specs=pl.BlockSpec((MOBA_BLOCK, A_DH), lambda b, h, i: (b * nblk + i, h)),
        out_shape=jax.ShapeDtypeStruct((nb * seq, A_W), BF16),
        scratch_shapes=[pltpu.VMEM((nblk, MOBA_BLOCK, A_DH), BF16),
                        pltpu.VMEM((nblk, A_DH, MOBA_BLOCK), BF16),
                        pltpu.VMEM((nblk, A_DH), F32),
                        pltpu.VMEM((nblk, MOBA_BLOCK), F32)],
        compiler_params=_params(("parallel", "parallel", "arbitrary")),
        name="moba",
    )(aq, k, v)


def _route(logits):
    lane = lax.broadcasted_iota(jnp.int32, logits.shape, 1)
    ninf = -jnp.inf
    big = jnp.int32(ROUTER_LANES)
    glog = jnp.where(lane < N_GROUPS, logits, ninf)
    gmax = jnp.max(glog, axis=-1, keepdims=True)
    g_idx = jnp.min(jnp.where(glog == gmax, lane, big), axis=-1, keepdims=True)
    g_w = 1.0 / jnp.sum(jnp.exp(glog - gmax), axis=-1, keepdims=True)
    e_lo = EXPERT_LANE0 + g_idx * EXPERTS_PER_GROUP
    in_group = (lane >= e_lo) & (lane < e_lo + EXPERTS_PER_GROUP)
    e1 = jnp.where(in_group, logits, ninf)
    v1 = jnp.max(e1, axis=-1, keepdims=True)
    i1 = jnp.min(jnp.where(e1 == v1, lane, big), axis=-1, keepdims=True)
    e2 = jnp.where(lane == i1, ninf, e1)
    v2 = jnp.max(e2, axis=-1, keepdims=True)
    i2 = jnp.min(jnp.where(e2 == v2, lane, big), axis=-1, keepdims=True)
    x2 = jnp.exp(v2 - v1)
    den = 1.0 + x2
    return jnp.where(lane == i1, g_w * (1.0 / den), jnp.where(lane == i2, g_w * (x2 / den), 0.0))


def _outproj_kernel(precise, ret_ref, att_ref, gr_ref, ga_ref, x_ref, wr_ref, wa_ref, wo_ref,
                    gf_ref, wrt_ref, brt_ref, y1_ref, h2_ref, comb_ref):
    prec = HI if precise else None
    mm_dtype = F32 if precise else BF16
    r = jnp.dot(ret_ref[...], wr_ref[...], precision=prec, preferred_element_type=F32)
    a = jnp.dot(att_ref[...], wa_ref[...], precision=prec, preferred_element_type=F32)
    merged = jax.nn.sigmoid(gr_ref[...]) * r + jax.nn.sigmoid(ga_ref[...]) * a
    m = jnp.dot(merged.astype(mm_dtype), wo_ref[...], precision=prec, preferred_element_type=F32)
    y1 = x_ref[...] + m
    y1_ref[...] = y1
    h2 = _rmsnorm(y1, gf_ref[...])
    h2_ref[...] = h2.astype(h2_ref.dtype)
    logits = jnp.dot(h2, wrt_ref[...], precision=HI, preferred_element_type=F32) + brt_ref[...]
    comb_ref[...] = _route(logits)


def _outproj(ret, att, gr, ga, x, w_ret_o, w_att_o, w_o, g_ffn, w_router, b_router, precise, tm):
    t = x.shape[0]
    row = lambda i: (i, 0)
    full = lambda i: (0, 0)
    wspec = lambda: pl.BlockSpec((D_MODEL, D_MODEL), full, pipeline_mode=pl.Buffered(1))
    return pl.pallas_call(
        functools.partial(_outproj_kernel, precise),
        grid=(t // tm,),
        in_specs=[pl.BlockSpec((tm, D_MODEL), row)] * 5 + [wspec(), wspec(), wspec(),
                  pl.BlockSpec((1, D_MODEL), full),
                  pl.BlockSpec((D_MODEL, ROUTER_LANES), full),
                  pl.BlockSpec((1, ROUTER_LANES), full)],
        out_specs=[pl.BlockSpec((tm, D_MODEL), row), pl.BlockSpec((tm, D_MODEL), row),
                   pl.BlockSpec((tm, ROUTER_LANES), row)],
        out_shape=[jax.ShapeDtypeStruct((t, D_MODEL), F32), jax.ShapeDtypeStruct((t, D_MODEL), BF16),
                   jax.ShapeDtypeStruct((t, ROUTER_LANES), F32)],
        compiler_params=_params(("parallel",)),
        name="outproj_precise" if precise else "outproj",
    )(ret, att, gr, ga, x, w_ret_o, w_att_o, w_o, g_ffn, w_router, b_router)


def _moe_kernel(epb, h_ref, comb_ref, y1_ref, wg_ref, wu_ref, wd_ref, gf_ref, o_ref, acc_ref):
    e = pl.program_id(1)

    @pl.when(e == 0)
    def _():
        acc_ref[...] = jnp.zeros_like(acc_ref)

    h = h_ref[...]
    comb = comb_ref[...]
    lane = lax.broadcasted_iota(jnp.int32, comb.shape, 1)
    for j in range(epb):
        c = jnp.sum(jnp.where(lane == EXPERT_LANE0 + e * epb + j, comb, 0.0), axis=-1, keepdims=True)
        a = jnp.dot(h, wg_ref[j], preferred_element_type=F32)
        u = jnp.dot(h, wu_ref[j], preferred_element_type=F32)
        act = (a * jax.nn.sigmoid(a)) * u * c
        acc_ref[...] += jnp.dot(act.astype(BF16), wd_ref[j], preferred_element_type=F32)

    @pl.when(e == pl.num_programs(1) - 1)
    def _():
        o_ref[...] = _rmsnorm(y1_ref[...] + acc_ref[...], gf_ref[...])


def _moe(h2, comb, y1, wg, wu, wd, g_final, tm, epb=4):
    t = h2.shape[0]
    row = lambda i, e: (i, 0)
    return pl.pallas_call(
        functools.partial(_moe_kernel, epb),
        grid=(t // tm, N_EXPERTS // epb),
        in_specs=[pl.BlockSpec((tm, D_MODEL), row), pl.BlockSpec((tm, ROUTER_LANES), row),
                  pl.BlockSpec((tm, D_MODEL), row),
                  pl.BlockSpec((epb, D_MODEL, D_EXPERT), lambda i, e: (e, 0, 0)),
                  pl.BlockSpec((epb, D_MODEL, D_EXPERT), lambda i, e: (e, 0, 0)),
                  pl.BlockSpec((epb, D_EXPERT, D_MODEL), lambda i, e: (e, 0, 0)),
                  pl.BlockSpec((1, D_MODEL), lambda i, e: (0, 0))],
        out_specs=pl.BlockSpec((tm, D_MODEL), row),
        out_shape=jax.ShapeDtypeStruct((t, D_MODEL), F32),
        scratch_shapes=[pltpu.VMEM((tm, D_MODEL), F32)],
        compiler_params=_params(("parallel", "arbitrary")),
        name="moe",
    )(h2, comb, y1, wg, wu, wd, g_final)


def _inproj_sample_kernel(x_ref, g_ref, w_ref, ta_ref, tb_ref, o_ref):
    h = _rmsnorm(x_ref[...], g_ref[...])
    z = jnp.dot(h, w_ref[...], precision=HI, preferred_element_type=F32)
    for c in range(0, z.shape[1], LANES):
        zh = z[:, c:c + LANES]
        o_ref[:, c:c + LANES] = zh * ta_ref[:, c:c + LANES] + pltpu.roll(zh, A_DH // 2, 1) * tb_ref[:, c:c + LANES]


def _inproj_sample(x, g, w_in, ta, tb, tn=512):
    rows = x.shape[0]
    return pl.pallas_call(
        _inproj_sample_kernel,
        grid=(D_IN // tn,),
        in_specs=[pl.BlockSpec((rows, D_MODEL), lambda j: (0, 0)),
                  pl.BlockSpec((1, D_MODEL), lambda j: (0, 0)),
                  pl.BlockSpec((D_MODEL, tn), lambda j: (0, j)),
                  pl.BlockSpec((rows, tn), lambda j: (0, j)),
                  pl.BlockSpec((rows, tn), lambda j: (0, j))],
        out_specs=pl.BlockSpec((rows, tn), lambda j: (0, j)),
        out_shape=jax.ShapeDtypeStruct((rows, D_IN), F32),
        compiler_params=_params(("parallel",)),
        name="inproj_sample",
    )(x, g, w_in, ta, tb)


def _sample_tables(rows, pos):
    half = A_DH // 2
    inv = ROPE_THETA ** (-jnp.arange(half, dtype=F32) / half)
    ang = jnp.full((rows, 1), pos, F32) * inv[None, :]
    cs = jnp.concatenate([jnp.cos(ang), jnp.cos(ang)], axis=-1)
    sn = jnp.concatenate([-jnp.sin(ang), jnp.sin(ang)], axis=-1)
    ones = jnp.ones((rows, LANES), F32)
    zeros = jnp.zeros((rows, LANES), F32)
    ta, tb = [], []
    for off, width, kind in ((OFF_RQ, R_QK_W, R_DK ** -0.5), (OFF_RK, R_QK_W, 1.0), (OFF_RV, 2 * R_V_W, None),
                             (OFF_AQ, A_W, A_DH ** -0.5), (OFF_AK, A_W, 1.0), (OFF_AV, A_W + 2 * D_MODEL, None)):
        for _ in range(width // LANES):
            ta.append(ones if kind is None else cs * kind)
            tb.append(zeros if kind is None else sn * kind)
    return jnp.concatenate(ta, axis=-1), jnp.concatenate(tb, axis=-1)


def _retention_step_kernel(q_ref, k_ref, v_ref, g_ref, s_ref, dec_ref, o_ref, sn_ref):
    for h in range(R_HEADS):
        q = q_ref[0, h]
        k = k_ref[0, h]
        v = v_ref[0, h]
        s = s_ref[0, h]
        dec = dec_ref[h]
        cross = jnp.sum(q * s, axis=0, keepdims=True) * dec
        inner = jnp.sum(q * k, axis=0, keepdims=True) * v
        sn_ref[0, h] = s * dec + k * v
        o = inner + cross
        mu = jnp.mean(o, axis=-1, keepdims=True)
        d = o - mu
        var = jnp.mean(d * d, axis=-1, keepdims=True)
        gate = g_ref[0, h]
        o_ref[0, h] = gate * jax.nn.sigmoid(gate) * (d * lax.rsqrt(var + EPS))


def _retention_step(q, k, v, g, state, dec):
    nb = q.shape[0]
    col = pl.BlockSpec((1, R_HEADS, R_DK, 1), lambda b: (b, 0, 0, 0))
    rowv = pl.BlockSpec((1, R_HEADS, 1, R_DV), lambda b: (b, 0, 0, 0))
    st = pl.BlockSpec((1, R_HEADS, R_DK, R_DV), lambda b: (b, 0, 0, 0))
    return pl.pallas_call(
        _retention_step_kernel,
        grid=(nb,),
        in_specs=[col, col, rowv, rowv, st, pl.BlockSpec((R_HEADS, 1, R_DV), lambda b: (0, 0, 0))],
        out_specs=[rowv, st],
        out_shape=[jax.ShapeDtypeStruct((nb, R_HEADS, 1, R_DV), F32),
                   jax.ShapeDtypeStruct((nb, R_HEADS, R_DK, R_DV), F32)],
        compiler_params=_params(("parallel",)),
        name="retention_step",
    )(q, k, v, g, state, dec)


def _block_sum_kernel(pt_ref, *refs):
    pages, o_ref = refs[:-1], refs[-1]
    per_blk = MOBA_BLOCK // PAGE_SIZE
    for n in range(len(pages) // per_blk):
        acc = jnp.sum(pages[n * per_blk][0], axis=0, keepdims=True)
        for p in range(1, per_blk):
            acc = acc + jnp.sum(pages[n * per_blk + p][0], axis=0, keepdims=True)
        o_ref[0, 0, n:n + 1, :] = acc


def _block_sums(cache_k, pt_flat, nb, n_pages):
    steps = n_pages // PAGES_PER_STEP
    blk_per_step = PAGES_PER_STEP * PAGE_SIZE // MOBA_BLOCK

    def page_spec(p):
        return pl.BlockSpec((1, PAGE_SIZE, A_W),
                            lambda b, j, pt: (pt[b * n_pages + j * PAGES_PER_STEP + p], 0, 0))

    grid_spec = pltpu.PrefetchScalarGridSpec(
        num_scalar_prefetch=1,
        grid=(nb, steps),
        in_specs=[page_spec(p) for p in range(PAGES_PER_STEP)],
        out_specs=pl.BlockSpec((1, 1, blk_per_step, A_W), lambda b, j, pt: (b, j, 0, 0)),
    )
    out = pl.pallas_call(
        _block_sum_kernel,
        grid_spec=grid_spec,
        out_shape=jax.ShapeDtypeStruct((nb, steps, blk_per_step, A_W), F32),
        compiler_params=_params(("parallel", "parallel")),
        name="cache_block_sums",
    )(pt_flat, *([cache_k] * PAGES_PER_STEP))
    return out.reshape(nb, steps * blk_per_step, A_W)


def _block_pick_kernel(q_ref, ks_ref, o_ref):
    ks = ks_ref[0] * (1.0 / MOBA_BLOCK)
    prod = ks * q_ref[0]
    nblk = prod.shape[0]
    lane = lax.broadcasted_iota(jnp.int32, (nblk, LANES), 1)
    gate = jnp.full((nblk, LANES), -jnp.inf, F32)
    for h in range(A_HEADS):
        gate = jnp.where(lane == h, jnp.sum(prod[:, h * A_DH:(h + 1) * A_DH], axis=-1, keepdims=True), gate)
    blk = lax.broadcasted_iota(jnp.int32, gate.shape, 0)
    rank = jnp.zeros(gate.shape, jnp.int32)
    for m in range(nblk):
        gm = gate[m:m + 1, :]
        ahead = (gm > gate) | ((gm == gate) & (m < blk))
        rank = rank + ahead.astype(jnp.int32)
    for j in range(MOBA_TOPK):
        o_ref[0, j:j + 1, :] = jnp.sum(jnp.where(rank == j, blk, 0), axis=0, keepdims=True)


def _block_pick(q, ksum):
    nb, nblk, _ = ksum.shape
    return pl.pallas_call(
        _block_pick_kernel,
        grid=(nb,),
        in_specs=[pl.BlockSpec((1, 1, A_W), lambda b: (b, 0, 0)),
                  pl.BlockSpec((1, nblk, A_W), lambda b: (b, 0, 0))],
        out_specs=pl.BlockSpec((1, MOBA_TOPK, LANES), lambda b: (b, 0, 0)),
        out_shape=jax.ShapeDtypeStruct((nb, MOBA_TOPK, LANES), jnp.int32),
        compiler_params=_params(("parallel",)),
        name="block_pick",
    )(q, ksum)


def _attend_sample_kernel(n_pages, pt_ref, idx_ref, q_ref, kn_ref, vn_ref, *refs):
    k_pages, v_pages, o_ref = refs[:n_pages], refs[n_pages:2 * n_pages], refs[-1]
    q = q_ref[0, 0]
    s_own = jnp.sum(q * kn_ref[0, 0], axis=-1, keepdims=True)
    scores = [jnp.sum(kp[0] * q, axis=-1, keepdims=True) for kp in k_pages]
    m = s_own
    for s in scores:
        m = jnp.maximum(m, jnp.max(s, axis=0, keepdims=True))
    p_own = jnp.exp(s_own - m)
    l = p_own
    acc = p_own * vn_ref[0, 0]
    for s, vp in zip(scores, v_pages):
        p = jnp.exp(s - m)
        l = l + jnp.sum(p, axis=0, keepdims=True)
        acc = acc + jnp.sum(p * vp[0], axis=0, keepdims=True)
    o_ref[0, 0] = acc / l


def _attend_sample(q, k_new, v_new, cache_k, cache_v, pt_flat, picks_flat, pages_per_seq):
    nb = q.shape[0]
    per_blk = MOBA_BLOCK // PAGE_SIZE
    n_pages = MOBA_TOPK * per_blk

    def page_spec(j, p):
        def index(b, h, pt, idx):
            blk = idx[(b * MOBA_TOPK + j) * A_HEADS + h]
            return (pt[b * pages_per_seq + blk * per_blk + p], 0, h)
        return pl.BlockSpec((1, PAGE_SIZE, A_DH), index)

    vec = pl.BlockSpec((1, 1, 1, A_DH), lambda b, h, pt, idx: (b, h, 0, 0))
    pages = [page_spec(j, p) for j in range(MOBA_TOPK) for p in range(per_blk)]
    grid_spec = pltpu.PrefetchScalarGridSpec(
        num_scalar_prefetch=2,
        grid=(nb, A_HEADS),
        in_specs=[vec, vec, vec] + pages + pages,
        out_specs=vec,
    )
    return pl.pallas_call(
        functools.partial(_attend_sample_kernel, n_pages),
        grid_spec=grid_spec,
        out_shape=jax.ShapeDtypeStruct((nb, A_HEADS, 1, A_DH), F32),
        compiler_params=_params(("parallel", "parallel")),
        name="attend_sample",
    )(pt_flat, picks_flat, q, k_new, v_new, *([cache_k] * n_pages), *([cache_v] * n_pages))


def _rope_tables(seq):
    half = A_DH // 2
    inv = ROPE_THETA ** (-jnp.arange(half, dtype=F32) / half)
    ang = jnp.arange(seq, dtype=jnp.int32).astype(F32)[:, None] * inv[None, :]
    cos, sin = jnp.cos(ang), jnp.sin(ang)
    return jnp.concatenate([cos, cos], axis=-1), jnp.concatenate([-sin, sin], axis=-1)


def _router_weights(w_rg, b_rg, w_re, b_re):
    pad = ROUTER_LANES - N_GROUPS - N_EXPERTS
    w = jnp.concatenate([w_rg, w_re.reshape(D_MODEL, N_EXPERTS), jnp.zeros((D_MODEL, pad), F32)], axis=-1)
    b = jnp.concatenate([b_rg, b_re.reshape(N_EXPERTS), jnp.zeros((pad,), F32)])[None, :]
    return w, b


def kernel(x_prompt, x_sample, cache_k, cache_v, state_ret, page_table, norm_mix, w_in, w_ret_o, w_att_o, w_o,
           norm_ffn, w_router_group, b_router_group, w_router_expert, b_router_expert, w_gate, w_up, w_down,
           norm_final):
    nb, seq, _ = x_prompt.shape
    db, ds, _ = x_sample.shape
    depth = norm_mix.shape[0]
    past_len = page_table.shape[1] * PAGE_SIZE
    assert depth == 1 and ds == 1
    assert seq % MOBA_BLOCK == 0 and past_len % (PAGES_PER_STEP * PAGE_SIZE) == 0
    assert past_len // MOBA_BLOCK >= MOBA_TOPK
    l = 0
    t = nb * seq

    g_mix = norm_mix[l][None, :]
    g_ffn = norm_ffn[l][None, :]
    g_fin = norm_final[None, :]
    w_router, b_router = _router_weights(w_router_group[l], b_router_group[l], w_router_expert[l], b_router_expert[l])
    w_in_bf = w_in[l].astype(BF16)
    wg_bf, wu_bf, wd_bf = w_gate[l].astype(BF16), w_up[l].astype(BF16), w_down[l].astype(BF16)

    xp = x_prompt.reshape(t, D_MODEL)
    cs, sn = _rope_tables(seq)
    rq, rk, rv, rg, aq, kp, vp, gr, ga = _inproj(xp, g_mix, w_in_bf, cs, sn, seq)
    ret, ret_state = _retention(rq, rk, rv, rg, _retention_tables(), nb, seq)
    att = _moba(aq, kp, vp, nb, seq)
    y1, h2, comb = _outproj(ret, att, gr, ga, xp, w_ret_o[l].astype(BF16), w_att_o[l].astype(BF16),
                            w_o[l].astype(BF16), g_ffn, w_router, b_router, precise=False, tm=256)
    y_prompt = _moe(h2, comb, y1, wg_bf, wu_bf, wd_bf, g_fin, tm=512).reshape(nb, seq, D_MODEL)

    xs = x_sample.reshape(db, D_MODEL)
    ta, tb = _sample_tables(db, past_len)
    zs = _inproj_sample(xs, g_mix, w_in[l], ta, tb)
    seg = lambda off, width: zs[:, off:off + width]
    ld = jnp.log(1.0 - 2.0 ** (-5.0 - jnp.arange(R_HEADS, dtype=F32)))
    dec = jnp.broadcast_to(jnp.exp(ld * 1.0)[:, None, None], (R_HEADS, 1, R_DV))
    ret_s, state_s = _retention_step(
        seg(OFF_RQ, R_QK_W).reshape(db, R_HEADS, R_DK, 1), seg(OFF_RK, R_QK_W).reshape(db, R_HEADS, R_DK, 1),
        seg(OFF_RV, R_V_W).reshape(db, R_HEADS, 1, R_DV), seg(OFF_RG, R_V_W).reshape(db, R_HEADS, 1, R_DV),
        state_ret[l].astype(F32), dec)
    ck = cache_k[l].reshape(-1, PAGE_SIZE, A_W)
    cv = cache_v[l].reshape(-1, PAGE_SIZE, A_W)
    aq_s, ak_s, av_s = seg(OFF_AQ, A_W), seg(OFF_AK, A_W), seg(OFF_AV, A_W)
    n_pages = page_table.shape[1]
    pt_flat = page_table.reshape(-1)
    ksum = _block_sums(ck, pt_flat, db, n_pages)
    picks = _block_pick(aq_s.reshape(db, 1, A_W), ksum)[:, :, :A_HEADS].reshape(-1)
    heads = lambda z: z.reshape(db, A_HEADS, 1, A_DH)
    att_s = _attend_sample(heads(aq_s), heads(ak_s), heads(av_s), ck, cv, pt_flat, picks, n_pages)
    y1_s, h2_s, comb_s = _outproj(ret_s.reshape(db, R_V_W), att_s.reshape(db, A_W), seg(OFF_GR, D_MODEL),
                                  seg(OFF_GA, D_MODEL), xs, w_ret_o[l], w_att_o[l], w_o[l], g_ffn,
                                  w_router, b_router, precise=True, tm=db)
    y_sample = _moe(h2_s, comb_s, y1_s, wg_bf, wu_bf, wd_bf, g_fin, tm=db).reshape(db, ds, D_MODEL)

    k_prompt = kp.reshape(1, nb, seq, A_HEADS, A_DH)
    v_prompt = vp.reshape(1, nb, seq, A_HEADS, A_DH)
    ret_prompt = ret_state[None]
    k_sample = ak_s.reshape(1, db, ds, A_HEADS, A_DH).astype(cache_k.dtype)
    v_sample = av_s.reshape(1, db, ds, A_HEADS, A_DH).astype(cache_v.dtype)
    ret_sample = state_s[None].astype(state_ret.dtype)
    return (y_prompt, y_sample, k_prompt, v_prompt, ret_prompt, k_sample, v_sample, ret_sample)
```

```python
import functools

import jax
import jax.numpy as jnp
from jax import lax
from jax.experimental import pallas as pl
from jax.experimental.pallas import tpu as pltpu

D_MODEL = 1024
PAGE_SIZE = 128
R_HEADS = 4
R_DK = 128
R_DV = 256
R_CHUNK = 128
A_HEADS = 8
A_DH = 128
MOBA_BLOCK = 256
MOBA_TOPK = 3
ROPE_THETA = 10000.0
N_GROUPS = 4
EXPERTS_PER_GROUP = 8
N_EXPERTS = N_GROUPS * EXPERTS_PER_GROUP
D_EXPERT = 256
EPS = 1e-6

R_QK_W = R_HEADS * R_DK
R_V_W = R_HEADS * R_DV
A_W = A_HEADS * A_DH
D_IN = 2 * R_QK_W + 2 * R_V_W + 3 * A_W + 2 * D_MODEL
OFF_RQ = 0
OFF_RK = OFF_RQ + R_QK_W
OFF_RV = OFF_RK + R_QK_W
OFF_RG = OFF_RV + R_V_W
OFF_AQ = OFF_RG + R_V_W
OFF_AK = OFF_AQ + A_W
OFF_AV = OFF_AK + A_W
OFF_GR = OFF_AV + A_W
OFF_GA = OFF_GR + D_MODEL

LANES = 128
ROUTER_LANES = LANES
EXPERT_LANE0 = N_GROUPS
PAGES_PER_STEP = 8
VMEM_LIMIT = 56 * 1024 * 1024

F32 = jnp.float32
BF16 = jnp.bfloat16
HI = lax.Precision.HIGHEST
NT = (((1,), (1,)), ((), ()))


def _params(sem, vmem=VMEM_LIMIT):
    return pltpu.CompilerParams(dimension_semantics=sem, vmem_limit_bytes=vmem)


def _rmsnorm(x, g):
    return x * lax.rsqrt(jnp.mean(x * x, axis=-1, keepdims=True) + EPS) * g


def _rope_head(z, cs, sn):
    return z * cs + pltpu.roll(z, A_DH // 2, 1) * sn


def _inproj_kernel(x_ref, g_ref, w_ref, cs_ref, sn_ref,
                   rq_ref, rk_ref, rv_ref, rg_ref, aq_ref, k_ref, v_ref, gr_ref, ga_ref):
    h = _rmsnorm(x_ref[...], g_ref[...]).astype(BF16)
    cs = cs_ref[...]
    sn = sn_ref[...]

    def z_cols(c0):
        return jnp.dot(h, w_ref[:, c0:c0 + 2 * LANES], preferred_element_type=F32)

    def plain(off, width, out_ref):
        for c in range(0, width, 2 * LANES):
            out_ref[:, c:c + 2 * LANES] = z_cols(off + c).astype(out_ref.dtype)

    def roped(off, width, out_ref, scale):
        for c in range(0, width, 2 * LANES):
            z = z_cols(off + c)
            for hh in range(2):
                r = _rope_head(z[:, hh * LANES:(hh + 1) * LANES], cs, sn)
                if scale is not None:
                    r = r * scale
                out_ref[:, c + hh * LANES:c + (hh + 1) * LANES] = r.astype(out_ref.dtype)

    roped(OFF_RQ, R_QK_W, rq_ref, R_DK ** -0.5)
    roped(OFF_RK, R_QK_W, rk_ref, None)
    plain(OFF_RV, R_V_W, rv_ref)
    plain(OFF_RG, R_V_W, rg_ref)
    roped(OFF_AQ, A_W, aq_ref, A_DH ** -0.5)
    roped(OFF_AK, A_W, k_ref, None)
    plain(OFF_AV, A_W, v_ref)
    plain(OFF_GR, D_MODEL, gr_ref)
    plain(OFF_GA, D_MODEL, ga_ref)


def _inproj(x, g, w_bf, cs, sn, seq, tm=256):
    t = x.shape[0]
    nseq = seq // tm
    row = lambda i: (i, 0)
    tab = lambda i: (i % nseq, 0)
    outs = [(R_QK_W, BF16), (R_QK_W, BF16), (R_V_W, BF16), (R_V_W, F32), (A_W, BF16),
            (A_W, F32), (A_W, F32), (D_MODEL, F32), (D_MODEL, F32)]
    return pl.pallas_call(
        _inproj_kernel,
        grid=(t // tm,),
        in_specs=[pl.BlockSpec((tm, D_MODEL), row),
                  pl.BlockSpec((1, D_MODEL), lambda i: (0, 0)),
                  pl.BlockSpec((D_MODEL, D_IN), lambda i: (0, 0), pipeline_mode=pl.Buffered(1)),
                  pl.BlockSpec((tm, LANES), tab),
                  pl.BlockSpec((tm, LANES), tab)],
        out_specs=[pl.BlockSpec((tm, w), row) for w, _ in outs],
        out_shape=[jax.ShapeDtypeStruct((t, w), d) for w, d in outs],
        compiler_params=_params(("parallel",)),
        name="inproj",
    )(x, g, w_bf, cs, sn)


def _retention_kernel(q_ref, k_ref, v_ref, g_ref, dm_ref, cd_ref, kd_ref, sd_ref, o_ref, st_ref, s_scr):
    nchunk = q_ref.shape[0] // R_CHUNK
    s_scr[...] = jnp.zeros_like(s_scr)
    for c in range(nchunk):
        rows = slice(c * R_CHUNK, (c + 1) * R_CHUNK)
        q = q_ref[rows, :]
        k = k_ref[rows, :]
        v = v_ref[rows, :]
        s = s_scr[...]
        scores = lax.dot_general(q, k, NT, preferred_element_type=F32) * dm_ref[0]
        inner = jnp.dot(scores.astype(BF16), v, preferred_element_type=F32)
        cross = jnp.dot(q, s.astype(BF16), preferred_element_type=F32) * cd_ref[0]
        k_dec_t = (k.astype(F32) * kd_ref[0]).T.astype(BF16)
        s_scr[...] = s * sd_ref[0] + jnp.dot(k_dec_t, v, preferred_element_type=F32)
        o = inner + cross
        mu = jnp.mean(o, axis=-1, keepdims=True)
        d = o - mu
        var = jnp.mean(d * d, axis=-1, keepdims=True)
        y = d * lax.rsqrt(var + EPS)
        gate = g_ref[rows, :]
        o_ref[rows, :] = (gate * jax.nn.sigmoid(gate) * y).astype(o_ref.dtype)
    st_ref[0, 0] = s_scr[...]


def _retention(rq, rk, rv, rg, tabs, nb, seq):
    dm, cd, kd, sd = tabs
    head_tab = lambda shp: pl.BlockSpec((1,) + shp, lambda b, h: (h, 0, 0))
    return pl.pallas_call(
        _retention_kernel,
        grid=(nb, R_HEADS),
        in_specs=[pl.BlockSpec((seq, R_DK), lambda b, h: (b, h)),
                  pl.BlockSpec((seq, R_DK), lambda b, h: (b, h)),
                  pl.BlockSpec((seq, R_DV), lambda b, h: (b, h)),
                  pl.BlockSpec((seq, R_DV), lambda b, h: (b, h)),
                  head_tab((R_CHUNK, R_CHUNK)), head_tab((R_CHUNK, R_DV)),
                  head_tab((R_CHUNK, R_DK)), head_tab((R_DK, R_DV))],
        out_specs=[pl.BlockSpec((seq, R_DV), lambda b, h: (b, h)),
                   pl.BlockSpec((1, 1, R_DK, R_DV), lambda b, h: (b, h, 0, 0))],
        out_shape=[jax.ShapeDtypeStruct((nb * seq, R_V_W), BF16),
                   jax.ShapeDtypeStruct((nb, R_HEADS, R_DK, R_DV), F32)],
        scratch_shapes=[pltpu.VMEM((R_DK, R_DV), F32)],
        compiler_params=_params(("parallel", "parallel")),
        name="retention",
    )(rq, rk, rv, rg, dm, cd, kd, sd)


def _retention_tables():
    ld = jnp.log(1.0 - 2.0 ** (-5.0 - jnp.arange(R_HEADS, dtype=F32)))[:, None, None]
    i = jnp.arange(R_CHUNK, dtype=F32)
    diff = i[:, None] - i[None, :]
    dmask = jnp.where(diff >= 0, jnp.exp(ld * jnp.maximum(diff, 0.0)), 0.0)
    cdec = jnp.broadcast_to(jnp.exp(ld * (i + 1.0)[None, :, None]), (R_HEADS, R_CHUNK, R_DV))
    kdec = jnp.broadcast_to(jnp.exp(ld * (R_CHUNK - 1.0 - i)[None, :, None]), (R_HEADS, R_CHUNK, R_DK))
    sdec = jnp.broadcast_to(jnp.exp(ld * R_CHUNK), (R_HEADS, R_DK, R_DV))
    return dmask, cdec, kdec, sdec


MASK_BIAS = -1e30


def _moba_kernel(q_ref, k_ref, v_ref, o_ref, kb_scr, vb_scr, km_scr, s_scr, p_scr):
    blk_sz = MOBA_BLOCK
    nblk = k_ref.shape[0] // blk_sz
    lane = lax.broadcasted_iota(jnp.int32, (blk_sz, A_DH), 1)
    for n in range(nblk):
        rows = slice(n * blk_sz, (n + 1) * blk_sz)
        kn = k_ref[rows, :]
        kb_scr[rows, 0:A_DH] = kn.astype(BF16)
        kb_scr[rows, A_DH:2 * A_DH] = jnp.where(lane == n, 1.0, 0.0).astype(BF16)
        km_scr[n:n + 1, :] = jnp.mean(kn, axis=0, keepdims=True)
        vb_scr[rows, :] = v_ref[rows, :].astype(BF16)
    kmean = km_scr[...].astype(BF16)

    for i in range(nblk):
        q = q_ref[i * blk_sz:(i + 1) * blk_sz, :]
        gate = lax.dot_general(kmean, q, NT, preferred_element_type=F32)
        blk = lax.broadcasted_iota(jnp.int32, gate.shape, 0)
        rank = jnp.zeros(gate.shape, jnp.int32)
        for m in range(i):
            gm = gate[m:m + 1, :]
            ahead = (gm > gate) | ((gm == gate) & (m < blk))
            rank = rank + ahead.astype(jnp.int32)
        bias_t = jnp.where((blk < i) & (rank >= MOBA_TOPK), MASK_BIAS, 0.0)
        bias_t = jnp.concatenate([bias_t, jnp.zeros((A_DH - nblk, blk_sz), F32)], axis=0)
        q_aug = jnp.concatenate([q, bias_t.T.astype(BF16)], axis=1)

        nk = (i + 1) * blk_sz
        half = blk_sz // 2
        m_run = jnp.full((blk_sz, half), -jnp.inf, F32)
        for n in range(i + 1):
            cols = slice(n * blk_sz, (n + 1) * blk_sz)
            s = lax.dot_general(q_aug, kb_scr[cols, :], NT, preferred_element_type=F32)
            if n == i:
                qpos = lax.broadcasted_iota(jnp.int32, s.shape, 0)
                kpos = lax.broadcasted_iota(jnp.int32, s.shape, 1)
                s = jnp.where(kpos <= qpos, s, -jnp.inf)
            s_scr[:, cols] = s
            m_run = jnp.maximum(m_run, jnp.maximum(s[:, :half], s[:, half:]))
        m_row = jnp.max(m_run, axis=-1, keepdims=True)
        l_run = jnp.zeros((blk_sz, half), F32)
        for n in range(i + 1):
            cols = slice(n * blk_sz, (n + 1) * blk_sz)
            p = jnp.exp(s_scr[:, cols] - m_row)
            l_run = l_run + (p[:, :half] + p[:, half:])
            p_scr[:, cols] = p.astype(BF16)
        l_row = jnp.sum(l_run, axis=-1, keepdims=True)
        o = jnp.dot(p_scr[:, 0:nk], vb_scr[0:nk, :], preferred_element_type=F32)
        o_ref[i * blk_sz:(i + 1) * blk_sz, :] = (o / l_row).astype(o_ref.dtype)


def _moba(aq, k, v, nb, seq):
    nblk = seq // MOBA_BLOCK
    assert nblk <= A_DH
    return pl.pallas_call(
        _moba_kernel,
        grid=(nb, A_HEADS),
        in_specs=[pl.BlockSpec((seq, A_DH), lambda b, h: (b, h)),
                  pl.BlockSpec((seq, A_DH), lambda b, h: (b, h)),
                  pl.BlockSpec((seq, A_DH), lambda b, h: (b, h))],
        out_specs=pl.BlockSpec((seq, A_DH), lambda b, h: (b, h)),
        out_shape=jax.ShapeDtypeStruct((nb * seq, A_W), BF16),
        scratch_shapes=[pltpu.VMEM((seq, 2 * A_DH), BF16),
                        pltpu.VMEM((seq, A_DH), BF16),
                        pltpu.VMEM((nblk, A_DH), F32),
                        pltpu.VMEM((MOBA_BLOCK, seq), F32),
                        pltpu.VMEM((MOBA_BLOCK, seq), BF16)],
        compiler_params=_params(("parallel", "parallel")),
        name="moba",
    )(aq, k, v)


def _route(logits):
    lane = lax.broadcasted_iota(jnp.int32, logits.shape, 1)
    ninf = -jnp.inf
    big = jnp.int32(ROUTER_LANES)
    glog = jnp.where(lane < N_GROUPS, logits, ninf)
    gmax = jnp.max(glog, axis=-1, keepdims=True)
    g_idx = jnp.min(jnp.where(glog == gmax, lane, big), axis=-1, keepdims=True)
    g_w = 1.0 / jnp.sum(jnp.exp(glog - gmax), axis=-1, keepdims=True)
    e_lo = EXPERT_LANE0 + g_idx * EXPERTS_PER_GROUP
    in_group = (lane >= e_lo) & (lane < e_lo + EXPERTS_PER_GROUP)
    e1 = jnp.where(in_group, logits, ninf)
    v1 = jnp.max(e1, axis=-1, keepdims=True)
    i1 = jnp.min(jnp.where(e1 == v1, lane, big), axis=-1, keepdims=True)
    e2 = jnp.where(lane == i1, ninf, e1)
    v2 = jnp.max(e2, axis=-1, keepdims=True)
    i2 = jnp.min(jnp.where(e2 == v2, lane, big), axis=-1, keepdims=True)
    x2 = jnp.exp(v2 - v1)
    den = 1.0 + x2
    return jnp.where(lane == i1, g_w * (1.0 / den), jnp.where(lane == i2, g_w * (x2 / den), 0.0))


def _outproj_kernel(precise, ret_ref, att_ref, gr_ref, ga_ref, x_ref, wr_ref, wa_ref, wo_ref,
                    gf_ref, wrt_ref, brt_ref, y1_ref, h2_ref, comb_ref):
    prec = HI if precise else None
    mm = F32 if precise else BF16
    r = jnp.dot(ret_ref[...].astype(mm), wr_ref[...], precision=prec, preferred_element_type=F32)
    a = jnp.dot(att_ref[...].astype(mm), wa_ref[...], precision=prec, preferred_element_type=F32)
    merged = jax.nn.sigmoid(gr_ref[...]) * r + jax.nn.sigmoid(ga_ref[...]) * a
    m = jnp.dot(merged.astype(mm), wo_ref[...], precision=prec, preferred_element_type=F32)
    y1 = x_ref[...] + m
    y1_ref[...] = y1
    h2 = _rmsnorm(y1, gf_ref[...])
    h2_ref[...] = h2.astype(h2_ref.dtype)
    logits = jnp.dot(h2.astype(mm), wrt_ref[...], precision=prec, preferred_element_type=F32) + brt_ref[...]
    comb_ref[...] = _route(logits)


def _outproj(ret, att, gr, ga, x, w_ret_o, w_att_o, w_o, g_ffn, w_router, b_router, precise, tm):
    t = x.shape[0]
    row = lambda i: (i, 0)
    full = lambda i: (0, 0)
    wspec = lambda: pl.BlockSpec((D_MODEL, D_MODEL), full, pipeline_mode=pl.Buffered(1))
    return pl.pallas_call(
        functools.partial(_outproj_kernel, precise),
        grid=(t // tm,),
        in_specs=[pl.BlockSpec((tm, D_MODEL), row)] * 5 + [wspec(), wspec(), wspec(),
                  pl.BlockSpec((1, D_MODEL), full),
                  pl.BlockSpec(w_router.shape, full),
                  pl.BlockSpec((1, ROUTER_LANES), full)],
        out_specs=[pl.BlockSpec((tm, D_MODEL), row), pl.BlockSpec((tm, D_MODEL), row),
                   pl.BlockSpec((tm, ROUTER_LANES), row)],
        out_shape=[jax.ShapeDtypeStruct((t, D_MODEL), F32), jax.ShapeDtypeStruct((t, D_MODEL), BF16),
                   jax.ShapeDtypeStruct((t, ROUTER_LANES), F32)],
        compiler_params=_params(("parallel",)),
        name="outproj_precise" if precise else "outproj",
    )(ret, att, gr, ga, x, w_ret_o, w_att_o, w_o, g_ffn, w_router, b_router)


def _moe_kernel(epb, h_ref, comb_ref, y1_ref, wg_ref, wu_ref, wd_ref, gf_ref, o_ref, acc_ref):
    e = pl.program_id(1)

    @pl.when(e == 0)
    def _():
        acc_ref[...] = jnp.zeros_like(acc_ref)

    h = h_ref[...]
    comb = comb_ref[...]
    lane = lax.broadcasted_iota(jnp.int32, comb.shape, 1)
    for j in range(epb):
        c = jnp.sum(jnp.where(lane == EXPERT_LANE0 + e * epb + j, comb, 0.0), axis=-1, keepdims=True)
        a = jnp.dot(h, wg_ref[j], preferred_element_type=F32)
        u = jnp.dot(h, wu_ref[j], preferred_element_type=F32)
        act = (a * jax.nn.sigmoid(a)) * u * c
        acc_ref[...] += jnp.dot(act.astype(BF16), wd_ref[j], preferred_element_type=F32)

    @pl.when(e == pl.num_programs(1) - 1)
    def _():
        o_ref[...] = _rmsnorm(y1_ref[...] + acc_ref[...], gf_ref[...])


def _moe(h2, comb, y1, wg, wu, wd, g_final, tm, epb=4):
    t = h2.shape[0]
    row = lambda i, e: (i, 0)
    return pl.pallas_call(
        functools.partial(_moe_kernel, epb),
        grid=(t // tm, N_EXPERTS // epb),
        in_specs=[pl.BlockSpec((tm, D_MODEL), row), pl.BlockSpec((tm, ROUTER_LANES), row),
                  pl.BlockSpec((tm, D_MODEL), row),
                  pl.BlockSpec((epb, D_MODEL, D_EXPERT), lambda i, e: (e, 0, 0)),
                  pl.BlockSpec((epb, D_MODEL, D_EXPERT), lambda i, e: (e, 0, 0)),
                  pl.BlockSpec((epb, D_EXPERT, D_MODEL), lambda i, e: (e, 0, 0)),
                  pl.BlockSpec((1, D_MODEL), lambda i, e: (0, 0))],
        out_specs=pl.BlockSpec((tm, D_MODEL), row),
        out_shape=jax.ShapeDtypeStruct((t, D_MODEL), F32),
        scratch_shapes=[pltpu.VMEM((tm, D_MODEL), F32)],
        compiler_params=_params(("parallel", "arbitrary")),
        name="moe",
    )(h2, comb, y1, wg, wu, wd, g_final)


def _inproj_sample_kernel(x_ref, g_ref, w_ref, ta_ref, tb_ref, o_ref):
    h = _rmsnorm(x_ref[...], g_ref[...])
    z = jnp.dot(h, w_ref[...], precision=HI, preferred_element_type=F32)
    for c in range(0, z.shape[1], LANES):
        zh = z[:, c:c + LANES]
        o_ref[:, c:c + LANES] = zh * ta_ref[:, c:c + LANES] + pltpu.roll(zh, A_DH // 2, 1) * tb_ref[:, c:c + LANES]


def _inproj_sample(x, g, w_in, ta, tb, tn=512):
    rows = x.shape[0]
    return pl.pallas_call(
        _inproj_sample_kernel,
        grid=(D_IN // tn,),
        in_specs=[pl.BlockSpec((rows, D_MODEL), lambda j: (0, 0)),
                  pl.BlockSpec((1, D_MODEL), lambda j: (0, 0)),
                  pl.BlockSpec((D_MODEL, tn), lambda j: (0, j)),
                  pl.BlockSpec((rows, tn), lambda j: (0, j)),
                  pl.BlockSpec((rows, tn), lambda j: (0, j))],
        out_specs=pl.BlockSpec((rows, tn), lambda j: (0, j)),
        out_shape=jax.ShapeDtypeStruct((rows, D_IN), F32),
        compiler_params=_params(("parallel",)),
        name="inproj_sample",
    )(x, g, w_in, ta, tb)


def _sample_tables(rows, pos):
    half = A_DH // 2
    inv = ROPE_THETA ** (-jnp.arange(half, dtype=F32) / half)
    ang = jnp.full((rows, 1), pos, F32) * inv[None, :]
    cs = jnp.concatenate([jnp.cos(ang), jnp.cos(ang)], axis=-1)
    sn = jnp.concatenate([-jnp.sin(ang), jnp.sin(ang)], axis=-1)
    ones = jnp.ones((rows, LANES), F32)
    zeros = jnp.zeros((rows, LANES), F32)
    ta, tb = [], []
    for off, width, kind in ((OFF_RQ, R_QK_W, R_DK ** -0.5), (OFF_RK, R_QK_W, 1.0), (OFF_RV, 2 * R_V_W, None),
                             (OFF_AQ, A_W, A_DH ** -0.5), (OFF_AK, A_W, 1.0), (OFF_AV, A_W + 2 * D_MODEL, None)):
        for _ in range(width // LANES):
            ta.append(ones if kind is None else cs * kind)
            tb.append(zeros if kind is None else sn * kind)
    return jnp.concatenate(ta, axis=-1), jnp.concatenate(tb, axis=-1)


def _retention_step_kernel(q_ref, k_ref, v_ref, g_ref, s_ref, dec_ref, o_ref, sn_ref):
    for h in range(R_HEADS):
        q = q_ref[0, h]
        k = k_ref[0, h]
        v = v_ref[0, h]
        s = s_ref[0, h]
        dec = dec_ref[h]
        cross = jnp.sum(q * s, axis=0, keepdims=True) * dec
        inner = jnp.sum(q * k, axis=0, keepdims=True) * v
        sn_ref[0, h] = s * dec + k * v
        o = inner + cross
        mu = jnp.mean(o, axis=-1, keepdims=True)
        d = o - mu
        var = jnp.mean(d * d, axis=-1, keepdims=True)
        gate = g_ref[0, h]
        o_ref[0, h] = gate * jax.nn.sigmoid(gate) * (d * lax.rsqrt(var + EPS))


def _retention_step(q, k, v, g, state, dec):
    nb = q.shape[0]
    col = pl.BlockSpec((1, R_HEADS, R_DK, 1), lambda b: (b, 0, 0, 0))
    rowv = pl.BlockSpec((1, R_HEADS, 1, R_DV), lambda b: (b, 0, 0, 0))
    st = pl.BlockSpec((1, R_HEADS, R_DK, R_DV), lambda b: (b, 0, 0, 0))
    return pl.pallas_call(
        _retention_step_kernel,
        grid=(nb,),
        in_specs=[col, col, rowv, rowv, st, pl.BlockSpec((R_HEADS, 1, R_DV), lambda b: (0, 0, 0))],
        out_specs=[rowv, st],
        out_shape=[jax.ShapeDtypeStruct((nb, R_HEADS, 1, R_DV), F32),
                   jax.ShapeDtypeStruct((nb, R_HEADS, R_DK, R_DV), F32)],
        compiler_params=_params(("parallel",)),
        name="retention_step",
    )(q, k, v, g, state, dec)


def _block_sum_kernel(pt_ref, *refs):
    pages, o_ref = refs[:-1], refs[-1]
    per_blk = MOBA_BLOCK // PAGE_SIZE
    for n in range(len(pages) // per_blk):
        acc = jnp.sum(pages[n * per_blk][0], axis=0)
        for p in range(1, per_blk):
            acc = acc + jnp.sum(pages[n * per_blk + p][0], axis=0)
        o_ref[0, 0, n] = acc


def _block_sums(cache_k, pt_flat, nb, n_pages):
    steps = n_pages // PAGES_PER_STEP
    blk_per_step = PAGES_PER_STEP * PAGE_SIZE // MOBA_BLOCK

    def page_spec(p):
        return pl.BlockSpec((1, PAGE_SIZE, A_HEADS, A_DH),
                            lambda b, j, pt: (pt[b * n_pages + j * PAGES_PER_STEP + p], 0, 0, 0))

    grid_spec = pltpu.PrefetchScalarGridSpec(
        num_scalar_prefetch=1,
        grid=(nb, steps),
        in_specs=[page_spec(p) for p in range(PAGES_PER_STEP)],
        out_specs=pl.BlockSpec((1, 1, blk_per_step, A_HEADS, A_DH), lambda b, j, pt: (b, j, 0, 0, 0)),
    )
    out = pl.pallas_call(
        _block_sum_kernel,
        grid_spec=grid_spec,
        out_shape=jax.ShapeDtypeStruct((nb, steps, blk_per_step, A_HEADS, A_DH), F32),
        compiler_params=_params(("parallel", "parallel")),
        name="cache_block_sums",
    )(pt_flat, *([cache_k] * PAGES_PER_STEP))
    return out.reshape(nb, steps * blk_per_step, A_W)


def _block_pick_kernel(q_ref, ks_ref, o_ref):
    ks = ks_ref[0] * (1.0 / MOBA_BLOCK)
    prod = ks * q_ref[0]
    nblk = prod.shape[0]
    lane = lax.broadcasted_iota(jnp.int32, (nblk, LANES), 1)
    gate = jnp.full((nblk, LANES), -jnp.inf, F32)
    for h in range(A_HEADS):
        gate = jnp.where(lane == h, jnp.sum(prod[:, h * A_DH:(h + 1) * A_DH], axis=-1, keepdims=True), gate)
    blk = lax.broadcasted_iota(jnp.int32, gate.shape, 0)
    rank = jnp.zeros(gate.shape, jnp.int32)
    for m in range(nblk):
        gm = gate[m:m + 1, :]
        ahead = (gm > gate) | ((gm == gate) & (m < blk))
        rank = rank + ahead.astype(jnp.int32)
    for j in range(MOBA_TOPK):
        o_ref[0, j:j + 1, :] = jnp.sum(jnp.where(rank == j, blk, 0), axis=0, keepdims=True)


def _block_pick(q, ksum):
    nb, nblk, _ = ksum.shape
    return pl.pallas_call(
        _block_pick_kernel,
        grid=(nb,),
        in_specs=[pl.BlockSpec((1, 1, A_W), lambda b: (b, 0, 0)),
                  pl.BlockSpec((1, nblk, A_W), lambda b: (b, 0, 0))],
        out_specs=pl.BlockSpec((1, MOBA_TOPK, LANES), lambda b: (b, 0, 0)),
        out_shape=jax.ShapeDtypeStruct((nb, MOBA_TOPK, LANES), jnp.int32),
        compiler_params=_params(("parallel",)),
        name="block_pick",
    )(q, ksum)


PICK_PAGES = MOBA_TOPK * (MOBA_BLOCK // PAGE_SIZE)


def _attend_sample_kernel(pages_per_seq, pt_ref, idx_ref, q_ref, kn_ref, vn_ref, ck_ref, cv_ref, o_ref,
                          kbuf, vbuf, ksem, vsem):
    b = pl.program_id(0)
    per_blk = MOBA_BLOCK // PAGE_SIZE

    def page_copies(seq_i, slot):
        out = []
        for h in range(A_HEADS):
            for j in range(MOBA_TOPK):
                blk = idx_ref[(seq_i * MOBA_TOPK + j) * A_HEADS + h]
                for p in range(per_blk):
                    page = pt_ref[seq_i * pages_per_seq + blk * per_blk + p]
                    dst = (slot, h, j * per_blk + p)
                    out.append(pltpu.make_async_copy(ck_ref.at[page, :, h, :], kbuf.at[dst], ksem.at[slot]))
                    out.append(pltpu.make_async_copy(cv_ref.at[page, :, h, :], vbuf.at[dst], vsem.at[slot]))
        return out

    slot = b % 2

    @pl.when(b == 0)
    def _():
        for c in page_copies(0, 0):
            c.start()

    @pl.when(b + 1 < pl.num_programs(0))
    def _():
        for c in page_copies(b + 1, 1 - slot):
            c.start()

    for c in page_copies(b, slot):
        c.wait()

    for h in range(A_HEADS):
        q = q_ref[0, h:h + 1, :]
        s_own = jnp.sum(q * kn_ref[0, h:h + 1, :], axis=-1, keepdims=True)
        scores = [jnp.sum(kbuf[slot, h, g] * q, axis=-1, keepdims=True) for g in range(PICK_PAGES)]
        m = s_own
        for s in scores:
            m = jnp.maximum(m, jnp.max(s, axis=0, keepdims=True))
        p_own = jnp.exp(s_own - m)
        l = p_own
        acc = p_own * vn_ref[0, h:h + 1, :]
        for g, s in enumerate(scores):
            p = jnp.exp(s - m)
            l = l + jnp.sum(p, axis=0, keepdims=True)
            acc = acc + jnp.sum(p * vbuf[slot, h, g], axis=0, keepdims=True)
        o_ref[0, h:h + 1, :] = acc / l


def _attend_sample(q, k_new, v_new, cache_k, cache_v, pt_flat, picks_flat, pages_per_seq):
    nb = q.shape[0]
    vec = pl.BlockSpec((1, A_HEADS, A_DH), lambda b, pt, idx: (b, 0, 0))
    hbm = pl.BlockSpec(memory_space=pl.ANY)
    buf = pltpu.VMEM((2, A_HEADS, PICK_PAGES, PAGE_SIZE, A_DH), F32)
    grid_spec = pltpu.PrefetchScalarGridSpec(
        num_scalar_prefetch=2,
        grid=(nb,),
        in_specs=[vec, vec, vec, hbm, hbm],
        out_specs=vec,
        scratch_shapes=[buf, buf, pltpu.SemaphoreType.DMA((2,)), pltpu.SemaphoreType.DMA((2,))],
    )
    return pl.pallas_call(
        functools.partial(_attend_sample_kernel, pages_per_seq),
        grid_spec=grid_spec,
        out_shape=jax.ShapeDtypeStruct((nb, A_HEADS, A_DH), F32),
        compiler_params=_params(("arbitrary",)),
        name="attend_sample",
    )(pt_flat, picks_flat, q, k_new, v_new, cache_k, cache_v)


def _rope_tables(seq):
    half = A_DH // 2
    inv = ROPE_THETA ** (-jnp.arange(half, dtype=F32) / half)
    ang = jnp.arange(seq, dtype=jnp.int32).astype(F32)[:, None] * inv[None, :]
    cos, sin = jnp.cos(ang), jnp.sin(ang)
    return jnp.concatenate([cos, cos], axis=-1), jnp.concatenate([-sin, sin], axis=-1)


def _router_weights(w_rg, b_rg, w_re, b_re):
    pad = ROUTER_LANES - N_GROUPS - N_EXPERTS
    w = jnp.concatenate([w_rg, w_re.reshape(D_MODEL, N_EXPERTS), jnp.zeros((D_MODEL, pad), F32)], axis=-1)
    b = jnp.concatenate([b_rg, b_re.reshape(N_EXPERTS), jnp.zeros((pad,), F32)])[None, :]
    return w, b


def kernel(x_prompt, x_sample, cache_k, cache_v, state_ret, page_table, norm_mix, w_in, w_ret_o, w_att_o, w_o,
           norm_ffn, w_router_group, b_router_group, w_router_expert, b_router_expert, w_gate, w_up, w_down,
           norm_final):
    nb, seq, _ = x_prompt.shape
    db, ds, _ = x_sample.shape
    depth = norm_mix.shape[0]
    past_len = page_table.shape[1] * PAGE_SIZE
    assert depth == 1 and ds == 1
    assert seq % MOBA_BLOCK == 0 and past_len % (PAGES_PER_STEP * PAGE_SIZE) == 0
    assert past_len // MOBA_BLOCK >= MOBA_TOPK
    l = 0
    t = nb * seq

    g_mix = norm_mix[l][None, :]
    g_ffn = norm_ffn[l][None, :]
    g_fin = norm_final[None, :]
    w_router, b_router = _router_weights(w_router_group[l], b_router_group[l], w_router_expert[l], b_router_expert[l])
    w_in_bf = w_in[l].astype(BF16)
    wg_bf, wu_bf, wd_bf = w_gate[l].astype(BF16), w_up[l].astype(BF16), w_down[l].astype(BF16)
    wro_bf, wao_bf, wo_bf = w_ret_o[l].astype(BF16), w_att_o[l].astype(BF16), w_o[l].astype(BF16)

    xp = x_prompt.reshape(t, D_MODEL)
    cs, sn = _rope_tables(seq)
    rq, rk, rv, rg, aq, kp, vp, gr, ga = _inproj(xp, g_mix, w_in_bf, cs, sn, seq)
    ret, ret_state = _retention(rq, rk, rv, rg, _retention_tables(), nb, seq)
    att = _moba(aq, kp, vp, nb, seq)
    y1, h2, comb = _outproj(ret, att, gr, ga, xp, wro_bf, wao_bf, wo_bf, g_ffn, w_router.astype(BF16), b_router,
                            precise=False, tm=256)
    y_prompt = _moe(h2, comb, y1, wg_bf, wu_bf, wd_bf, g_fin, tm=512).reshape(nb, seq, D_MODEL)

    xs = x_sample.reshape(db, D_MODEL)
    ta, tb = _sample_tables(db, past_len)
    zs = _inproj_sample(xs, g_mix, w_in[l], ta, tb)
    seg = lambda off, width: zs[:, off:off + width]
    ld = jnp.log(1.0 - 2.0 ** (-5.0 - jnp.arange(R_HEADS, dtype=F32)))
    dec = jnp.broadcast_to(jnp.exp(ld * 1.0)[:, None, None], (R_HEADS, 1, R_DV))
    ret_s, state_s = _retention_step(
        seg(OFF_RQ, R_QK_W).reshape(db, R_HEADS, R_DK, 1), seg(OFF_RK, R_QK_W).reshape(db, R_HEADS, R_DK, 1),
        seg(OFF_RV, R_V_W).reshape(db, R_HEADS, 1, R_DV), seg(OFF_RG, R_V_W).reshape(db, R_HEADS, 1, R_DV),
        state_ret[l].astype(F32), dec)
    ck, cv = cache_k[l], cache_v[l]
    aq_s, ak_s, av_s = seg(OFF_AQ, A_W), seg(OFF_AK, A_W), seg(OFF_AV, A_W)
    n_pages = page_table.shape[1]
    pt_flat = page_table.reshape(-1)
    ksum = _block_sums(ck, pt_flat, db, n_pages)
    picks = _block_pick(aq_s.reshape(db, 1, A_W), ksum)[:, :, :A_HEADS].reshape(-1)
    heads = lambda z: z.reshape(db, A_HEADS, A_DH)
    att_s = _attend_sample(heads(aq_s), heads(ak_s), heads(av_s), ck, cv, pt_flat, picks, n_pages)
    y1_s, h2_s, comb_s = _outproj(ret_s.reshape(db, R_V_W), att_s.reshape(db, A_W), seg(OFF_GR, D_MODEL),
                                  seg(OFF_GA, D_MODEL), xs, w_ret_o[l], w_att_o[l], w_o[l], g_ffn,
                                  w_router, b_router, precise=True, tm=db)
    y_sample = _moe(h2_s, comb_s, y1_s, wg_bf, wu_bf, wd_bf, g_fin, tm=db).reshape(db, ds, D_MODEL)

    k_prompt = kp.reshape(1, nb, seq, A_HEADS, A_DH)
    v_prompt = vp.reshape(1, nb, seq, A_HEADS, A_DH)
    ret_prompt = ret_state[None]
    k_sample = ak_s.reshape(1, db, ds, A_HEADS, A_DH).astype(cache_k.dtype)
    v_sample = av_s.reshape(1, db, ds, A_HEADS, A_DH).astype(cache_v.dtype)
    ret_sample = state_s[None].astype(state_ret.dtype)
    return (y_prompt, y_sample, k_prompt, v_prompt, ret_prompt, k_sample, v_sample, ret_sample)
```

```python
import functools

import jax
import jax.numpy as jnp
from jax import lax
from jax.experimental import pallas as pl
from jax.experimental.pallas import tpu as pltpu

D_MODEL = 1024
PAGE_SIZE = 128
R_HEADS = 4
R_DK = 128
R_DV = 256
R_CHUNK = 128
A_HEADS = 8
A_DH = 128
MOBA_BLOCK = 256
MOBA_TOPK = 3
ROPE_THETA = 10000.0
N_GROUPS = 4
EXPERTS_PER_GROUP = 8
N_EXPERTS = N_GROUPS * EXPERTS_PER_GROUP
D_EXPERT = 256
EPS = 1e-6

R_QK_W = R_HEADS * R_DK
R_V_W = R_HEADS * R_DV
A_W = A_HEADS * A_DH
D_IN = 2 * R_QK_W + 2 * R_V_W + 3 * A_W + 2 * D_MODEL
OFF_RQ = 0
OFF_RK = OFF_RQ + R_QK_W
OFF_RV = OFF_RK + R_QK_W
OFF_RG = OFF_RV + R_V_W
OFF_AQ = OFF_RG + R_V_W
OFF_AK = OFF_AQ + A_W
OFF_AV = OFF_AK + A_W
OFF_GR = OFF_AV + A_W
OFF_GA = OFF_GR + D_MODEL

LANES = 128
ROUTER_LANES = LANES
EXPERT_LANE0 = N_GROUPS
PAGES_PER_STEP = 8
MOE_TM = 512
MOE_EPB = 4
VMEM_LIMIT = 56 * 1024 * 1024

F32 = jnp.float32
BF16 = jnp.bfloat16
HI = lax.Precision.HIGHEST
NT = (((1,), (1,)), ((), ()))


def _params(sem, vmem=VMEM_LIMIT):
    return pltpu.CompilerParams(dimension_semantics=sem, vmem_limit_bytes=vmem)


def _rmsnorm(x, g):
    return x * lax.rsqrt(jnp.mean(x * x, axis=-1, keepdims=True) + EPS) * g


def _rope_head(z, cs, sn):
    return z * cs + pltpu.roll(z, A_DH // 2, 1) * sn


def _inproj_kernel(x_ref, g_ref, w_ref, cs_ref, sn_ref,
                   rq_ref, rk_ref, rv_ref, rg_ref, aq_ref, k_ref, v_ref, gr_ref, ga_ref):
    h = _rmsnorm(x_ref[...], g_ref[...]).astype(BF16)
    cs = cs_ref[...]
    sn = sn_ref[...]

    def z_cols(c0):
        return jnp.dot(h, w_ref[:, c0:c0 + 2 * LANES], preferred_element_type=F32)

    def plain(off, width, out_ref):
        for c in range(0, width, 2 * LANES):
            out_ref[:, c:c + 2 * LANES] = z_cols(off + c).astype(out_ref.dtype)

    def roped(off, width, out_ref, scale):
        for c in range(0, width, 2 * LANES):
            z = z_cols(off + c)
            for hh in range(2):
                r = _rope_head(z[:, hh * LANES:(hh + 1) * LANES], cs, sn)
                if scale is not None:
                    r = r * scale
                out_ref[:, c + hh * LANES:c + (hh + 1) * LANES] = r.astype(out_ref.dtype)

    roped(OFF_RQ, R_QK_W, rq_ref, R_DK ** -0.5)
    roped(OFF_RK, R_QK_W, rk_ref, None)
    plain(OFF_RV, R_V_W, rv_ref)
    plain(OFF_RG, R_V_W, rg_ref)
    roped(OFF_AQ, A_W, aq_ref, A_DH ** -0.5)
    roped(OFF_AK, A_W, k_ref, None)
    plain(OFF_AV, A_W, v_ref)
    plain(OFF_GR, D_MODEL, gr_ref)
    plain(OFF_GA, D_MODEL, ga_ref)


def _inproj(x, g, w_bf, cs, sn, seq, tm=256):
    t = x.shape[0]
    nseq = seq // tm
    row = lambda i: (i, 0)
    tab = lambda i: (i % nseq, 0)
    outs = [(R_QK_W, BF16), (R_QK_W, BF16), (R_V_W, BF16), (R_V_W, F32), (A_W, BF16),
            (A_W, F32), (A_W, F32), (D_MODEL, F32), (D_MODEL, F32)]
    return pl.pallas_call(
        _inproj_kernel,
        grid=(t // tm,),
        in_specs=[pl.BlockSpec((tm, D_MODEL), row),
                  pl.BlockSpec((1, D_MODEL), lambda i: (0, 0)),
                  pl.BlockSpec((D_MODEL, D_IN), lambda i: (0, 0), pipeline_mode=pl.Buffered(1)),
                  pl.BlockSpec((tm, LANES), tab),
                  pl.BlockSpec((tm, LANES), tab)],
        out_specs=[pl.BlockSpec((tm, w), row) for w, _ in outs],
        out_shape=[jax.ShapeDtypeStruct((t, w), d) for w, d in outs],
        compiler_params=_params(("parallel",)),
        name="inproj",
    )(x, g, w_bf, cs, sn)


def _retention_kernel(q_ref, k_ref, v_ref, g_ref, dm_ref, cd_ref, kd_ref, sd_ref, o_ref, st_ref, s_scr):
    nchunk = q_ref.shape[0] // R_CHUNK
    s_scr[...] = jnp.zeros_like(s_scr)
    for c in range(nchunk):
        rows = slice(c * R_CHUNK, (c + 1) * R_CHUNK)
        q = q_ref[rows, :]
        k = k_ref[rows, :]
        v = v_ref[rows, :]
        s = s_scr[...]
        scores = lax.dot_general(q, k, NT, preferred_element_type=F32) * dm_ref[0]
        inner = jnp.dot(scores.astype(BF16), v, preferred_element_type=F32)
        cross = jnp.dot(q, s.astype(BF16), preferred_element_type=F32) * cd_ref[0]
        k_dec_t = (k.astype(F32) * kd_ref[0]).T.astype(BF16)
        s_scr[...] = s * sd_ref[0] + jnp.dot(k_dec_t, v, preferred_element_type=F32)
        o = inner + cross
        mu = jnp.mean(o, axis=-1, keepdims=True)
        d = o - mu
        var = jnp.mean(d * d, axis=-1, keepdims=True)
        y = d * lax.rsqrt(var + EPS)
        gate = g_ref[rows, :]
        o_ref[rows, :] = (gate * jax.nn.sigmoid(gate) * y).astype(o_ref.dtype)
    st_ref[0, 0] = s_scr[...]


def _retention(rq, rk, rv, rg, tabs, nb, seq):
    dm, cd, kd, sd = tabs
    head_tab = lambda shp: pl.BlockSpec((1,) + shp, lambda b, h: (h, 0, 0))
    return pl.pallas_call(
        _retention_kernel,
        grid=(nb, R_HEADS),
        in_specs=[pl.BlockSpec((seq, R_DK), lambda b, h: (b, h)),
                  pl.BlockSpec((seq, R_DK), lambda b, h: (b, h)),
                  pl.BlockSpec((seq, R_DV), lambda b, h: (b, h)),
                  pl.BlockSpec((seq, R_DV), lambda b, h: (b, h)),
                  head_tab((R_CHUNK, R_CHUNK)), head_tab((R_CHUNK, R_DV)),
                  head_tab((R_CHUNK, R_DK)), head_tab((R_DK, R_DV))],
        out_specs=[pl.BlockSpec((seq, R_DV), lambda b, h: (b, h)),
                   pl.BlockSpec((1, 1, R_DK, R_DV), lambda b, h: (b, h, 0, 0))],
        out_shape=[jax.ShapeDtypeStruct((nb * seq, R_V_W), BF16),
                   jax.ShapeDtypeStruct((nb, R_HEADS, R_DK, R_DV), F32)],
        scratch_shapes=[pltpu.VMEM((R_DK, R_DV), F32)],
        compiler_params=_params(("parallel", "parallel")),
        name="retention",
    )(rq, rk, rv, rg, dm, cd, kd, sd)


def _retention_tables():
    ld = jnp.log(1.0 - 2.0 ** (-5.0 - jnp.arange(R_HEADS, dtype=F32)))[:, None, None]
    i = jnp.arange(R_CHUNK, dtype=F32)
    diff = i[:, None] - i[None, :]
    dmask = jnp.where(diff >= 0, jnp.exp(ld * jnp.maximum(diff, 0.0)), 0.0)
    cdec = jnp.broadcast_to(jnp.exp(ld * (i + 1.0)[None, :, None]), (R_HEADS, R_CHUNK, R_DV))
    kdec = jnp.broadcast_to(jnp.exp(ld * (R_CHUNK - 1.0 - i)[None, :, None]), (R_HEADS, R_CHUNK, R_DK))
    sdec = jnp.broadcast_to(jnp.exp(ld * R_CHUNK), (R_HEADS, R_DK, R_DV))
    return dmask, cdec, kdec, sdec


MASK_BIAS = -1e30


def _moba_kernel(q_ref, k_ref, v_ref, o_ref, kb_scr, vb_scr, km_scr, s_scr, p_scr):
    blk_sz = MOBA_BLOCK
    nblk = k_ref.shape[0] // blk_sz
    lane = lax.broadcasted_iota(jnp.int32, (blk_sz, A_DH), 1)
    for n in range(nblk):
        rows = slice(n * blk_sz, (n + 1) * blk_sz)
        kn = k_ref[rows, :]
        kb_scr[rows, 0:A_DH] = kn.astype(BF16)
        kb_scr[rows, A_DH:2 * A_DH] = jnp.where(lane == n, 1.0, 0.0).astype(BF16)
        km_scr[n:n + 1, :] = jnp.mean(kn, axis=0, keepdims=True)
        vb_scr[rows, :] = v_ref[rows, :].astype(BF16)
    kmean = km_scr[...].astype(BF16)

    for i in range(nblk):
        q = q_ref[i * blk_sz:(i + 1) * blk_sz, :]
        gate = lax.dot_general(kmean, q, NT, preferred_element_type=F32)
        blk = lax.broadcasted_iota(jnp.int32, gate.shape, 0)
        rank = jnp.zeros(gate.shape, jnp.int32)
        for m in range(i):
            gm = gate[m:m + 1, :]
            ahead = (gm > gate) | ((gm == gate) & (m < blk))
            rank = rank + ahead.astype(jnp.int32)
        bias_t = jnp.where((blk < i) & (rank >= MOBA_TOPK), MASK_BIAS, 0.0)
        bias_t = jnp.concatenate([bias_t, jnp.zeros((A_DH - nblk, blk_sz), F32)], axis=0)
        q_aug = jnp.concatenate([q, bias_t.T.astype(BF16)], axis=1)

        nk = (i + 1) * blk_sz
        half = blk_sz // 2
        m_run = jnp.full((blk_sz, half), -jnp.inf, F32)
        for n in range(i + 1):
            cols = slice(n * blk_sz, (n + 1) * blk_sz)
            s = lax.dot_general(q_aug, kb_scr[cols, :], NT, preferred_element_type=F32)
            if n == i:
                qpos = lax.broadcasted_iota(jnp.int32, s.shape, 0)
                kpos = lax.broadcasted_iota(jnp.int32, s.shape, 1)
                s = jnp.where(kpos <= qpos, s, -jnp.inf)
            s_scr[i % 2, :, cols] = s
            m_run = jnp.maximum(m_run, jnp.maximum(s[:, :half], s[:, half:]))
        m_row = jnp.max(m_run, axis=-1, keepdims=True)
        l_run = jnp.zeros((blk_sz, half), F32)
        for n in range(i + 1):
            cols = slice(n * blk_sz, (n + 1) * blk_sz)
            p = jnp.exp(s_scr[i % 2, :, cols] - m_row)
            l_run = l_run + (p[:, :half] + p[:, half:])
            p_scr[i % 2, :, cols] = p.astype(BF16)
        l_row = jnp.sum(l_run, axis=-1, keepdims=True)
        o = jnp.dot(p_scr[i % 2, :, 0:nk], vb_scr[0:nk, :], preferred_element_type=F32)
        o_ref[i * blk_sz:(i + 1) * blk_sz, :] = (o / l_row).astype(o_ref.dtype)


def _moba(aq, k, v, nb, seq):
    nblk = seq // MOBA_BLOCK
    assert nblk <= A_DH
    return pl.pallas_call(
        _moba_kernel,
        grid=(nb, A_HEADS),
        in_specs=[pl.BlockSpec((seq, A_DH), lambda b, h: (b, h)),
                  pl.BlockSpec((seq, A_DH), lambda b, h: (b, h)),
                  pl.BlockSpec((seq, A_DH), lambda b, h: (b, h))],
        out_specs=pl.BlockSpec((seq, A_DH), lambda b, h: (b, h)),
        out_shape=jax.ShapeDtypeStruct((nb * seq, A_W), BF16),
        scratch_shapes=[pltpu.VMEM((seq, 2 * A_DH), BF16),
                        pltpu.VMEM((seq, A_DH), BF16),
                        pltpu.VMEM((nblk, A_DH), F32),
                        pltpu.VMEM((2, MOBA_BLOCK, seq), F32),
                        pltpu.VMEM((2, MOBA_BLOCK, seq), BF16)],
        compiler_params=_params(("parallel", "parallel")),
        name="moba",
    )(aq, k, v)


def _route(logits):
    lane = lax.broadcasted_iota(jnp.int32, logits.shape, 1)
    ninf = -jnp.inf
    big = jnp.int32(ROUTER_LANES)
    glog = jnp.where(lane < N_GROUPS, logits, ninf)
    gmax = jnp.max(glog, axis=-1, keepdims=True)
    g_idx = jnp.min(jnp.where(glog == gmax, lane, big), axis=-1, keepdims=True)
    g_w = 1.0 / jnp.sum(jnp.exp(glog - gmax), axis=-1, keepdims=True)
    e_lo = EXPERT_LANE0 + g_idx * EXPERTS_PER_GROUP
    in_group = (lane >= e_lo) & (lane < e_lo + EXPERTS_PER_GROUP)
    e1 = jnp.where(in_group, logits, ninf)
    v1 = jnp.max(e1, axis=-1, keepdims=True)
    i1 = jnp.min(jnp.where(e1 == v1, lane, big), axis=-1, keepdims=True)
    e2 = jnp.where(lane == i1, ninf, e1)
    v2 = jnp.max(e2, axis=-1, keepdims=True)
    i2 = jnp.min(jnp.where(e2 == v2, lane, big), axis=-1, keepdims=True)
    x2 = jnp.exp(v2 - v1)
    den = 1.0 + x2
    return jnp.where(lane == i1, g_w * (1.0 / den), jnp.where(lane == i2, g_w * (x2 / den), 0.0))


def _outproj_kernel(precise, ret_ref, att_ref, gr_ref, ga_ref, x_ref, wr_ref, wa_ref, wo_ref,
                    gf_ref, wrt_ref, brt_ref, y1_ref, h2_ref, comb_ref):
    prec = HI if precise else None
    mm = F32 if precise else BF16
    r = jnp.dot(ret_ref[...].astype(mm), wr_ref[...], precision=prec, preferred_element_type=F32)
    a = jnp.dot(att_ref[...].astype(mm), wa_ref[...], precision=prec, preferred_element_type=F32)
    merged = jax.nn.sigmoid(gr_ref[...]) * r + jax.nn.sigmoid(ga_ref[...]) * a
    m = jnp.dot(merged.astype(mm), wo_ref[...], precision=prec, preferred_element_type=F32)
    y1 = x_ref[...] + m
    y1_ref[...] = y1
    h2 = _rmsnorm(y1, gf_ref[...])
    h2_ref[...] = h2.astype(h2_ref.dtype)
    logits = jnp.dot(h2.astype(mm), wrt_ref[...], precision=prec, preferred_element_type=F32) + brt_ref[...]
    comb_ref[...] = _route(logits)


def _outproj(ret, att, gr, ga, x, w_ret_o, w_att_o, w_o, g_ffn, w_router, b_router, precise, tm):
    t = x.shape[0]
    row = lambda i: (i, 0)
    full = lambda i: (0, 0)
    wspec = lambda: pl.BlockSpec((D_MODEL, D_MODEL), full, pipeline_mode=pl.Buffered(1))
    return pl.pallas_call(
        functools.partial(_outproj_kernel, precise),
        grid=(t // tm,),
        in_specs=[pl.BlockSpec((tm, D_MODEL), row)] * 5 + [wspec(), wspec(), wspec(),
                  pl.BlockSpec((1, D_MODEL), full),
                  pl.BlockSpec(w_router.shape, full),
                  pl.BlockSpec((1, ROUTER_LANES), full)],
        out_specs=[pl.BlockSpec((tm, D_MODEL), row), pl.BlockSpec((tm, D_MODEL), row),
                   pl.BlockSpec((tm, ROUTER_LANES), row)],
        out_shape=[jax.ShapeDtypeStruct((t, D_MODEL), F32), jax.ShapeDtypeStruct((t, D_MODEL), BF16),
                   jax.ShapeDtypeStruct((t, ROUTER_LANES), F32)],
        compiler_params=_params(("parallel",)),
        name="outproj_precise" if precise else "outproj",
    )(ret, att, gr, ga, x, w_ret_o, w_att_o, w_o, g_ffn, w_router, b_router)


def _sum_pages(pages, out_block):
    per_blk = MOBA_BLOCK // PAGE_SIZE
    for n in range(len(pages) // per_blk):
        acc = jnp.sum(pages[n * per_blk][0], axis=0)
        for p in range(1, per_blk):
            acc = acc + jnp.sum(pages[n * per_blk + p][0], axis=0)
        out_block[n] = acc


def _moe_kernel(epb, n_page, *refs):
    if n_page:
        refs = refs[1:]
    h_ref, comb_ref, y1_ref, wg_ref, wu_ref, wd_ref, gf_ref = refs[:7]
    pages = refs[7:7 + n_page]
    if n_page:
        o_ref, ks_ref, acc_ref = refs[7 + n_page:]
        _sum_pages(pages, ks_ref.at[0])
    else:
        o_ref, acc_ref = refs[7:]
    e = pl.program_id(1)

    @pl.when(e == 0)
    def _():
        acc_ref[...] = jnp.zeros_like(acc_ref)

    h = h_ref[...]
    comb = comb_ref[...]
    lane = lax.broadcasted_iota(jnp.int32, comb.shape, 1)
    for j in range(epb):
        c = jnp.sum(jnp.where(lane == EXPERT_LANE0 + e * epb + j, comb, 0.0), axis=-1, keepdims=True)
        a = jnp.dot(h, wg_ref[j], preferred_element_type=F32)
        u = jnp.dot(h, wu_ref[j], preferred_element_type=F32)
        act = (a * jax.nn.sigmoid(a)) * u * c
        acc_ref[...] += jnp.dot(act.astype(BF16), wd_ref[j], preferred_element_type=F32)

    @pl.when(e == pl.num_programs(1) - 1)
    def _():
        o_ref[...] = _rmsnorm(y1_ref[...] + acc_ref[...], gf_ref[...])


def _moe_fusable_pages(t, tm, epb, total_pages, pages_per_seq):
    steps = (t // tm) * (N_EXPERTS // epb)
    per_blk = MOBA_BLOCK // PAGE_SIZE
    n_page = total_pages // steps
    ok = n_page > 0 and total_pages % steps == 0 and n_page % per_blk == 0 and pages_per_seq % n_page == 0
    return n_page if ok else 0


def _moe(h2, comb, y1, wg, wu, wd, g_final, tm, epb=4, cache_k=None, pt_flat=None, n_page=0):
    t = h2.shape[0]
    n_e = N_EXPERTS // epb
    row = lambda i, e, *_: (i, 0)
    wblk = lambda i, e, *_: (e, 0, 0)
    in_specs = [pl.BlockSpec((tm, D_MODEL), row), pl.BlockSpec((tm, ROUTER_LANES), row),
                pl.BlockSpec((tm, D_MODEL), row),
                pl.BlockSpec((epb, D_MODEL, D_EXPERT), wblk),
                pl.BlockSpec((epb, D_MODEL, D_EXPERT), wblk),
                pl.BlockSpec((epb, D_EXPERT, D_MODEL), wblk),
                pl.BlockSpec((1, D_MODEL), lambda i, e, *_: (0, 0))]
    out_specs = [pl.BlockSpec((tm, D_MODEL), row)]
    out_shape = [jax.ShapeDtypeStruct((t, D_MODEL), F32)]
    args = [h2, comb, y1, wg, wu, wd, g_final]
    if n_page:
        blk_per_step = n_page * PAGE_SIZE // MOBA_BLOCK
        steps = (t // tm) * n_e
        for p in range(n_page):
            in_specs.append(pl.BlockSpec((1, PAGE_SIZE, A_HEADS, A_DH),
                                         lambda i, e, pt, p=p: (pt[(i * n_e + e) * n_page + p], 0, 0, 0)))
        out_specs.append(pl.BlockSpec((1, blk_per_step, A_HEADS, A_DH), lambda i, e, pt: (i * n_e + e, 0, 0, 0)))
        out_shape.append(jax.ShapeDtypeStruct((steps, blk_per_step, A_HEADS, A_DH), F32))
        args = [pt_flat] + args + [cache_k] * n_page
    grid_spec = pltpu.PrefetchScalarGridSpec(
        num_scalar_prefetch=1 if n_page else 0,
        grid=(t // tm, n_e),
        in_specs=in_specs,
        out_specs=out_specs,
        scratch_shapes=[pltpu.VMEM((tm, D_MODEL), F32)],
    )
    out = pl.pallas_call(
        functools.partial(_moe_kernel, epb, n_page),
        grid_spec=grid_spec,
        out_shape=out_shape,
        compiler_params=_params(("parallel", "arbitrary")),
        name="moe_with_cache_sums" if n_page else "moe",
    )(*args)
    return (out[0], out[1]) if n_page else out[0]


def _inproj_sample_kernel(x_ref, g_ref, w_ref, ta_ref, tb_ref, o_ref):
    h = _rmsnorm(x_ref[...], g_ref[...])
    z = jnp.dot(h, w_ref[...], precision=HI, preferred_element_type=F32)
    for c in range(0, z.shape[1], LANES):
        zh = z[:, c:c + LANES]
        o_ref[:, c:c + LANES] = zh * ta_ref[:, c:c + LANES] + pltpu.roll(zh, A_DH // 2, 1) * tb_ref[:, c:c + LANES]


def _inproj_sample(x, g, w_in, ta, tb, tn=512):
    rows = x.shape[0]
    return pl.pallas_call(
        _inproj_sample_kernel,
        grid=(D_IN // tn,),
        in_specs=[pl.BlockSpec((rows, D_MODEL), lambda j: (0, 0)),
                  pl.BlockSpec((1, D_MODEL), lambda j: (0, 0)),
                  pl.BlockSpec((D_MODEL, tn), lambda j: (0, j)),
                  pl.BlockSpec((rows, tn), lambda j: (0, j)),
                  pl.BlockSpec((rows, tn), lambda j: (0, j))],
        out_specs=pl.BlockSpec((rows, tn), lambda j: (0, j)),
        out_shape=jax.ShapeDtypeStruct((rows, D_IN), F32),
        compiler_params=_params(("parallel",)),
        name="inproj_sample",
    )(x, g, w_in, ta, tb)


def _sample_tables(rows, pos):
    half = A_DH // 2
    inv = ROPE_THETA ** (-jnp.arange(half, dtype=F32) / half)
    ang = jnp.full((rows, 1), pos, F32) * inv[None, :]
    cs = jnp.concatenate([jnp.cos(ang), jnp.cos(ang)], axis=-1)
    sn = jnp.concatenate([-jnp.sin(ang), jnp.sin(ang)], axis=-1)
    ones = jnp.ones((rows, LANES), F32)
    zeros = jnp.zeros((rows, LANES), F32)
    ta, tb = [], []
    for off, width, kind in ((OFF_RQ, R_QK_W, R_DK ** -0.5), (OFF_RK, R_QK_W, 1.0), (OFF_RV, 2 * R_V_W, None),
                             (OFF_AQ, A_W, A_DH ** -0.5), (OFF_AK, A_W, 1.0), (OFF_AV, A_W + 2 * D_MODEL, None)):
        for _ in range(width // LANES):
            ta.append(ones if kind is None else cs * kind)
            tb.append(zeros if kind is None else sn * kind)
    return jnp.concatenate(ta, axis=-1), jnp.concatenate(tb, axis=-1)


def _retention_step_kernel(q_ref, k_ref, v_ref, g_ref, s_ref, dec_ref, o_ref, sn_ref):
    for h in range(R_HEADS):
        q = q_ref[0, h]
        k = k_ref[0, h]
        v = v_ref[0, h]
        s = s_ref[0, h]
        dec = dec_ref[h]
        cross = jnp.sum(q * s, axis=0, keepdims=True) * dec
        inner = jnp.sum(q * k, axis=0, keepdims=True) * v
        sn_ref[0, h] = s * dec + k * v
        o = inner + cross
        mu = jnp.mean(o, axis=-1, keepdims=True)
        d = o - mu
        var = jnp.mean(d * d, axis=-1, keepdims=True)
        gate = g_ref[0, h]
        o_ref[0, h] = gate * jax.nn.sigmoid(gate) * (d * lax.rsqrt(var + EPS))


def _retention_step(q, k, v, g, state, dec):
    nb = q.shape[0]
    col = pl.BlockSpec((1, R_HEADS, R_DK, 1), lambda b: (b, 0, 0, 0))
    rowv = pl.BlockSpec((1, R_HEADS, 1, R_DV), lambda b: (b, 0, 0, 0))
    st = pl.BlockSpec((1, R_HEADS, R_DK, R_DV), lambda b: (b, 0, 0, 0))
    return pl.pallas_call(
        _retention_step_kernel,
        grid=(nb,),
        in_specs=[col, col, rowv, rowv, st, pl.BlockSpec((R_HEADS, 1, R_DV), lambda b: (0, 0, 0))],
        out_specs=[rowv, st],
        out_shape=[jax.ShapeDtypeStruct((nb, R_HEADS, 1, R_DV), F32),
                   jax.ShapeDtypeStruct((nb, R_HEADS, R_DK, R_DV), F32)],
        compiler_params=_params(("parallel",)),
        name="retention_step",
    )(q, k, v, g, state, dec)


def _block_sum_kernel(pt_ref, *refs):
    _sum_pages(refs[:-1], refs[-1].at[0, 0])


def _block_sums(cache_k, pt_flat, nb, n_pages):
    steps = n_pages // PAGES_PER_STEP
    blk_per_step = PAGES_PER_STEP * PAGE_SIZE // MOBA_BLOCK

    def page_spec(p):
        return pl.BlockSpec((1, PAGE_SIZE, A_HEADS, A_DH),
                            lambda b, j, pt: (pt[b * n_pages + j * PAGES_PER_STEP + p], 0, 0, 0))

    grid_spec = pltpu.PrefetchScalarGridSpec(
        num_scalar_prefetch=1,
        grid=(nb, steps),
        in_specs=[page_spec(p) for p in range(PAGES_PER_STEP)],
        out_specs=pl.BlockSpec((1, 1, blk_per_step, A_HEADS, A_DH), lambda b, j, pt: (b, j, 0, 0, 0)),
    )
    out = pl.pallas_call(
        _block_sum_kernel,
        grid_spec=grid_spec,
        out_shape=jax.ShapeDtypeStruct((nb, steps, blk_per_step, A_HEADS, A_DH), F32),
        compiler_params=_params(("parallel", "parallel")),
        name="cache_block_sums",
    )(pt_flat, *([cache_k] * PAGES_PER_STEP))
    return out.reshape(nb, steps * blk_per_step, A_W)


def _block_pick_kernel(q_ref, ks_ref, o_ref):
    ks = ks_ref[0] * (1.0 / MOBA_BLOCK)
    prod = ks * q_ref[0]
    nblk = prod.shape[0]
    lane = lax.broadcasted_iota(jnp.int32, (nblk, LANES), 1)
    gate = jnp.full((nblk, LANES), -jnp.inf, F32)
    for h in range(A_HEADS):
        gate = jnp.where(lane == h, jnp.sum(prod[:, h * A_DH:(h + 1) * A_DH], axis=-1, keepdims=True), gate)
    blk = lax.broadcasted_iota(jnp.int32, gate.shape, 0)
    rank = jnp.zeros(gate.shape, jnp.int32)
    for m in range(nblk):
        gm = gate[m:m + 1, :]
        ahead = (gm > gate) | ((gm == gate) & (m < blk))
        rank = rank + ahead.astype(jnp.int32)
    for j in range(MOBA_TOPK):
        o_ref[0, j:j + 1, :] = jnp.sum(jnp.where(rank == j, blk, 0), axis=0, keepdims=True)


def _block_pick(q, ksum):
    nb, nblk, _ = ksum.shape
    return pl.pallas_call(
        _block_pick_kernel,
        grid=(nb,),
        in_specs=[pl.BlockSpec((1, 1, A_W), lambda b: (b, 0, 0)),
                  pl.BlockSpec((1, nblk, A_W), lambda b: (b, 0, 0))],
        out_specs=pl.BlockSpec((1, MOBA_TOPK, LANES), lambda b: (b, 0, 0)),
        out_shape=jax.ShapeDtypeStruct((nb, MOBA_TOPK, LANES), jnp.int32),
        compiler_params=_params(("parallel",)),
        name="block_pick",
    )(q, ksum)


PICK_PAGES = MOBA_TOPK * (MOBA_BLOCK // PAGE_SIZE)


def _attend_sample_kernel(pages_per_seq, pt_ref, idx_ref, q_ref, kn_ref, vn_ref, ck_ref, cv_ref, o_ref,
                          kbuf, vbuf, ksem, vsem):
    b = pl.program_id(0)
    per_blk = MOBA_BLOCK // PAGE_SIZE

    def page_copies(seq_i, slot):
        out = []
        for h in range(A_HEADS):
            for j in range(MOBA_TOPK):
                blk = idx_ref[(seq_i * MOBA_TOPK + j) * A_HEADS + h]
                for p in range(per_blk):
                    page = pt_ref[seq_i * pages_per_seq + blk * per_blk + p]
                    dst = (slot, h, j * per_blk + p)
                    out.append(pltpu.make_async_copy(ck_ref.at[page, :, h, :], kbuf.at[dst], ksem.at[slot]))
                    out.append(pltpu.make_async_copy(cv_ref.at[page, :, h, :], vbuf.at[dst], vsem.at[slot]))
        return out

    slot = b % 2

    @pl.when(b == 0)
    def _():
        for c in page_copies(0, 0):
            c.start()

    @pl.when(b + 1 < pl.num_programs(0))
    def _():
        for c in page_copies(b + 1, 1 - slot):
            c.start()

    for c in page_copies(b, slot):
        c.wait()

    for h in range(A_HEADS):
        q = q_ref[0, h:h + 1, :]
        s_own = jnp.sum(q * kn_ref[0, h:h + 1, :], axis=-1, keepdims=True)
        scores = [jnp.sum(kbuf[slot, h, g] * q, axis=-1, keepdims=True) for g in range(PICK_PAGES)]
        m = s_own
        for s in scores:
            m = jnp.maximum(m, jnp.max(s, axis=0, keepdims=True))
        p_own = jnp.exp(s_own - m)
        l = p_own
        acc = p_own * vn_ref[0, h:h + 1, :]
        for g, s in enumerate(scores):
            p = jnp.exp(s - m)
            l = l + jnp.sum(p, axis=0, keepdims=True)
            acc = acc + jnp.sum(p * vbuf[slot, h, g], axis=0, keepdims=True)
        o_ref[0, h:h + 1, :] = acc / l


def _attend_sample(q, k_new, v_new, cache_k, cache_v, pt_flat, picks_flat, pages_per_seq):
    nb = q.shape[0]
    vec = pl.BlockSpec((1, A_HEADS, A_DH), lambda b, pt, idx: (b, 0, 0))
    hbm = pl.BlockSpec(memory_space=pl.ANY)
    buf = pltpu.VMEM((2, A_HEADS, PICK_PAGES, PAGE_SIZE, A_DH), F32)
    grid_spec = pltpu.PrefetchScalarGridSpec(
        num_scalar_prefetch=2,
        grid=(nb,),
        in_specs=[vec, vec, vec, hbm, hbm],
        out_specs=vec,
        scratch_shapes=[buf, buf, pltpu.SemaphoreType.DMA((2,)), pltpu.SemaphoreType.DMA((2,))],
    )
    return pl.pallas_call(
        functools.partial(_attend_sample_kernel, pages_per_seq),
        grid_spec=grid_spec,
        out_shape=jax.ShapeDtypeStruct((nb, A_HEADS, A_DH), F32),
        compiler_params=_params(("arbitrary",)),
        name="attend_sample",
    )(pt_flat, picks_flat, q, k_new, v_new, cache_k, cache_v)


def _rope_tables(seq):
    half = A_DH // 2
    inv = ROPE_THETA ** (-jnp.arange(half, dtype=F32) / half)
    ang = jnp.arange(seq, dtype=jnp.int32).astype(F32)[:, None] * inv[None, :]
    cos, sin = jnp.cos(ang), jnp.sin(ang)
    return jnp.concatenate([cos, cos], axis=-1), jnp.concatenate([-sin, sin], axis=-1)


def _router_weights(w_rg, b_rg, w_re, b_re):
    pad = ROUTER_LANES - N_GROUPS - N_EXPERTS
    w = jnp.concatenate([w_rg, w_re.reshape(D_MODEL, N_EXPERTS), jnp.zeros((D_MODEL, pad), F32)], axis=-1)
    b = jnp.concatenate([b_rg, b_re.reshape(N_EXPERTS), jnp.zeros((pad,), F32)])[None, :]
    return w, b


def kernel(x_prompt, x_sample, cache_k, cache_v, state_ret, page_table, norm_mix, w_in, w_ret_o, w_att_o, w_o,
           norm_ffn, w_router_group, b_router_group, w_router_expert, b_router_expert, w_gate, w_up, w_down,
           norm_final):
    nb, seq, _ = x_prompt.shape
    db, ds, _ = x_sample.shape
    depth = norm_mix.shape[0]
    past_len = page_table.shape[1] * PAGE_SIZE
    assert depth == 1 and ds == 1
    assert seq % MOBA_BLOCK == 0 and past_len % (PAGES_PER_STEP * PAGE_SIZE) == 0
    assert past_len // MOBA_BLOCK >= MOBA_TOPK
    l = 0
    t = nb * seq

    g_mix = norm_mix[l][None, :]
    g_ffn = norm_ffn[l][None, :]
    g_fin = norm_final[None, :]
    w_router, b_router = _router_weights(w_router_group[l], b_router_group[l], w_router_expert[l], b_router_expert[l])
    w_in_bf = w_in[l].astype(BF16)
    wg_bf, wu_bf, wd_bf = w_gate[l].astype(BF16), w_up[l].astype(BF16), w_down[l].astype(BF16)
    wro_bf, wao_bf, wo_bf = w_ret_o[l].astype(BF16), w_att_o[l].astype(BF16), w_o[l].astype(BF16)

    xp = x_prompt.reshape(t, D_MODEL)
    cs, sn = _rope_tables(seq)
    rq, rk, rv, rg, aq, kp, vp, gr, ga = _inproj(xp, g_mix, w_in_bf, cs, sn, seq)
    ret, ret_state = _retention(rq, rk, rv, rg, _retention_tables(), nb, seq)
    att = _moba(aq, kp, vp, nb, seq)
    y1, h2, comb = _outproj(ret, att, gr, ga, xp, wro_bf, wao_bf, wo_bf, g_ffn, w_router.astype(BF16), b_router,
                            precise=False, tm=256)
    ck, cv = cache_k[l], cache_v[l]
    n_pages = page_table.shape[1]
    pt_flat = page_table.reshape(-1)
    fused_pages = _moe_fusable_pages(t, MOE_TM, MOE_EPB, db * n_pages, n_pages)
    moe_out = _moe(h2, comb, y1, wg_bf, wu_bf, wd_bf, g_fin, tm=MOE_TM, epb=MOE_EPB,
                   cache_k=ck, pt_flat=pt_flat, n_page=fused_pages)
    if fused_pages:
        y_prompt, ksum = moe_out
        ksum = ksum.reshape(db, n_pages * PAGE_SIZE // MOBA_BLOCK, A_W)
    else:
        y_prompt, ksum = moe_out, _block_sums(ck, pt_flat, db, n_pages)
    y_prompt = y_prompt.reshape(nb, seq, D_MODEL)

    xs = x_sample.reshape(db, D_MODEL)
    ta, tb = _sample_tables(db, past_len)
    zs = _inproj_sample(xs, g_mix, w_in[l], ta, tb)
    seg = lambda off, width: zs[:, off:off + width]
    ld = jnp.log(1.0 - 2.0 ** (-5.0 - jnp.arange(R_HEADS, dtype=F32)))
    dec = jnp.broadcast_to(jnp.exp(ld * 1.0)[:, None, None], (R_HEADS, 1, R_DV))
    ret_s, state_s = _retention_step(
        seg(OFF_RQ, R_QK_W).reshape(db, R_HEADS, R_DK, 1), seg(OFF_RK, R_QK_W).reshape(db, R_HEADS, R_DK, 1),
        seg(OFF_RV, R_V_W).reshape(db, R_HEADS, 1, R_DV), seg(OFF_RG, R_V_W).reshape(db, R_HEADS, 1, R_DV),
        state_ret[l].astype(F32), dec)
    aq_s, ak_s, av_s = seg(OFF_AQ, A_W), seg(OFF_AK, A_W), seg(OFF_AV, A_W)
    picks = _block_pick(aq_s.reshape(db, 1, A_W), ksum)[:, :, :A_HEADS].reshape(-1)
    heads = lambda z: z.reshape(db, A_HEADS, A_DH)
    att_s = _attend_sample(heads(aq_s), heads(ak_s), heads(av_s), ck, cv, pt_flat, picks, n_pages)
    y1_s, h2_s, comb_s = _outproj(ret_s.reshape(db, R_V_W), att_s.reshape(db, A_W), seg(OFF_GR, D_MODEL),
                                  seg(OFF_GA, D_MODEL), xs, w_ret_o[l], w_att_o[l], w_o[l], g_ffn,
                                  w_router, b_router, precise=True, tm=db)
    y_sample = _moe(h2_s, comb_s, y1_s, wg_bf, wu_bf, wd_bf, g_fin, tm=db).reshape(db, ds, D_MODEL)

    k_prompt = kp.reshape(1, nb, seq, A_HEADS, A_DH)
    v_prompt = vp.reshape(1, nb, seq, A_HEADS, A_DH)
    ret_prompt = ret_state[None]
    k_sample = ak_s.reshape(1, db, ds, A_HEADS, A_DH).astype(cache_k.dtype)
    v_sample = av_s.reshape(1, db, ds, A_HEADS, A_DH).astype(cache_v.dtype)
    ret_sample = state_s[None].astype(state_ret.dtype)
    return (y_prompt, y_sample, k_prompt, v_prompt, ret_prompt, k_sample, v_sample, ret_sample)
```

```python
import functools

import jax
import jax.numpy as jnp
from jax import lax
from jax.experimental import pallas as pl
from jax.experimental.pallas import tpu as pltpu

D_MODEL = 1024
PAGE_SIZE = 128
R_HEADS = 4
R_DK = 128
R_DV = 256
R_CHUNK = 128
A_HEADS = 8
A_DH = 128
MOBA_BLOCK = 256
MOBA_TOPK = 3
ROPE_THETA = 10000.0
N_GROUPS = 4
EXPERTS_PER_GROUP = 8
N_EXPERTS = N_GROUPS * EXPERTS_PER_GROUP
D_EXPERT = 256
EPS = 1e-6

R_QK_W = R_HEADS * R_DK
R_V_W = R_HEADS * R_DV
A_W = A_HEADS * A_DH
D_IN = 2 * R_QK_W + 2 * R_V_W + 3 * A_W + 2 * D_MODEL
OFF_RQ = 0
OFF_RK = OFF_RQ + R_QK_W
OFF_RV = OFF_RK + R_QK_W
OFF_RG = OFF_RV + R_V_W
OFF_AQ = OFF_RG + R_V_W
OFF_AK = OFF_AQ + A_W
OFF_AV = OFF_AK + A_W
OFF_GR = OFF_AV + A_W
OFF_GA = OFF_GR + D_MODEL

LANES = 128
ROUTER_LANES = LANES
EXPERT_LANE0 = N_GROUPS
GROUP_ID_LANE = 0
PAGES_PER_STEP = 8
MOE_BLOCK = 1024
MOE_RIDER_PAGES = 8
MOBA_RIDER_PAGES = 16
VMEM_LIMIT = 56 * 1024 * 1024

F32 = jnp.float32
BF16 = jnp.bfloat16
HI = lax.Precision.HIGHEST
NT = (((1,), (1,)), ((), ()))


def _params(sem, vmem=VMEM_LIMIT):
    return pltpu.CompilerParams(dimension_semantics=sem, vmem_limit_bytes=vmem)


def _rmsnorm(x, g):
    return x * lax.rsqrt(jnp.mean(x * x, axis=-1, keepdims=True) + EPS) * g


def _rope_head(z, cs, sn):
    return z * cs + pltpu.roll(z, A_DH // 2, 1) * sn


def _inproj_kernel(x_ref, g_ref, w_ref, cs_ref, sn_ref,
                   rq_ref, rk_ref, rv_ref, rg_ref, aq_ref, k_ref, v_ref, gr_ref, ga_ref):
    h = _rmsnorm(x_ref[...], g_ref[...]).astype(BF16)
    cs = cs_ref[...]
    sn = sn_ref[...]

    def z_cols(c0):
        return jnp.dot(h, w_ref[:, c0:c0 + 2 * LANES], preferred_element_type=F32)

    def plain(off, width, out_ref):
        for c in range(0, width, 2 * LANES):
            out_ref[:, c:c + 2 * LANES] = z_cols(off + c).astype(out_ref.dtype)

    def roped(off, width, out_ref, scale):
        for c in range(0, width, 2 * LANES):
            z = z_cols(off + c)
            for hh in range(2):
                r = _rope_head(z[:, hh * LANES:(hh + 1) * LANES], cs, sn)
                if scale is not None:
                    r = r * scale
                out_ref[:, c + hh * LANES:c + (hh + 1) * LANES] = r.astype(out_ref.dtype)

    roped(OFF_RQ, R_QK_W, rq_ref, R_DK ** -0.5)
    roped(OFF_RK, R_QK_W, rk_ref, None)
    plain(OFF_RV, R_V_W, rv_ref)
    plain(OFF_RG, R_V_W, rg_ref)
    roped(OFF_AQ, A_W, aq_ref, A_DH ** -0.5)
    roped(OFF_AK, A_W, k_ref, None)
    plain(OFF_AV, A_W, v_ref)
    plain(OFF_GR, D_MODEL, gr_ref)
    plain(OFF_GA, D_MODEL, ga_ref)


def _inproj(x, g, w_bf, cs, sn, seq, tm=256):
    t = x.shape[0]
    nseq = seq // tm
    row = lambda i: (i, 0)
    tab = lambda i: (i % nseq, 0)
    outs = [(R_QK_W, BF16), (R_QK_W, BF16), (R_V_W, BF16), (R_V_W, F32), (A_W, BF16),
            (A_W, F32), (A_W, F32), (D_MODEL, F32), (D_MODEL, F32)]
    return pl.pallas_call(
        _inproj_kernel,
        grid=(t // tm,),
        in_specs=[pl.BlockSpec((tm, D_MODEL), row),
                  pl.BlockSpec((1, D_MODEL), lambda i: (0, 0)),
                  pl.BlockSpec((D_MODEL, D_IN), lambda i: (0, 0), pipeline_mode=pl.Buffered(1)),
                  pl.BlockSpec((tm, LANES), tab),
                  pl.BlockSpec((tm, LANES), tab)],
        out_specs=[pl.BlockSpec((tm, w), row) for w, _ in outs],
        out_shape=[jax.ShapeDtypeStruct((t, w), d) for w, d in outs],
        compiler_params=_params(("parallel",)),
        name="inproj",
    )(x, g, w_bf, cs, sn)


def _retention_kernel(q_ref, k_ref, v_ref, g_ref, dm_ref, cd_ref, kd_ref, sd_ref, o_ref, st_ref, s_scr):
    nchunk = q_ref.shape[0] // R_CHUNK
    s_scr[...] = jnp.zeros_like(s_scr)
    for c in range(nchunk):
        rows = slice(c * R_CHUNK, (c + 1) * R_CHUNK)
        q = q_ref[rows, :]
        k = k_ref[rows, :]
        v = v_ref[rows, :]
        s = s_scr[...]
        scores = lax.dot_general(q, k, NT, preferred_element_type=F32) * dm_ref[0]
        inner = jnp.dot(scores.astype(BF16), v, preferred_element_type=F32)
        cross = jnp.dot(q, s.astype(BF16), preferred_element_type=F32) * cd_ref[0]
        k_dec_t = (k.astype(F32) * kd_ref[0]).T.astype(BF16)
        s_scr[...] = s * sd_ref[0] + jnp.dot(k_dec_t, v, preferred_element_type=F32)
        o = inner + cross
        mu = jnp.mean(o, axis=-1, keepdims=True)
        d = o - mu
        var = jnp.mean(d * d, axis=-1, keepdims=True)
        y = d * lax.rsqrt(var + EPS)
        gate = g_ref[rows, :]
        o_ref[rows, :] = (gate * jax.nn.sigmoid(gate) * y).astype(o_ref.dtype)
    st_ref[0, 0] = s_scr[...]


def _retention(rq, rk, rv, rg, tabs, nb, seq):
    dm, cd, kd, sd = tabs
    head_tab = lambda shp: pl.BlockSpec((1,) + shp, lambda b, h: (h, 0, 0))
    return pl.pallas_call(
        _retention_kernel,
        grid=(nb, R_HEADS),
        in_specs=[pl.BlockSpec((seq, R_DK), lambda b, h: (b, h)),
                  pl.BlockSpec((seq, R_DK), lambda b, h: (b, h)),
                  pl.BlockSpec((seq, R_DV), lambda b, h: (b, h)),
                  pl.BlockSpec((seq, R_DV), lambda b, h: (b, h)),
                  head_tab((R_CHUNK, R_CHUNK)), head_tab((R_CHUNK, R_DV)),
                  head_tab((R_CHUNK, R_DK)), head_tab((R_DK, R_DV))],
        out_specs=[pl.BlockSpec((seq, R_DV), lambda b, h: (b, h)),
                   pl.BlockSpec((1, 1, R_DK, R_DV), lambda b, h: (b, h, 0, 0))],
        out_shape=[jax.ShapeDtypeStruct((nb * seq, R_V_W), BF16),
                   jax.ShapeDtypeStruct((nb, R_HEADS, R_DK, R_DV), F32)],
        scratch_shapes=[pltpu.VMEM((R_DK, R_DV), F32)],
        compiler_params=_params(("parallel", "parallel")),
        name="retention",
    )(rq, rk, rv, rg, dm, cd, kd, sd)


def _retention_tables():
    ld = jnp.log(1.0 - 2.0 ** (-5.0 - jnp.arange(R_HEADS, dtype=F32)))[:, None, None]
    i = jnp.arange(R_CHUNK, dtype=F32)
    diff = i[:, None] - i[None, :]
    dmask = jnp.where(diff >= 0, jnp.exp(ld * jnp.maximum(diff, 0.0)), 0.0)
    cdec = jnp.broadcast_to(jnp.exp(ld * (i + 1.0)[None, :, None]), (R_HEADS, R_CHUNK, R_DV))
    kdec = jnp.broadcast_to(jnp.exp(ld * (R_CHUNK - 1.0 - i)[None, :, None]), (R_HEADS, R_CHUNK, R_DK))
    sdec = jnp.broadcast_to(jnp.exp(ld * R_CHUNK), (R_HEADS, R_DK, R_DV))
    return dmask, cdec, kdec, sdec


MASK_BIAS = -1e30


def _moba_kernel(n_page, *refs):
    if n_page:
        refs = refs[1:]
    q_ref, k_ref, v_ref = refs[:3]
    pages = refs[3:3 + n_page]
    if n_page:
        o_ref, ks_ref, kb_scr, vb_scr, km_scr, s_scr, p_scr = refs[3 + n_page:]
        _sum_pages(pages, ks_ref.at[0])
    else:
        o_ref, kb_scr, vb_scr, km_scr, s_scr, p_scr = refs[3:]
    blk_sz = MOBA_BLOCK
    nblk = k_ref.shape[0] // blk_sz
    lane = lax.broadcasted_iota(jnp.int32, (blk_sz, A_DH), 1)
    for n in range(nblk):
        rows = slice(n * blk_sz, (n + 1) * blk_sz)
        kn = k_ref[rows, :]
        kb_scr[rows, 0:A_DH] = kn.astype(BF16)
        kb_scr[rows, A_DH:2 * A_DH] = jnp.where(lane == n, 1.0, 0.0).astype(BF16)
        km_scr[n:n + 1, :] = jnp.mean(kn, axis=0, keepdims=True)
        vb_scr[rows, :] = v_ref[rows, :].astype(BF16)
    kmean = km_scr[...].astype(BF16)

    for i in range(nblk):
        q = q_ref[i * blk_sz:(i + 1) * blk_sz, :]
        gate = lax.dot_general(kmean, q, NT, preferred_element_type=F32)
        blk = lax.broadcasted_iota(jnp.int32, gate.shape, 0)
        rank = jnp.zeros(gate.shape, jnp.int32)
        for m in range(i):
            gm = gate[m:m + 1, :]
            ahead = (gm > gate) | ((gm == gate) & (m < blk))
            rank = rank + ahead.astype(jnp.int32)
        bias_t = jnp.where((blk < i) & (rank >= MOBA_TOPK), MASK_BIAS, 0.0)
        bias_t = jnp.concatenate([bias_t, jnp.zeros((A_DH - nblk, blk_sz), F32)], axis=0)
        q_aug = jnp.concatenate([q, bias_t.T.astype(BF16)], axis=1)

        nk = (i + 1) * blk_sz
        half = blk_sz // 2
        m_run = jnp.full((blk_sz, half), -jnp.inf, F32)
        for n in range(i + 1):
            cols = slice(n * blk_sz, (n + 1) * blk_sz)
            s = lax.dot_general(q_aug, kb_scr[cols, :], NT, preferred_element_type=F32)
            if n == i:
                qpos = lax.broadcasted_iota(jnp.int32, s.shape, 0)
                kpos = lax.broadcasted_iota(jnp.int32, s.shape, 1)
                s = jnp.where(kpos <= qpos, s, -jnp.inf)
            s_scr[i % 2, :, cols] = s
            m_run = jnp.maximum(m_run, jnp.maximum(s[:, :half], s[:, half:]))
        m_row = jnp.max(m_run, axis=-1, keepdims=True)
        l_run = jnp.zeros((blk_sz, half), F32)
        for n in range(i + 1):
            cols = slice(n * blk_sz, (n + 1) * blk_sz)
            p = jnp.exp(s_scr[i % 2, :, cols] - m_row)
            l_run = l_run + (p[:, :half] + p[:, half:])
            p_scr[i % 2, :, cols] = p.astype(BF16)
        l_row = jnp.sum(l_run, axis=-1, keepdims=True)
        o = jnp.dot(p_scr[i % 2, :, 0:nk], vb_scr[0:nk, :], preferred_element_type=F32)
        o_ref[i * blk_sz:(i + 1) * blk_sz, :] = (o / l_row).astype(o_ref.dtype)


def _moba(aq, k, v, nb, seq, cache_k=None, pt_flat=None, n_page=0):
    nblk = seq // MOBA_BLOCK
    assert nblk <= A_DH
    blk = lambda b, h, *_: (b, h)
    in_specs = [pl.BlockSpec((seq, A_DH), blk)] * 3
    out_specs = [pl.BlockSpec((seq, A_DH), blk)]
    out_shape = [jax.ShapeDtypeStruct((nb * seq, A_W), BF16)]
    args = [aq, k, v]
    if n_page:
        blk_per_step = n_page * PAGE_SIZE // MOBA_BLOCK
        for q in range(n_page):
            in_specs.append(pl.BlockSpec((1, PAGE_SIZE, A_HEADS, A_DH),
                                         lambda b, h, pt, q=q: (pt[(b * A_HEADS + h) * n_page + q], 0, 0, 0)))
        out_specs.append(pl.BlockSpec((1, blk_per_step, A_HEADS, A_DH), lambda b, h, pt: (b * A_HEADS + h, 0, 0, 0)))
        out_shape.append(jax.ShapeDtypeStruct((nb * A_HEADS, blk_per_step, A_HEADS, A_DH), F32))
        args = [pt_flat] + args + [cache_k] * n_page
    grid_spec = pltpu.PrefetchScalarGridSpec(
        num_scalar_prefetch=1 if n_page else 0,
        grid=(nb, A_HEADS),
        in_specs=in_specs,
        out_specs=out_specs,
        scratch_shapes=[pltpu.VMEM((seq, 2 * A_DH), BF16),
                        pltpu.VMEM((seq, A_DH), BF16),
                        pltpu.VMEM((nblk, A_DH), F32),
                        pltpu.VMEM((2, MOBA_BLOCK, seq), F32),
                        pltpu.VMEM((2, MOBA_BLOCK, seq), BF16)],
    )
    out = pl.pallas_call(
        functools.partial(_moba_kernel, n_page),
        grid_spec=grid_spec,
        out_shape=out_shape,
        compiler_params=_params(("parallel", "parallel")),
        name="moba",
    )(*args)
    return (out[0], out[1].reshape(-1, A_HEADS, A_DH)) if n_page else out[0]


def _route(logits):
    lane = lax.broadcasted_iota(jnp.int32, logits.shape, 1)
    ninf = -jnp.inf
    big = jnp.int32(ROUTER_LANES)
    glog = jnp.where(lane < N_GROUPS, logits, ninf)
    gmax = jnp.max(glog, axis=-1, keepdims=True)
    g_idx = jnp.min(jnp.where(glog == gmax, lane, big), axis=-1, keepdims=True)
    g_w = 1.0 / jnp.sum(jnp.exp(glog - gmax), axis=-1, keepdims=True)
    e_lo = EXPERT_LANE0 + g_idx * EXPERTS_PER_GROUP
    in_group = (lane >= e_lo) & (lane < e_lo + EXPERTS_PER_GROUP)
    e1 = jnp.where(in_group, logits, ninf)
    v1 = jnp.max(e1, axis=-1, keepdims=True)
    i1 = jnp.min(jnp.where(e1 == v1, lane, big), axis=-1, keepdims=True)
    e2 = jnp.where(lane == i1, ninf, e1)
    v2 = jnp.max(e2, axis=-1, keepdims=True)
    i2 = jnp.min(jnp.where(e2 == v2, lane, big), axis=-1, keepdims=True)
    x2 = jnp.exp(v2 - v1)
    den = 1.0 + x2
    comb = jnp.where(lane == i1, g_w * (1.0 / den), jnp.where(lane == i2, g_w * (x2 / den), 0.0))
    return jnp.where(lane == GROUP_ID_LANE, g_idx.astype(F32), comb)


def _outproj_kernel(precise, ret_ref, att_ref, gr_ref, ga_ref, x_ref, wr_ref, wa_ref, wo_ref,
                    gf_ref, wrt_ref, brt_ref, y1_ref, h2_ref, comb_ref):
    prec = HI if precise else None
    mm = F32 if precise else BF16
    r = jnp.dot(ret_ref[...].astype(mm), wr_ref[...], precision=prec, preferred_element_type=F32)
    a = jnp.dot(att_ref[...].astype(mm), wa_ref[...], precision=prec, preferred_element_type=F32)
    merged = jax.nn.sigmoid(gr_ref[...]) * r + jax.nn.sigmoid(ga_ref[...]) * a
    m = jnp.dot(merged.astype(mm), wo_ref[...], precision=prec, preferred_element_type=F32)
    y1 = x_ref[...] + m
    y1_ref[...] = y1
    h2 = _rmsnorm(y1, gf_ref[...])
    h2_ref[...] = h2.astype(h2_ref.dtype)
    logits = jnp.dot(h2.astype(mm), wrt_ref[...], precision=prec, preferred_element_type=F32) + brt_ref[...]
    comb_ref[...] = _route(logits)


def _outproj(ret, att, gr, ga, x, w_ret_o, w_att_o, w_o, g_ffn, w_router, b_router, precise, tm):
    t = x.shape[0]
    row = lambda i: (i, 0)
    full = lambda i: (0, 0)
    wspec = lambda: pl.BlockSpec((D_MODEL, D_MODEL), full, pipeline_mode=pl.Buffered(1))
    return pl.pallas_call(
        functools.partial(_outproj_kernel, precise),
        grid=(t // tm,),
        in_specs=[pl.BlockSpec((tm, D_MODEL), row)] * 5 + [wspec(), wspec(), wspec(),
                  pl.BlockSpec((1, D_MODEL), full),
                  pl.BlockSpec(w_router.shape, full),
                  pl.BlockSpec((1, ROUTER_LANES), full)],
        out_specs=[pl.BlockSpec((tm, D_MODEL), row), pl.BlockSpec((tm, D_MODEL), row),
                   pl.BlockSpec((tm, ROUTER_LANES), row)],
        out_shape=[jax.ShapeDtypeStruct((t, D_MODEL), F32), jax.ShapeDtypeStruct((t, D_MODEL), BF16),
                   jax.ShapeDtypeStruct((t, ROUTER_LANES), F32)],
        compiler_params=_params(("parallel",)),
        name="outproj_precise" if precise else "outproj",
    )(ret, att, gr, ga, x, w_ret_o, w_att_o, w_o, g_ffn, w_router, b_router)


def _sum_pages(pages, out_block):
    per_blk = MOBA_BLOCK // PAGE_SIZE
    for n in range(len(pages) // per_blk):
        acc = jnp.sum(pages[n * per_blk][0], axis=0)
        for p in range(1, per_blk):
            acc = acc + jnp.sum(pages[n * per_blk + p][0], axis=0)
        out_block[n] = acc


def _moe_kernel(epb, h_ref, comb_ref, y1_ref, wg_ref, wu_ref, wd_ref, gf_ref, o_ref, acc_ref):
    e = pl.program_id(1)

    @pl.when(e == 0)
    def _():
        acc_ref[...] = jnp.zeros_like(acc_ref)

    h = h_ref[...]
    comb = comb_ref[...]
    lane = lax.broadcasted_iota(jnp.int32, comb.shape, 1)
    for j in range(epb):
        c = jnp.sum(jnp.where(lane == EXPERT_LANE0 + e * epb + j, comb, 0.0), axis=-1, keepdims=True)
        a = jnp.dot(h, wg_ref[j], preferred_element_type=F32)
        u = jnp.dot(h, wu_ref[j], preferred_element_type=F32)
        act = (a * jax.nn.sigmoid(a)) * u * c
        acc_ref[...] += jnp.dot(act.astype(BF16), wd_ref[j], preferred_element_type=F32)

    @pl.when(e == pl.num_programs(1) - 1)
    def _():
        o_ref[...] = _rmsnorm(y1_ref[...] + acc_ref[...], gf_ref[...])


def _moe(h2, comb, y1, wg, wu, wd, g_final, tm, epb=4):
    t = h2.shape[0]
    row = lambda i, e: (i, 0)
    wblk = lambda i, e: (e, 0, 0)
    return pl.pallas_call(
        functools.partial(_moe_kernel, epb),
        grid=(t // tm, N_EXPERTS // epb),
        in_specs=[pl.BlockSpec((tm, D_MODEL), row), pl.BlockSpec((tm, ROUTER_LANES), row),
                  pl.BlockSpec((tm, D_MODEL), row),
                  pl.BlockSpec((epb, D_MODEL, D_EXPERT), wblk),
                  pl.BlockSpec((epb, D_MODEL, D_EXPERT), wblk),
                  pl.BlockSpec((epb, D_EXPERT, D_MODEL), wblk),
                  pl.BlockSpec((1, D_MODEL), lambda i, e: (0, 0))],
        out_specs=pl.BlockSpec((tm, D_MODEL), row),
        out_shape=jax.ShapeDtypeStruct((t, D_MODEL), F32),
        scratch_shapes=[pltpu.VMEM((tm, D_MODEL), F32)],
        compiler_params=_params(("parallel", "arbitrary")),
        name="moe",
    )(h2, comb, y1, wg, wu, wd, g_final)


SORT_TILE = 128
EXPERT_SPLIT = 2


def _moe_sorted_kernel(n_page, *refs):
    if n_page:
        refs = refs[1:]
    h_ref, comb_ref, y1_ref, wg_ref, wu_ref, wd_ref, gf_ref, tri_ref = refs[:8]
    pages = refs[8:8 + n_page]
    rest = refs[8 + n_page:]
    if n_page:
        o_ref, ks_ref = rest[:2]
        _sum_pages(pages, ks_ref.at[0])
        rest = rest[2:]
    else:
        o_ref, rest = rest[0], rest[1:]
    xs_scr, cs_scr, out_scr, dest_scr, run_smem = rest
    g = pl.program_id(1)
    part = pl.program_id(2)
    tb = h_ref.shape[0]
    cap = xs_scr.shape[0]
    epb = wg_ref.shape[0]

    @pl.when((g == 0) & (part == 0))
    def _sort_block():
        comb = comb_ref[...]
        lane = lax.broadcasted_iota(jnp.int32, comb.shape, 1)
        gcol = comb[:, GROUP_ID_LANE:GROUP_ID_LANE + 1]
        ind = jnp.where((lane < N_GROUPS) & (lane.astype(F32) == gcol), 1.0, 0.0)
        pos = jnp.dot(tri_ref[...], ind.astype(BF16), preferred_element_type=F32)
        dest = jnp.zeros((tb, 1), F32)
        start = jnp.int32(0)
        for gg in range(N_GROUPS):
            col = ind[:, gg:gg + 1]
            n_tiles = (jnp.sum(col).astype(jnp.int32) + SORT_TILE - 1) // SORT_TILE
            run_smem[gg] = start
            run_smem[N_GROUPS + gg] = n_tiles
            dest = dest + col * (pos[:, gg:gg + 1] + start.astype(F32))
            start = start + n_tiles * SORT_TILE
        dest_b = jnp.broadcast_to(dest, (tb, LANES))
        dest_scr[...] = dest_b
        dest_row = dest_b.T[0:1, :].astype(jnp.int32)
        c_hi = comb.astype(BF16)
        r1 = comb - c_hi.astype(F32)
        c_mid = r1.astype(BF16)
        c_lo = (r1 - c_mid.astype(F32)).astype(BF16)
        h = h_ref[...]
        for c in range(cap // SORT_TILE):
            rows = slice(c * SORT_TILE, (c + 1) * SORT_TILE)
            riota = lax.broadcasted_iota(jnp.int32, (SORT_TILE, tb), 0) + c * SORT_TILE
            perm = jnp.where(riota == dest_row, 1.0, 0.0).astype(BF16)
            xs_scr[rows, :] = jnp.dot(perm, h, preferred_element_type=F32).astype(BF16)
            cs_scr[rows, :] = (jnp.dot(perm, c_hi, preferred_element_type=F32)
                               + jnp.dot(perm, c_mid, preferred_element_type=F32)
                               + jnp.dot(perm, c_lo, preferred_element_type=F32))
        out_scr[...] = jnp.zeros_like(out_scr)

    start = run_smem[g]
    n_tiles = run_smem[N_GROUPS + g]
    lane_c = lax.broadcasted_iota(jnp.int32, (SORT_TILE, ROUTER_LANES), 1)
    lane0 = EXPERT_LANE0 + g * EXPERTS_PER_GROUP + part * epb

    def run_tile(t, carry):
        r0 = pl.multiple_of(start + t * SORT_TILE, SORT_TILE)
        x = xs_scr[pl.ds(r0, SORT_TILE), :]
        cs = cs_scr[pl.ds(r0, SORT_TILE), :]
        acc = jnp.zeros((SORT_TILE, D_MODEL), F32)
        for j in range(epb):
            c = jnp.sum(jnp.where(lane_c == lane0 + j, cs, 0.0), axis=-1, keepdims=True)
            a = jnp.dot(x, wg_ref[j], preferred_element_type=F32)
            u = jnp.dot(x, wu_ref[j], preferred_element_type=F32)
            act = (a * jax.nn.sigmoid(a)) * u * c
            acc = acc + jnp.dot(act.astype(BF16), wd_ref[j], preferred_element_type=F32)
        out_scr[pl.ds(r0, SORT_TILE), :] += acc
        return carry

    lax.fori_loop(0, n_tiles, run_tile, 0)

    @pl.when((g == pl.num_programs(1) - 1) & (part == pl.num_programs(2) - 1))
    def _unsort_block():
        chunk = 2 * SORT_TILE
        lane_r = lax.broadcasted_iota(jnp.int32, (chunk, cap), 1)

        def perm_t(rows):
            return jnp.where(lane_r == dest_scr[rows, 0:1].astype(jnp.int32), 1.0, 0.0).astype(BF16)

        xs_scr[...] = out_scr[...].astype(BF16)
        for c in range(tb // chunk):
            rows = slice(c * chunk, (c + 1) * chunk)
            o_ref[rows, :] = jnp.dot(perm_t(rows), xs_scr[...], preferred_element_type=F32)
        xs_scr[...] = (out_scr[...] - xs_scr[...].astype(F32)).astype(BF16)
        for c in range(tb // chunk):
            rows = slice(c * chunk, (c + 1) * chunk)
            moe = o_ref[rows, :] + jnp.dot(perm_t(rows), xs_scr[...], preferred_element_type=F32)
            o_ref[rows, :] = _rmsnorm(y1_ref[rows, :] + moe, gf_ref[...])


def _rider_plan(steps, max_pages, n_seq, pages_per_seq):
    per_blk = MOBA_BLOCK // PAGE_SIZE
    n_page = min(max_pages, n_seq * pages_per_seq // steps) // per_blk * per_blk
    if n_page == 0 or pages_per_seq % n_page or (steps * n_page) % pages_per_seq:
        return 0, 0
    return n_page, steps * n_page // pages_per_seq


def _moe_sorted(h2, comb, y1, wg, wu, wd, g_final, tb, cache_k=None, pt_flat=None, n_page=0):
    t = h2.shape[0]
    epb = EXPERTS_PER_GROUP // EXPERT_SPLIT
    cap = tb + N_GROUPS * SORT_TILE
    grid = (t // tb, N_GROUPS, EXPERT_SPLIT)
    row = lambda b, g, p, *_: (b, 0)
    wblk = lambda b, g, p, *_: (g * EXPERT_SPLIT + p, 0, 0)
    const = lambda b, g, p, *_: (0, 0)
    tri = jnp.tril(jnp.ones((tb, tb), BF16), -1)
    in_specs = [pl.BlockSpec((tb, D_MODEL), row, pipeline_mode=pl.Buffered(1)),
                pl.BlockSpec((tb, ROUTER_LANES), row, pipeline_mode=pl.Buffered(1)),
                pl.BlockSpec((tb, D_MODEL), row, pipeline_mode=pl.Buffered(1)),
                pl.BlockSpec((epb, D_MODEL, D_EXPERT), wblk),
                pl.BlockSpec((epb, D_MODEL, D_EXPERT), wblk),
                pl.BlockSpec((epb, D_EXPERT, D_MODEL), wblk),
                pl.BlockSpec((1, D_MODEL), const),
                pl.BlockSpec((tb, tb), const, pipeline_mode=pl.Buffered(1))]
    out_specs = [pl.BlockSpec((tb, D_MODEL), row)]
    out_shape = [jax.ShapeDtypeStruct((t, D_MODEL), F32)]
    args = [h2, comb, y1, wg, wu, wd, g_final, tri]
    if n_page:
        blk_per_step = n_page * PAGE_SIZE // MOBA_BLOCK
        steps = grid[0] * grid[1] * grid[2]
        step = lambda b, g, p: (b * N_GROUPS + g) * EXPERT_SPLIT + p
        for q in range(n_page):
            in_specs.append(pl.BlockSpec((1, PAGE_SIZE, A_HEADS, A_DH),
                                         lambda b, g, p, pt, q=q: (pt[step(b, g, p) * n_page + q], 0, 0, 0)))
        out_specs.append(pl.BlockSpec((1, blk_per_step, A_HEADS, A_DH), lambda b, g, p, pt: (step(b, g, p), 0, 0, 0)))
        out_shape.append(jax.ShapeDtypeStruct((steps, blk_per_step, A_HEADS, A_DH), F32))
        args = [pt_flat] + args + [cache_k] * n_page
    grid_spec = pltpu.PrefetchScalarGridSpec(
        num_scalar_prefetch=1 if n_page else 0,
        grid=grid,
        in_specs=in_specs,
        out_specs=out_specs,
        scratch_shapes=[pltpu.VMEM((cap, D_MODEL), BF16),
                        pltpu.VMEM((cap, ROUTER_LANES), F32),
                        pltpu.VMEM((cap, D_MODEL), F32),
                        pltpu.VMEM((tb, LANES), F32),
                        pltpu.SMEM((2 * N_GROUPS,), jnp.int32)],
    )
    out = pl.pallas_call(
        functools.partial(_moe_sorted_kernel, n_page),
        grid_spec=grid_spec,
        out_shape=out_shape,
        compiler_params=_params(("parallel", "arbitrary", "arbitrary")),
        name="moe_sorted",
    )(*args)
    return (out[0], out[1].reshape(-1, A_HEADS, A_DH)) if n_page else out[0]


def _inproj_sample_kernel(x_ref, g_ref, w_ref, ta_ref, tb_ref, o_ref):
    h = _rmsnorm(x_ref[...], g_ref[...])
    z = jnp.dot(h, w_ref[...], precision=HI, preferred_element_type=F32)
    for c in range(0, z.shape[1], LANES):
        zh = z[:, c:c + LANES]
        o_ref[:, c:c + LANES] = zh * ta_ref[:, c:c + LANES] + pltpu.roll(zh, A_DH // 2, 1) * tb_ref[:, c:c + LANES]


def _inproj_sample(x, g, w_in, ta, tb, tn=512):
    rows = x.shape[0]
    return pl.pallas_call(
        _inproj_sample_kernel,
        grid=(D_IN // tn,),
        in_specs=[pl.BlockSpec((rows, D_MODEL), lambda j: (0, 0)),
                  pl.BlockSpec((1, D_MODEL), lambda j: (0, 0)),
                  pl.BlockSpec((D_MODEL, tn), lambda j: (0, j)),
                  pl.BlockSpec((rows, tn), lambda j: (0, j)),
                  pl.BlockSpec((rows, tn), lambda j: (0, j))],
        out_specs=pl.BlockSpec((rows, tn), lambda j: (0, j)),
        out_shape=jax.ShapeDtypeStruct((rows, D_IN), F32),
        compiler_params=_params(("parallel",)),
        name="inproj_sample",
    )(x, g, w_in, ta, tb)


def _sample_tables(rows, pos):
    half = A_DH // 2
    inv = ROPE_THETA ** (-jnp.arange(half, dtype=F32) / half)
    ang = jnp.full((rows, 1), pos, F32) * inv[None, :]
    cs = jnp.concatenate([jnp.cos(ang), jnp.cos(ang)], axis=-1)
    sn = jnp.concatenate([-jnp.sin(ang), jnp.sin(ang)], axis=-1)
    ones = jnp.ones((rows, LANES), F32)
    zeros = jnp.zeros((rows, LANES), F32)
    ta, tb = [], []
    for off, width, kind in ((OFF_RQ, R_QK_W, R_DK ** -0.5), (OFF_RK, R_QK_W, 1.0), (OFF_RV, 2 * R_V_W, None),
                             (OFF_AQ, A_W, A_DH ** -0.5), (OFF_AK, A_W, 1.0), (OFF_AV, A_W + 2 * D_MODEL, None)):
        for _ in range(width // LANES):
            ta.append(ones if kind is None else cs * kind)
            tb.append(zeros if kind is None else sn * kind)
    return jnp.concatenate(ta, axis=-1), jnp.concatenate(tb, axis=-1)


def _retention_step_kernel(q_ref, k_ref, v_ref, g_ref, s_ref, dec_ref, o_ref, sn_ref):
    for h in range(R_HEADS):
        q = q_ref[0, h]
        k = k_ref[0, h]
        v = v_ref[0, h]
        s = s_ref[0, h]
        dec = dec_ref[h]
        cross = jnp.sum(q * s, axis=0, keepdims=True) * dec
        inner = jnp.sum(q * k, axis=0, keepdims=True) * v
        sn_ref[0, h] = s * dec + k * v
        o = inner + cross
        mu = jnp.mean(o, axis=-1, keepdims=True)
        d = o - mu
        var = jnp.mean(d * d, axis=-1, keepdims=True)
        gate = g_ref[0, h]
        o_ref[0, h] = gate * jax.nn.sigmoid(gate) * (d * lax.rsqrt(var + EPS))


def _retention_step(q, k, v, g, state, dec):
    nb = q.shape[0]
    col = pl.BlockSpec((1, R_HEADS, R_DK, 1), lambda b: (b, 0, 0, 0))
    rowv = pl.BlockSpec((1, R_HEADS, 1, R_DV), lambda b: (b, 0, 0, 0))
    st = pl.BlockSpec((1, R_HEADS, R_DK, R_DV), lambda b: (b, 0, 0, 0))
    return pl.pallas_call(
        _retention_step_kernel,
        grid=(nb,),
        in_specs=[col, col, rowv, rowv, st, pl.BlockSpec((R_HEADS, 1, R_DV), lambda b: (0, 0, 0))],
        out_specs=[rowv, st],
        out_shape=[jax.ShapeDtypeStruct((nb, R_HEADS, 1, R_DV), F32),
                   jax.ShapeDtypeStruct((nb, R_HEADS, R_DK, R_DV), F32)],
        compiler_params=_params(("parallel",)),
        name="retention_step",
    )(q, k, v, g, state, dec)


def _block_sum_kernel(pt_ref, *refs):
    _sum_pages(refs[:-1], refs[-1].at[0, 0])


def _block_sums(cache_k, pt_flat, nb, n_pages):
    steps = n_pages // PAGES_PER_STEP
    blk_per_step = PAGES_PER_STEP * PAGE_SIZE // MOBA_BLOCK

    def page_spec(p):
        return pl.BlockSpec((1, PAGE_SIZE, A_HEADS, A_DH),
                            lambda b, j, pt: (pt[b * n_pages + j * PAGES_PER_STEP + p], 0, 0, 0))

    grid_spec = pltpu.PrefetchScalarGridSpec(
        num_scalar_prefetch=1,
        grid=(nb, steps),
        in_specs=[page_spec(p) for p in range(PAGES_PER_STEP)],
        out_specs=pl.BlockSpec((1, 1, blk_per_step, A_HEADS, A_DH), lambda b, j, pt: (b, j, 0, 0, 0)),
    )
    out = pl.pallas_call(
        _block_sum_kernel,
        grid_spec=grid_spec,
        out_shape=jax.ShapeDtypeStruct((nb, steps, blk_per_step, A_HEADS, A_DH), F32),
        compiler_params=_params(("parallel", "parallel")),
        name="cache_block_sums",
    )(pt_flat, *([cache_k] * PAGES_PER_STEP))
    return out.reshape(nb, steps * blk_per_step, A_W)


def _block_pick_kernel(q_ref, ks_ref, o_ref):
    ks = ks_ref[0] * (1.0 / MOBA_BLOCK)
    prod = ks * q_ref[0]
    nblk = prod.shape[0]
    lane = lax.broadcasted_iota(jnp.int32, (nblk, LANES), 1)
    gate = jnp.full((nblk, LANES), -jnp.inf, F32)
    for h in range(A_HEADS):
        gate = jnp.where(lane == h, jnp.sum(prod[:, h * A_DH:(h + 1) * A_DH], axis=-1, keepdims=True), gate)
    blk = lax.broadcasted_iota(jnp.int32, gate.shape, 0)
    rank = jnp.zeros(gate.shape, jnp.int32)
    for m in range(nblk):
        gm = gate[m:m + 1, :]
        ahead = (gm > gate) | ((gm == gate) & (m < blk))
        rank = rank + ahead.astype(jnp.int32)
    for j in range(MOBA_TOPK):
        o_ref[0, j:j + 1, :] = jnp.sum(jnp.where(rank == j, blk, 0), axis=0, keepdims=True)


def _block_pick(q, ksum):
    nb, nblk, _ = ksum.shape
    return pl.pallas_call(
        _block_pick_kernel,
        grid=(nb,),
        in_specs=[pl.BlockSpec((1, 1, A_W), lambda b: (b, 0, 0)),
                  pl.BlockSpec((1, nblk, A_W), lambda b: (b, 0, 0))],
        out_specs=pl.BlockSpec((1, MOBA_TOPK, LANES), lambda b: (b, 0, 0)),
        out_shape=jax.ShapeDtypeStruct((nb, MOBA_TOPK, LANES), jnp.int32),
        compiler_params=_params(("parallel",)),
        name="block_pick",
    )(q, ksum)


PICK_PAGES = MOBA_TOPK * (MOBA_BLOCK // PAGE_SIZE)


def _attend_sample_kernel(pages_per_seq, pt_ref, idx_ref, q_ref, kn_ref, vn_ref, ck_ref, cv_ref, o_ref,
                          kbuf, vbuf, ksem, vsem):
    b = pl.program_id(0)
    per_blk = MOBA_BLOCK // PAGE_SIZE

    def page_copies(seq_i, slot):
        out = []
        for h in range(A_HEADS):
            for j in range(MOBA_TOPK):
                blk = idx_ref[(seq_i * MOBA_TOPK + j) * A_HEADS + h]
                for p in range(per_blk):
                    page = pt_ref[seq_i * pages_per_seq + blk * per_blk + p]
                    dst = (slot, h, j * per_blk + p)
                    out.append(pltpu.make_async_copy(ck_ref.at[page, :, h, :], kbuf.at[dst], ksem.at[slot]))
                    out.append(pltpu.make_async_copy(cv_ref.at[page, :, h, :], vbuf.at[dst], vsem.at[slot]))
        return out

    slot = b % 2

    @pl.when(b == 0)
    def _():
        for c in page_copies(0, 0):
            c.start()

    @pl.when(b + 1 < pl.num_programs(0))
    def _():
        for c in page_copies(b + 1, 1 - slot):
            c.start()

    for c in page_copies(b, slot):
        c.wait()

    for h in range(A_HEADS):
        q = q_ref[0, h:h + 1, :]
        s_own = jnp.sum(q * kn_ref[0, h:h + 1, :], axis=-1, keepdims=True)
        scores = [jnp.sum(kbuf[slot, h, g] * q, axis=-1, keepdims=True) for g in range(PICK_PAGES)]
        m = s_own
        for s in scores:
            m = jnp.maximum(m, jnp.max(s, axis=0, keepdims=True))
        p_own = jnp.exp(s_own - m)
        l = p_own
        acc = p_own * vn_ref[0, h:h + 1, :]
        for g, s in enumerate(scores):
            p = jnp.exp(s - m)
            l = l + jnp.sum(p, axis=0, keepdims=True)
            acc = acc + jnp.sum(p * vbuf[slot, h, g], axis=0, keepdims=True)
        o_ref[0, h:h + 1, :] = acc / l


def _attend_sample(q, k_new, v_new, cache_k, cache_v, pt_flat, picks_flat, pages_per_seq):
    nb = q.shape[0]
    vec = pl.BlockSpec((1, A_HEADS, A_DH), lambda b, pt, idx: (b, 0, 0))
    hbm = pl.BlockSpec(memory_space=pl.ANY)
    buf = pltpu.VMEM((2, A_HEADS, PICK_PAGES, PAGE_SIZE, A_DH), F32)
    grid_spec = pltpu.PrefetchScalarGridSpec(
        num_scalar_prefetch=2,
        grid=(nb,),
        in_specs=[vec, vec, vec, hbm, hbm],
        out_specs=vec,
        scratch_shapes=[buf, buf, pltpu.SemaphoreType.DMA((2,)), pltpu.SemaphoreType.DMA((2,))],
    )
    return pl.pallas_call(
        functools.partial(_attend_sample_kernel, pages_per_seq),
        grid_spec=grid_spec,
        out_shape=jax.ShapeDtypeStruct((nb, A_HEADS, A_DH), F32),
        compiler_params=_params(("arbitrary",)),
        name="attend_sample",
    )(pt_flat, picks_flat, q, k_new, v_new, cache_k, cache_v)


def _rope_tables(seq):
    half = A_DH // 2
    inv = ROPE_THETA ** (-jnp.arange(half, dtype=F32) / half)
    ang = jnp.arange(seq, dtype=jnp.int32).astype(F32)[:, None] * inv[None, :]
    cos, sin = jnp.cos(ang), jnp.sin(ang)
    return jnp.concatenate([cos, cos], axis=-1), jnp.concatenate([-sin, sin], axis=-1)


def _router_weights(w_rg, b_rg, w_re, b_re):
    pad = ROUTER_LANES - N_GROUPS - N_EXPERTS
    w = jnp.concatenate([w_rg, w_re.reshape(D_MODEL, N_EXPERTS), jnp.zeros((D_MODEL, pad), F32)], axis=-1)
    b = jnp.concatenate([b_rg, b_re.reshape(N_EXPERTS), jnp.zeros((pad,), F32)])[None, :]
    return w, b


def kernel(x_prompt, x_sample, cache_k, cache_v, state_ret, page_table, norm_mix, w_in, w_ret_o, w_att_o, w_o,
           norm_ffn, w_router_group, b_router_group, w_router_expert, b_router_expert, w_gate, w_up, w_down,
           norm_final):
    nb, seq, _ = x_prompt.shape
    db, ds, _ = x_sample.shape
    depth = norm_mix.shape[0]
    past_len = page_table.shape[1] * PAGE_SIZE
    assert depth == 1 and ds == 1
    assert seq % MOBA_BLOCK == 0 and past_len % (PAGES_PER_STEP * PAGE_SIZE) == 0
    assert past_len // MOBA_BLOCK >= MOBA_TOPK
    l = 0
    t = nb * seq

    g_mix = norm_mix[l][None, :]
    g_ffn = norm_ffn[l][None, :]
    g_fin = norm_final[None, :]
    w_router, b_router = _router_weights(w_router_group[l], b_router_group[l], w_router_expert[l], b_router_expert[l])
    w_in_bf = w_in[l].astype(BF16)
    wg_bf, wu_bf, wd_bf = w_gate[l].astype(BF16), w_up[l].astype(BF16), w_down[l].astype(BF16)
    wro_bf, wao_bf, wo_bf = w_ret_o[l].astype(BF16), w_att_o[l].astype(BF16), w_o[l].astype(BF16)

    xp = x_prompt.reshape(t, D_MODEL)
    cs, sn = _rope_tables(seq)
    rq, rk, rv, rg, aq, kp, vp, gr, ga = _inproj(xp, g_mix, w_in_bf, cs, sn, seq)
    ret, ret_state = _retention(rq, rk, rv, rg, _retention_tables(), nb, seq)
    ck, cv = cache_k[l], cache_v[l]
    n_pages = page_table.shape[1]
    pt_flat = page_table.reshape(-1)
    blocks_per_seq = n_pages * PAGE_SIZE // MOBA_BLOCK
    moe_tb = MOE_BLOCK if t % MOE_BLOCK == 0 else MOE_BLOCK // 2
    moe_steps = (t // moe_tb) * N_GROUPS * EXPERT_SPLIT
    moe_pages, moe_seqs = _rider_plan(moe_steps, MOE_RIDER_PAGES, db, n_pages)
    moba_pages, moba_seqs = _rider_plan(nb * A_HEADS, MOBA_RIDER_PAGES, db - moe_seqs, n_pages)
    ksum_parts = []

    moba_out = _moba(aq, kp, vp, nb, seq, cache_k=ck, pt_flat=pt_flat[moe_seqs * n_pages:], n_page=moba_pages)
    att, ks_moba = moba_out if moba_pages else (moba_out, None)
    y1, h2, comb = _outproj(ret, att, gr, ga, xp, wro_bf, wao_bf, wo_bf, g_ffn, w_router.astype(BF16), b_router,
                            precise=False, tm=256)
    moe_out = _moe_sorted(h2, comb, y1, wg_bf, wu_bf, wd_bf, g_fin, moe_tb,
                          cache_k=ck, pt_flat=pt_flat, n_page=moe_pages)
    y_prompt, ks_moe = moe_out if moe_pages else (moe_out, None)
    y_prompt = y_prompt.reshape(nb, seq, D_MODEL)
    if moe_pages:
        ksum_parts.append(ks_moe.reshape(moe_seqs, blocks_per_seq, A_W))
    if moba_pages:
        ksum_parts.append(ks_moba.reshape(moba_seqs, blocks_per_seq, A_W))
    done_seqs = moe_seqs + moba_seqs
    if done_seqs < db:
        ksum_parts.append(_block_sums(ck, pt_flat[done_seqs * n_pages:], db - done_seqs, n_pages))
    ksum = jnp.concatenate(ksum_parts, axis=0) if len(ksum_parts) > 1 else ksum_parts[0]

    xs = x_sample.reshape(db, D_MODEL)
    ta, tb = _sample_tables(db, past_len)
    zs = _inproj_sample(xs, g_mix, w_in[l], ta, tb)
    seg = lambda off, width: zs[:, off:off + width]
    ld = jnp.log(1.0 - 2.0 ** (-5.0 - jnp.arange(R_HEADS, dtype=F32)))
    dec = jnp.broadcast_to(jnp.exp(ld * 1.0)[:, None, None], (R_HEADS, 1, R_DV))
    ret_s, state_s = _retention_step(
        seg(OFF_RQ, R_QK_W).reshape(db, R_HEADS, R_DK, 1), seg(OFF_RK, R_QK_W).reshape(db, R_HEADS, R_DK, 1),
        seg(OFF_RV, R_V_W).reshape(db, R_HEADS, 1, R_DV), seg(OFF_RG, R_V_W).reshape(db, R_HEADS, 1, R_DV),
        state_ret[l].astype(F32), dec)
    aq_s, ak_s, av_s = seg(OFF_AQ, A_W), seg(OFF_AK, A_W), seg(OFF_AV, A_W)
    picks = _block_pick(aq_s.reshape(db, 1, A_W), ksum)[:, :, :A_HEADS].reshape(-1)
    heads = lambda z: z.reshape(db, A_HEADS, A_DH)
    att_s = _attend_sample(heads(aq_s), heads(ak_s), heads(av_s), ck, cv, pt_flat, picks, n_pages)
    y1_s, h2_s, comb_s = _outproj(ret_s.reshape(db, R_V_W), att_s.reshape(db, A_W), seg(OFF_GR, D_MODEL),
                                  seg(OFF_GA, D_MODEL), xs, w_ret_o[l], w_att_o[l], w_o[l], g_ffn,
                                  w_router, b_router, precise=True, tm=db)
    y_sample = _moe(h2_s, comb_s, y1_s, wg_bf, wu_bf, wd_bf, g_fin, tm=db).reshape(db, ds, D_MODEL)

    k_prompt = kp.reshape(1, nb, seq, A_HEADS, A_DH)
    v_prompt = vp.reshape(1, nb, seq, A_HEADS, A_DH)
    ret_prompt = ret_state[None]
    k_sample = ak_s.reshape(1, db, ds, A_HEADS, A_DH).astype(cache_k.dtype)
    v_sample = av_s.reshape(1, db, ds, A_HEADS, A_DH).astype(cache_v.dtype)
    ret_sample = state_s[None].astype(state_ret.dtype)
    return (y_prompt, y_sample, k_prompt, v_prompt, ret_prompt, k_sample, v_sample, ret_sample)
```

```python
import functools

import jax
import jax.numpy as jnp
from jax import lax
from jax.experimental import pallas as pl
from jax.experimental.pallas import tpu as pltpu

D_MODEL = 1024
PAGE_SIZE = 128
R_HEADS = 4
R_DK = 128
R_DV = 256
R_CHUNK = 128
A_HEADS = 8
A_DH = 128
MOBA_BLOCK = 256
MOBA_TOPK = 3
ROPE_THETA = 10000.0
N_GROUPS = 4
EXPERTS_PER_GROUP = 8
N_EXPERTS = N_GROUPS * EXPERTS_PER_GROUP
D_EXPERT = 256
EPS = 1e-6

R_QK_W = R_HEADS * R_DK
R_V_W = R_HEADS * R_DV
A_W = A_HEADS * A_DH
D_IN = 2 * R_QK_W + 2 * R_V_W + 3 * A_W + 2 * D_MODEL
OFF_RQ = 0
OFF_RK = OFF_RQ + R_QK_W
OFF_RV = OFF_RK + R_QK_W
OFF_RG = OFF_RV + R_V_W
OFF_AQ = OFF_RG + R_V_W
OFF_AK = OFF_AQ + A_W
OFF_AV = OFF_AK + A_W
OFF_GR = OFF_AV + A_W
OFF_GA = OFF_GR + D_MODEL

LANES = 128
ROUTER_LANES = LANES
EXPERT_LANE0 = N_GROUPS
GROUP_ID_LANE = 0
PAGES_PER_STEP = 8
OUTPROJ_TM = 512
MOE_BLOCK = 1024
MOE_RIDER_PAGES = 8
MOBA_RIDER_PAGES = 16
SAMPLE_SEQS_PER_STEP = 4
VMEM_LIMIT = 56 * 1024 * 1024

F32 = jnp.float32
BF16 = jnp.bfloat16
HI = lax.Precision.HIGHEST
NT = (((1,), (1,)), ((), ()))


def _params(sem, vmem=VMEM_LIMIT):
    return pltpu.CompilerParams(dimension_semantics=sem, vmem_limit_bytes=vmem)


def _rmsnorm(x, g):
    return x * lax.rsqrt(jnp.mean(x * x, axis=-1, keepdims=True) + EPS) * g


def _rope_head(z, cs, sn):
    return z * cs + pltpu.roll(z, A_DH // 2, 1) * sn


def _inproj_kernel(x_ref, g_ref, w_ref, cs_ref, sn_ref,
                   rq_ref, rk_ref, rv_ref, rg_ref, aq_ref, k_ref, v_ref, gr_ref, ga_ref):
    h = _rmsnorm(x_ref[...], g_ref[...]).astype(BF16)
    cs = cs_ref[...]
    sn = sn_ref[...]

    def z_cols(c0):
        return jnp.dot(h, w_ref[:, c0:c0 + 2 * LANES], preferred_element_type=F32)

    def plain(off, width, out_ref):
        for c in range(0, width, 2 * LANES):
            out_ref[:, c:c + 2 * LANES] = z_cols(off + c).astype(out_ref.dtype)

    def roped(off, width, out_ref, scale):
        for c in range(0, width, 2 * LANES):
            z = z_cols(off + c)
            for hh in range(2):
                r = _rope_head(z[:, hh * LANES:(hh + 1) * LANES], cs, sn)
                if scale is not None:
                    r = r * scale
                out_ref[:, c + hh * LANES:c + (hh + 1) * LANES] = r.astype(out_ref.dtype)

    roped(OFF_RQ, R_QK_W, rq_ref, R_DK ** -0.5)
    roped(OFF_RK, R_QK_W, rk_ref, None)
    plain(OFF_RV, R_V_W, rv_ref)
    plain(OFF_RG, R_V_W, rg_ref)
    roped(OFF_AQ, A_W, aq_ref, A_DH ** -0.5)
    roped(OFF_AK, A_W, k_ref, None)
    plain(OFF_AV, A_W, v_ref)
    plain(OFF_GR, D_MODEL, gr_ref)
    plain(OFF_GA, D_MODEL, ga_ref)


def _inproj(x, g, w_bf, cs, sn, seq, tm=256):
    t = x.shape[0]
    nseq = seq // tm
    row = lambda i: (i, 0)
    tab = lambda i: (i % nseq, 0)
    outs = [(R_QK_W, BF16), (R_QK_W, BF16), (R_V_W, BF16), (R_V_W, F32), (A_W, BF16),
            (A_W, F32), (A_W, F32), (D_MODEL, F32), (D_MODEL, F32)]
    return pl.pallas_call(
        _inproj_kernel,
        grid=(t // tm,),
        in_specs=[pl.BlockSpec((tm, D_MODEL), row),
                  pl.BlockSpec((1, D_MODEL), lambda i: (0, 0)),
                  pl.BlockSpec((D_MODEL, D_IN), lambda i: (0, 0), pipeline_mode=pl.Buffered(1)),
                  pl.BlockSpec((tm, LANES), tab),
                  pl.BlockSpec((tm, LANES), tab)],
        out_specs=[pl.BlockSpec((tm, w), row) for w, _ in outs],
        out_shape=[jax.ShapeDtypeStruct((t, w), d) for w, d in outs],
        compiler_params=_params(("parallel",)),
        name="inproj",
    )(x, g, w_bf, cs, sn)


def _retention_kernel(q_ref, k_ref, v_ref, g_ref, dm_ref, cd_ref, kd_ref, sd_ref, o_ref, st_ref, s_scr):
    nchunk = q_ref.shape[0] // R_CHUNK
    s_scr[...] = jnp.zeros_like(s_scr)
    for c in range(nchunk):
        rows = slice(c * R_CHUNK, (c + 1) * R_CHUNK)
        q = q_ref[rows, :]
        k = k_ref[rows, :]
        v = v_ref[rows, :]
        s = s_scr[...]
        scores = lax.dot_general(q, k, NT, preferred_element_type=F32) * dm_ref[0]
        inner = jnp.dot(scores.astype(BF16), v, preferred_element_type=F32)
        cross = jnp.dot(q, s.astype(BF16), preferred_element_type=F32) * cd_ref[0]
        k_dec_t = (k.astype(F32) * kd_ref[0]).T.astype(BF16)
        s_scr[...] = s * sd_ref[0] + jnp.dot(k_dec_t, v, preferred_element_type=F32)
        o = inner + cross
        mu = jnp.mean(o, axis=-1, keepdims=True)
        d = o - mu
        var = jnp.mean(d * d, axis=-1, keepdims=True)
        y = d * lax.rsqrt(var + EPS)
        gate = g_ref[rows, :]
        o_ref[rows, :] = (gate * jax.nn.sigmoid(gate) * y).astype(o_ref.dtype)
    st_ref[0, 0] = s_scr[...]


def _retention(rq, rk, rv, rg, tabs, nb, seq):
    dm, cd, kd, sd = tabs
    head_tab = lambda shp: pl.BlockSpec((1,) + shp, lambda b, h: (h, 0, 0))
    return pl.pallas_call(
        _retention_kernel,
        grid=(nb, R_HEADS),
        in_specs=[pl.BlockSpec((seq, R_DK), lambda b, h: (b, h)),
                  pl.BlockSpec((seq, R_DK), lambda b, h: (b, h)),
                  pl.BlockSpec((seq, R_DV), lambda b, h: (b, h)),
                  pl.BlockSpec((seq, R_DV), lambda b, h: (b, h)),
                  head_tab((R_CHUNK, R_CHUNK)), head_tab((R_CHUNK, R_DV)),
                  head_tab((R_CHUNK, R_DK)), head_tab((R_DK, R_DV))],
        out_specs=[pl.BlockSpec((seq, R_DV), lambda b, h: (b, h)),
                   pl.BlockSpec((1, 1, R_DK, R_DV), lambda b, h: (b, h, 0, 0))],
        out_shape=[jax.ShapeDtypeStruct((nb * seq, R_V_W), BF16),
                   jax.ShapeDtypeStruct((nb, R_HEADS, R_DK, R_DV), F32)],
        scratch_shapes=[pltpu.VMEM((R_DK, R_DV), F32)],
        compiler_params=_params(("parallel", "parallel")),
        name="retention",
    )(rq, rk, rv, rg, dm, cd, kd, sd)


def _retention_tables():
    ld = jnp.log(1.0 - 2.0 ** (-5.0 - jnp.arange(R_HEADS, dtype=F32)))[:, None, None]
    i = jnp.arange(R_CHUNK, dtype=F32)
    diff = i[:, None] - i[None, :]
    dmask = jnp.where(diff >= 0, jnp.exp(ld * jnp.maximum(diff, 0.0)), 0.0)
    cdec = jnp.broadcast_to(jnp.exp(ld * (i + 1.0)[None, :, None]), (R_HEADS, R_CHUNK, R_DV))
    kdec = jnp.broadcast_to(jnp.exp(ld * (R_CHUNK - 1.0 - i)[None, :, None]), (R_HEADS, R_CHUNK, R_DK))
    sdec = jnp.broadcast_to(jnp.exp(ld * R_CHUNK), (R_HEADS, R_DK, R_DV))
    return dmask, cdec, kdec, sdec


MASK_BIAS = -1e30


def _moba_kernel(n_page, *refs):
    if n_page:
        refs = refs[1:]
    q_ref, k_ref, v_ref = refs[:3]
    pages = refs[3:3 + n_page]
    if n_page:
        o_ref, ks_ref, kb_scr, vb_scr, km_scr, s_scr, p_scr = refs[3 + n_page:]
        _sum_pages(pages, ks_ref.at[0])
    else:
        o_ref, kb_scr, vb_scr, km_scr, s_scr, p_scr = refs[3:]
    blk_sz = MOBA_BLOCK
    nblk = k_ref.shape[0] // blk_sz
    lane = lax.broadcasted_iota(jnp.int32, (blk_sz, A_DH), 1)
    for n in range(nblk):
        rows = slice(n * blk_sz, (n + 1) * blk_sz)
        kn = k_ref[rows, :]
        kb_scr[rows, 0:A_DH] = kn.astype(BF16)
        kb_scr[rows, A_DH:2 * A_DH] = jnp.where(lane == n, 1.0, 0.0).astype(BF16)
        km_scr[n:n + 1, :] = jnp.mean(kn, axis=0, keepdims=True)
        vb_scr[rows, 0:A_DH] = v_ref[rows, :].astype(BF16)
        vb_scr[rows, A_DH:2 * A_DH] = jnp.ones((blk_sz, A_DH), BF16)
    kmean = km_scr[...].astype(BF16)

    for i in range(nblk):
        q = q_ref[i * blk_sz:(i + 1) * blk_sz, :]
        gate = lax.dot_general(kmean, q, NT, preferred_element_type=F32)
        blk = lax.broadcasted_iota(jnp.int32, gate.shape, 0)
        rank = jnp.zeros(gate.shape, jnp.int32)
        for m in range(i):
            gm = gate[m:m + 1, :]
            ahead = (gm > gate) | ((gm == gate) & (m < blk))
            rank = rank + ahead.astype(jnp.int32)
        bias_t = jnp.where((blk < i) & (rank >= MOBA_TOPK), MASK_BIAS, 0.0)
        bias_t = jnp.concatenate([bias_t, jnp.zeros((A_DH - nblk, blk_sz), F32)], axis=0)
        q_aug = jnp.concatenate([q, bias_t.T.astype(BF16)], axis=1)

        nk = (i + 1) * blk_sz
        half = blk_sz // 2
        m_run = jnp.full((blk_sz, half), -jnp.inf, F32)
        for n in range(i + 1):
            cols = slice(n * blk_sz, (n + 1) * blk_sz)
            s = lax.dot_general(q_aug, kb_scr[cols, :], NT, preferred_element_type=F32)
            if n == i:
                qpos = lax.broadcasted_iota(jnp.int32, s.shape, 0)
                kpos = lax.broadcasted_iota(jnp.int32, s.shape, 1)
                s = jnp.where(kpos <= qpos, s, -jnp.inf)
            s_scr[i % 2, :, cols] = s
            m_run = jnp.maximum(m_run, jnp.maximum(s[:, :half], s[:, half:]))
        m_row = jnp.max(m_run, axis=-1, keepdims=True)
        for n in range(i + 1):
            cols = slice(n * blk_sz, (n + 1) * blk_sz)
            p_scr[i % 2, :, cols] = jnp.exp(s_scr[i % 2, :, cols] - m_row).astype(BF16)
        o = jnp.dot(p_scr[i % 2, :, 0:nk], vb_scr[0:nk, :], preferred_element_type=F32)
        o_ref[i * blk_sz:(i + 1) * blk_sz, :] = (o[:, :A_DH] / o[:, A_DH:]).astype(o_ref.dtype)


def _moba(aq, k, v, nb, seq, cache_k=None, pt_flat=None, n_page=0):
    nblk = seq // MOBA_BLOCK
    assert nblk <= A_DH
    blk = lambda b, h, *_: (b, h)
    in_specs = [pl.BlockSpec((seq, A_DH), blk)] * 3
    out_specs = [pl.BlockSpec((seq, A_DH), blk)]
    out_shape = [jax.ShapeDtypeStruct((nb * seq, A_W), BF16)]
    args = [aq, k, v]
    if n_page:
        blk_per_step = n_page * PAGE_SIZE // MOBA_BLOCK
        for q in range(n_page):
            in_specs.append(pl.BlockSpec((1, PAGE_SIZE, A_HEADS, A_DH),
                                         lambda b, h, pt, q=q: (pt[(b * A_HEADS + h) * n_page + q], 0, 0, 0)))
        out_specs.append(pl.BlockSpec((1, blk_per_step, A_HEADS, A_DH), lambda b, h, pt: (b * A_HEADS + h, 0, 0, 0)))
        out_shape.append(jax.ShapeDtypeStruct((nb * A_HEADS, blk_per_step, A_HEADS, A_DH), F32))
        args = [pt_flat] + args + [cache_k] * n_page
    grid_spec = pltpu.PrefetchScalarGridSpec(
        num_scalar_prefetch=1 if n_page else 0,
        grid=(nb, A_HEADS),
        in_specs=in_specs,
        out_specs=out_specs,
        scratch_shapes=[pltpu.VMEM((seq, 2 * A_DH), BF16),
                        pltpu.VMEM((seq, 2 * A_DH), BF16),
                        pltpu.VMEM((nblk, A_DH), F32),
                        pltpu.VMEM((2, MOBA_BLOCK, seq), F32),
                        pltpu.VMEM((2, MOBA_BLOCK, seq), BF16)],
    )
    out = pl.pallas_call(
        functools.partial(_moba_kernel, n_page),
        grid_spec=grid_spec,
        out_shape=out_shape,
        compiler_params=_params(("parallel", "parallel")),
        name="moba",
    )(*args)
    return (out[0], out[1].reshape(-1, A_HEADS, A_DH)) if n_page else out[0]


def _route(logits):
    lane = lax.broadcasted_iota(jnp.int32, logits.shape, 1)
    ninf = -jnp.inf
    big = jnp.int32(ROUTER_LANES)
    glog = jnp.where(lane < N_GROUPS, logits, ninf)
    gmax = jnp.max(glog, axis=-1, keepdims=True)
    g_idx = jnp.min(jnp.where(glog == gmax, lane, big), axis=-1, keepdims=True)
    g_w = 1.0 / jnp.sum(jnp.exp(glog - gmax), axis=-1, keepdims=True)
    e_lo = EXPERT_LANE0 + g_idx * EXPERTS_PER_GROUP
    in_group = (lane >= e_lo) & (lane < e_lo + EXPERTS_PER_GROUP)
    e1 = jnp.where(in_group, logits, ninf)
    v1 = jnp.max(e1, axis=-1, keepdims=True)
    i1 = jnp.min(jnp.where(e1 == v1, lane, big), axis=-1, keepdims=True)
    e2 = jnp.where(lane == i1, ninf, e1)
    v2 = jnp.max(e2, axis=-1, keepdims=True)
    i2 = jnp.min(jnp.where(e2 == v2, lane, big), axis=-1, keepdims=True)
    x2 = jnp.exp(v2 - v1)
    den = 1.0 + x2
    comb = jnp.where(lane == i1, g_w * (1.0 / den), jnp.where(lane == i2, g_w * (x2 / den), 0.0))
    return jnp.where(lane == GROUP_ID_LANE, g_idx.astype(F32), comb)


def _outproj_kernel(precise, ret_ref, att_ref, gr_ref, ga_ref, x_ref, wr_ref, wa_ref, wo_ref,
                    gf_ref, wrt_ref, brt_ref, y1_ref, h2_ref, comb_ref):
    prec = HI if precise else None
    mm = F32 if precise else BF16
    r = jnp.dot(ret_ref[...].astype(mm), wr_ref[...], precision=prec, preferred_element_type=F32)
    a = jnp.dot(att_ref[...].astype(mm), wa_ref[...], precision=prec, preferred_element_type=F32)
    merged = jax.nn.sigmoid(gr_ref[...]) * r + jax.nn.sigmoid(ga_ref[...]) * a
    m = jnp.dot(merged.astype(mm), wo_ref[...], precision=prec, preferred_element_type=F32)
    y1 = x_ref[...] + m
    y1_ref[...] = y1
    h2 = _rmsnorm(y1, gf_ref[...])
    h2_ref[...] = h2.astype(h2_ref.dtype)
    logits = jnp.dot(h2.astype(mm), wrt_ref[...], precision=prec, preferred_element_type=F32) + brt_ref[...]
    comb_ref[...] = _route(logits)


def _outproj(ret, att, gr, ga, x, w_ret_o, w_att_o, w_o, g_ffn, w_router, b_router, precise, tm):
    t = x.shape[0]
    row = lambda i: (i, 0)
    full = lambda i: (0, 0)
    wspec = lambda: pl.BlockSpec((D_MODEL, D_MODEL), full, pipeline_mode=pl.Buffered(1))
    return pl.pallas_call(
        functools.partial(_outproj_kernel, precise),
        grid=(t // tm,),
        in_specs=[pl.BlockSpec((tm, D_MODEL), row)] * 5 + [wspec(), wspec(), wspec(),
                  pl.BlockSpec((1, D_MODEL), full),
                  pl.BlockSpec(w_router.shape, full),
                  pl.BlockSpec((1, ROUTER_LANES), full)],
        out_specs=[pl.BlockSpec((tm, D_MODEL), row), pl.BlockSpec((tm, D_MODEL), row),
                   pl.BlockSpec((tm, ROUTER_LANES), row)],
        out_shape=[jax.ShapeDtypeStruct((t, D_MODEL), F32), jax.ShapeDtypeStruct((t, D_MODEL), BF16),
                   jax.ShapeDtypeStruct((t, ROUTER_LANES), F32)],
        compiler_params=_params(("parallel",)),
        name="outproj_precise" if precise else "outproj",
    )(ret, att, gr, ga, x, w_ret_o, w_att_o, w_o, g_ffn, w_router, b_router)


def _sum_pages(pages, out_block):
    per_blk = MOBA_BLOCK // PAGE_SIZE
    for n in range(len(pages) // per_blk):
        acc = jnp.sum(pages[n * per_blk][0], axis=0)
        for p in range(1, per_blk):
            acc = acc + jnp.sum(pages[n * per_blk + p][0], axis=0)
        out_block[n] = acc


def _moe_kernel(epb, h_ref, comb_ref, y1_ref, wg_ref, wu_ref, wd_ref, gf_ref, o_ref, acc_ref):
    e = pl.program_id(1)

    @pl.when(e == 0)
    def _():
        acc_ref[...] = jnp.zeros_like(acc_ref)

    h = h_ref[...]
    comb = comb_ref[...]
    lane = lax.broadcasted_iota(jnp.int32, comb.shape, 1)
    for j in range(epb):
        c = jnp.sum(jnp.where(lane == EXPERT_LANE0 + e * epb + j, comb, 0.0), axis=-1, keepdims=True)
        a = jnp.dot(h, wg_ref[j], preferred_element_type=F32)
        u = jnp.dot(h, wu_ref[j], preferred_element_type=F32)
        act = (a * jax.nn.sigmoid(a)) * u * c
        acc_ref[...] += jnp.dot(act.astype(BF16), wd_ref[j], preferred_element_type=F32)

    @pl.when(e == pl.num_programs(1) - 1)
    def _():
        o_ref[...] = _rmsnorm(y1_ref[...] + acc_ref[...], gf_ref[...])


def _moe(h2, comb, y1, wg, wu, wd, g_final, tm, epb=4):
    t = h2.shape[0]
    row = lambda i, e: (i, 0)
    wblk = lambda i, e: (e, 0, 0)
    return pl.pallas_call(
        functools.partial(_moe_kernel, epb),
        grid=(t // tm, N_EXPERTS // epb),
        in_specs=[pl.BlockSpec((tm, D_MODEL), row), pl.BlockSpec((tm, ROUTER_LANES), row),
                  pl.BlockSpec((tm, D_MODEL), row),
                  pl.BlockSpec((epb, D_MODEL, D_EXPERT), wblk),
                  pl.BlockSpec((epb, D_MODEL, D_EXPERT), wblk),
                  pl.BlockSpec((epb, D_EXPERT, D_MODEL), wblk),
                  pl.BlockSpec((1, D_MODEL), lambda i, e: (0, 0))],
        out_specs=pl.BlockSpec((tm, D_MODEL), row),
        out_shape=jax.ShapeDtypeStruct((t, D_MODEL), F32),
        scratch_shapes=[pltpu.VMEM((tm, D_MODEL), F32)],
        compiler_params=_params(("parallel", "arbitrary")),
        name="moe",
    )(h2, comb, y1, wg, wu, wd, g_final)


SORT_TILE = 128
EXPERT_SPLIT = 2
COMB_LANES = 40
assert EXPERT_LANE0 + N_EXPERTS <= COMB_LANES and 3 * COMB_LANES <= LANES


def _moe_sorted_kernel(n_page, *refs):
    if n_page:
        refs = refs[1:]
    h_ref, comb_ref, y1_ref, wg_ref, wu_ref, wd_ref, gf_ref, tri_ref = refs[:8]
    pages = refs[8:8 + n_page]
    rest = refs[8 + n_page:]
    if n_page:
        o_ref, ks_ref = rest[:2]
        _sum_pages(pages, ks_ref.at[0])
        rest = rest[2:]
    else:
        o_ref, rest = rest[0], rest[1:]
    xs_scr, cs_scr, out_scr, dest_scr, run_smem = rest
    g = pl.program_id(1)
    part = pl.program_id(2)
    tb = h_ref.shape[0]
    cap = xs_scr.shape[0]
    epb = wg_ref.shape[0]

    @pl.when((g == 0) & (part == 0))
    def _sort_block():
        comb = comb_ref[...]
        lane = lax.broadcasted_iota(jnp.int32, comb.shape, 1)
        gcol = comb[:, GROUP_ID_LANE:GROUP_ID_LANE + 1]
        ind = jnp.where((lane < N_GROUPS) & (lane.astype(F32) == gcol), 1.0, 0.0)
        pos = jnp.dot(tri_ref[...], ind.astype(BF16), preferred_element_type=F32)
        dest = jnp.zeros((tb, 1), F32)
        start = jnp.int32(0)
        for gg in range(N_GROUPS):
            col = ind[:, gg:gg + 1]
            n_tiles = (jnp.sum(col).astype(jnp.int32) + SORT_TILE - 1) // SORT_TILE
            run_smem[gg] = start
            run_smem[N_GROUPS + gg] = n_tiles
            dest = dest + col * (pos[:, gg:gg + 1] + start.astype(F32))
            start = start + n_tiles * SORT_TILE
        dest_b = jnp.broadcast_to(dest, (tb, LANES))
        dest_scr[...] = dest_b
        dest_row = dest_b.T[0:1, :].astype(jnp.int32)
        c_hi = comb.astype(BF16).astype(F32)
        r1 = comb - c_hi
        c_mid = r1.astype(BF16).astype(F32)
        c_lo = r1 - c_mid
        packed = (c_hi + pltpu.roll(c_mid, COMB_LANES, 1) + pltpu.roll(c_lo, 2 * COMB_LANES, 1)).astype(BF16)
        lane_t = lax.broadcasted_iota(jnp.int32, (SORT_TILE, ROUTER_LANES), 1)
        h = h_ref[...]
        for c in range(cap // SORT_TILE):
            rows = slice(c * SORT_TILE, (c + 1) * SORT_TILE)

            def sort_rows(c=c, rows=rows):
                riota = lax.broadcasted_iota(jnp.int32, (SORT_TILE, tb), 0) + c * SORT_TILE
                perm = jnp.where(riota == dest_row, 1.0, 0.0).astype(BF16)
                xs_scr[rows, :] = jnp.dot(perm, h, preferred_element_type=F32).astype(BF16)
                pk = jnp.dot(perm, packed, preferred_element_type=F32)
                terms = (pk + pltpu.roll(pk, ROUTER_LANES - COMB_LANES, 1)
                         + pltpu.roll(pk, ROUTER_LANES - 2 * COMB_LANES, 1))
                cs_scr[rows, :] = jnp.where(lane_t < COMB_LANES, terms, 0.0)

            if c * SORT_TILE < tb:
                sort_rows()
            else:
                pl.when(c * SORT_TILE < start)(sort_rows)
        out_scr[...] = jnp.zeros_like(out_scr)

    start = run_smem[g]
    n_tiles = run_smem[N_GROUPS + g]
    lane_c = lax.broadcasted_iota(jnp.int32, (SORT_TILE, ROUTER_LANES), 1)
    lane0 = EXPERT_LANE0 + g * EXPERTS_PER_GROUP + part * epb

    def run_tile(t, carry):
        r0 = pl.multiple_of(start + t * SORT_TILE, SORT_TILE)
        x = xs_scr[pl.ds(r0, SORT_TILE), :]
        cs = cs_scr[pl.ds(r0, SORT_TILE), :]
        acc = jnp.zeros((SORT_TILE, D_MODEL), F32)
        for j in range(epb):
            c = jnp.sum(jnp.where(lane_c == lane0 + j, cs, 0.0), axis=-1, keepdims=True)
            a = jnp.dot(x, wg_ref[j], preferred_element_type=F32)
            u = jnp.dot(x, wu_ref[j], preferred_element_type=F32)
            act = (a * jax.nn.sigmoid(a)) * u * c
            acc = acc + jnp.dot(act.astype(BF16), wd_ref[j], preferred_element_type=F32)
        out_scr[pl.ds(r0, SORT_TILE), :] += acc
        return carry

    lax.fori_loop(0, n_tiles, run_tile, 0)

    @pl.when((g == pl.num_programs(1) - 1) & (part == pl.num_programs(2) - 1))
    def _unsort_block():
        chunk = 2 * SORT_TILE
        lane_r = lax.broadcasted_iota(jnp.int32, (chunk, cap), 1)

        def perm_t(rows):
            return jnp.where(lane_r == dest_scr[rows, 0:1].astype(jnp.int32), 1.0, 0.0).astype(BF16)

        xs_scr[...] = out_scr[...].astype(BF16)
        for c in range(tb // chunk):
            rows = slice(c * chunk, (c + 1) * chunk)
            o_ref[rows, :] = jnp.dot(perm_t(rows), xs_scr[...], preferred_element_type=F32)
        xs_scr[...] = (out_scr[...] - xs_scr[...].astype(F32)).astype(BF16)
        for c in range(tb // chunk):
            rows = slice(c * chunk, (c + 1) * chunk)
            moe = o_ref[rows, :] + jnp.dot(perm_t(rows), xs_scr[...], preferred_element_type=F32)
            o_ref[rows, :] = _rmsnorm(y1_ref[rows, :] + moe, gf_ref[...])


def _rider_plan(steps, max_pages, n_seq, pages_per_seq):
    per_blk = MOBA_BLOCK // PAGE_SIZE
    n_page = min(max_pages, n_seq * pages_per_seq // steps) // per_blk * per_blk
    if n_page == 0 or pages_per_seq % n_page or (steps * n_page) % pages_per_seq:
        return 0, 0
    return n_page, steps * n_page // pages_per_seq


def _moe_sorted(h2, comb, y1, wg, wu, wd, g_final, tb, cache_k=None, pt_flat=None, n_page=0):
    t = h2.shape[0]
    epb = EXPERTS_PER_GROUP // EXPERT_SPLIT
    cap = tb + N_GROUPS * SORT_TILE
    grid = (t // tb, N_GROUPS, EXPERT_SPLIT)
    row = lambda b, g, p, *_: (b, 0)
    wblk = lambda b, g, p, *_: (g * EXPERT_SPLIT + p, 0, 0)
    const = lambda b, g, p, *_: (0, 0)
    tri = jnp.tril(jnp.ones((tb, tb), BF16), -1)
    in_specs = [pl.BlockSpec((tb, D_MODEL), row, pipeline_mode=pl.Buffered(1)),
                pl.BlockSpec((tb, ROUTER_LANES), row, pipeline_mode=pl.Buffered(1)),
                pl.BlockSpec((tb, D_MODEL), row, pipeline_mode=pl.Buffered(1)),
                pl.BlockSpec((epb, D_MODEL, D_EXPERT), wblk),
                pl.BlockSpec((epb, D_MODEL, D_EXPERT), wblk),
                pl.BlockSpec((epb, D_EXPERT, D_MODEL), wblk),
                pl.BlockSpec((1, D_MODEL), const),
                pl.BlockSpec((tb, tb), const, pipeline_mode=pl.Buffered(1))]
    out_specs = [pl.BlockSpec((tb, D_MODEL), row)]
    out_shape = [jax.ShapeDtypeStruct((t, D_MODEL), F32)]
    args = [h2, comb, y1, wg, wu, wd, g_final, tri]
    if n_page:
        blk_per_step = n_page * PAGE_SIZE // MOBA_BLOCK
        steps = grid[0] * grid[1] * grid[2]
        step = lambda b, g, p: (b * N_GROUPS + g) * EXPERT_SPLIT + p
        for q in range(n_page):
            in_specs.append(pl.BlockSpec((1, PAGE_SIZE, A_HEADS, A_DH),
                                         lambda b, g, p, pt, q=q: (pt[step(b, g, p) * n_page + q], 0, 0, 0)))
        out_specs.append(pl.BlockSpec((1, blk_per_step, A_HEADS, A_DH), lambda b, g, p, pt: (step(b, g, p), 0, 0, 0)))
        out_shape.append(jax.ShapeDtypeStruct((steps, blk_per_step, A_HEADS, A_DH), F32))
        args = [pt_flat] + args + [cache_k] * n_page
    grid_spec = pltpu.PrefetchScalarGridSpec(
        num_scalar_prefetch=1 if n_page else 0,
        grid=grid,
        in_specs=in_specs,
        out_specs=out_specs,
        scratch_shapes=[pltpu.VMEM((cap, D_MODEL), BF16),
                        pltpu.VMEM((cap, ROUTER_LANES), F32),
                        pltpu.VMEM((cap, D_MODEL), F32),
                        pltpu.VMEM((tb, LANES), F32),
                        pltpu.SMEM((2 * N_GROUPS,), jnp.int32)],
    )
    out = pl.pallas_call(
        functools.partial(_moe_sorted_kernel, n_page),
        grid_spec=grid_spec,
        out_shape=out_shape,
        compiler_params=_params(("parallel", "arbitrary", "arbitrary")),
        name="moe_sorted",
    )(*args)
    return (out[0], out[1].reshape(-1, A_HEADS, A_DH)) if n_page else out[0]


def _inproj_sample_kernel(x_ref, g_ref, w_ref, ta_ref, tb_ref, o_ref):
    h = _rmsnorm(x_ref[...], g_ref[...])
    z = jnp.dot(h, w_ref[...], precision=HI, preferred_element_type=F32)
    for c in range(0, z.shape[1], LANES):
        zh = z[:, c:c + LANES]
        o_ref[:, c:c + LANES] = zh * ta_ref[:, c:c + LANES] + pltpu.roll(zh, A_DH // 2, 1) * tb_ref[:, c:c + LANES]


def _inproj_sample(x, g, w_in, ta, tb, tn=512):
    rows = x.shape[0]
    return pl.pallas_call(
        _inproj_sample_kernel,
        grid=(D_IN // tn,),
        in_specs=[pl.BlockSpec((rows, D_MODEL), lambda j: (0, 0)),
                  pl.BlockSpec((1, D_MODEL), lambda j: (0, 0)),
                  pl.BlockSpec((D_MODEL, tn), lambda j: (0, j)),
                  pl.BlockSpec((rows, tn), lambda j: (0, j)),
                  pl.BlockSpec((rows, tn), lambda j: (0, j))],
        out_specs=pl.BlockSpec((rows, tn), lambda j: (0, j)),
        out_shape=jax.ShapeDtypeStruct((rows, D_IN), F32),
        compiler_params=_params(("parallel",)),
        name="inproj_sample",
    )(x, g, w_in, ta, tb)


def _sample_tables(rows, pos):
    half = A_DH // 2
    inv = ROPE_THETA ** (-jnp.arange(half, dtype=F32) / half)
    ang = jnp.full((rows, 1), pos, F32) * inv[None, :]
    cs = jnp.concatenate([jnp.cos(ang), jnp.cos(ang)], axis=-1)
    sn = jnp.concatenate([-jnp.sin(ang), jnp.sin(ang)], axis=-1)
    ones = jnp.ones((rows, LANES), F32)
    zeros = jnp.zeros((rows, LANES), F32)
    ta, tb = [], []
    for off, width, kind in ((OFF_RQ, R_QK_W, R_DK ** -0.5), (OFF_RK, R_QK_W, 1.0), (OFF_RV, 2 * R_V_W, None),
                             (OFF_AQ, A_W, A_DH ** -0.5), (OFF_AK, A_W, 1.0), (OFF_AV, A_W + 2 * D_MODEL, None)):
        for _ in range(width // LANES):
            ta.append(ones if kind is None else cs * kind)
            tb.append(zeros if kind is None else sn * kind)
    return jnp.concatenate(ta, axis=-1), jnp.concatenate(tb, axis=-1)


def _retention_step_kernel(q_ref, k_ref, v_ref, g_ref, s_ref, dec_ref, o_ref, sn_ref):
    for i in range(q_ref.shape[0]):
        for h in range(R_HEADS):
            col = lambda r: jnp.broadcast_to(r[i, h:h + 1, :], (R_DK, R_DK)).T[:, 0:1]
            q = col(q_ref)
            k = col(k_ref)
            v = v_ref[i, h]
            s = s_ref[i, h]
            dec = dec_ref[h]
            cross = jnp.sum(q * s, axis=0, keepdims=True) * dec
            inner = jnp.sum(q * k, axis=0, keepdims=True) * v
            sn_ref[i, h] = s * dec + k * v
            o = inner + cross
            mu = jnp.mean(o, axis=-1, keepdims=True)
            d = o - mu
            var = jnp.mean(d * d, axis=-1, keepdims=True)
            gate = g_ref[i, h]
            o_ref[i, h] = gate * jax.nn.sigmoid(gate) * (d * lax.rsqrt(var + EPS))


def _retention_step(q, k, v, g, state, dec):
    nb = q.shape[0]
    per = SAMPLE_SEQS_PER_STEP if nb % SAMPLE_SEQS_PER_STEP == 0 else 1
    col = pl.BlockSpec((per, R_HEADS, R_DK), lambda b: (b, 0, 0))
    rowv = pl.BlockSpec((per, R_HEADS, 1, R_DV), lambda b: (b, 0, 0, 0))
    st = pl.BlockSpec((per, R_HEADS, R_DK, R_DV), lambda b: (b, 0, 0, 0))
    return pl.pallas_call(
        _retention_step_kernel,
        grid=(nb // per,),
        in_specs=[col, col, rowv, rowv, st, pl.BlockSpec((R_HEADS, 1, R_DV), lambda b: (0, 0, 0))],
        out_specs=[rowv, st],
        out_shape=[jax.ShapeDtypeStruct((nb, R_HEADS, 1, R_DV), F32),
                   jax.ShapeDtypeStruct((nb, R_HEADS, R_DK, R_DV), F32)],
        compiler_params=_params(("parallel",)),
        name="retention_step",
    )(q, k, v, g, state, dec)


def _block_sum_kernel(pt_ref, *refs):
    _sum_pages(refs[:-1], refs[-1].at[0, 0])


def _block_sums(cache_k, pt_flat, nb, n_pages):
    steps = n_pages // PAGES_PER_STEP
    blk_per_step = PAGES_PER_STEP * PAGE_SIZE // MOBA_BLOCK

    def page_spec(p):
        return pl.BlockSpec((1, PAGE_SIZE, A_HEADS, A_DH),
                            lambda b, j, pt: (pt[b * n_pages + j * PAGES_PER_STEP + p], 0, 0, 0))

    grid_spec = pltpu.PrefetchScalarGridSpec(
        num_scalar_prefetch=1,
        grid=(nb, steps),
        in_specs=[page_spec(p) for p in range(PAGES_PER_STEP)],
        out_specs=pl.BlockSpec((1, 1, blk_per_step, A_HEADS, A_DH), lambda b, j, pt: (b, j, 0, 0, 0)),
    )
    out = pl.pallas_call(
        _block_sum_kernel,
        grid_spec=grid_spec,
        out_shape=jax.ShapeDtypeStruct((nb, steps, blk_per_step, A_HEADS, A_DH), F32),
        compiler_params=_params(("parallel", "parallel")),
        name="cache_block_sums",
    )(pt_flat, *([cache_k] * PAGES_PER_STEP))
    return out.reshape(nb, steps * blk_per_step, A_W)


def _block_pick_kernel(q_ref, ks_ref, o_ref):
    for i in range(q_ref.shape[0]):
        ks = ks_ref[i] * (1.0 / MOBA_BLOCK)
        prod = ks * q_ref[i]
        nblk = prod.shape[0]
        lane = lax.broadcasted_iota(jnp.int32, (nblk, LANES), 1)
        gate = jnp.full((nblk, LANES), -jnp.inf, F32)
        for h in range(A_HEADS):
            gate = jnp.where(lane == h, jnp.sum(prod[:, h * A_DH:(h + 1) * A_DH], axis=-1, keepdims=True), gate)
        blk = lax.broadcasted_iota(jnp.int32, gate.shape, 0)
        rank = jnp.zeros(gate.shape, jnp.int32)
        for m in range(nblk):
            gm = gate[m:m + 1, :]
            ahead = (gm > gate) | ((gm == gate) & (m < blk))
            rank = rank + ahead.astype(jnp.int32)
        for j in range(MOBA_TOPK):
            o_ref[i, j:j + 1, :] = jnp.sum(jnp.where(rank == j, blk, 0), axis=0, keepdims=True)


def _block_pick(q, ksum):
    nb, nblk, _ = ksum.shape
    per = SAMPLE_SEQS_PER_STEP if nb % SAMPLE_SEQS_PER_STEP == 0 else 1
    return pl.pallas_call(
        _block_pick_kernel,
        grid=(nb // per,),
        in_specs=[pl.BlockSpec((per, 1, A_W), lambda b: (b, 0, 0)),
                  pl.BlockSpec((per, nblk, A_W), lambda b: (b, 0, 0))],
        out_specs=pl.BlockSpec((per, MOBA_TOPK, LANES), lambda b: (b, 0, 0)),
        out_shape=jax.ShapeDtypeStruct((nb, MOBA_TOPK, LANES), jnp.int32),
        compiler_params=_params(("parallel",)),
        name="block_pick",
    )(q, ksum)


PICK_PAGES = MOBA_TOPK * (MOBA_BLOCK // PAGE_SIZE)


def _attend_sample_kernel(pages_per_seq, pt_ref, idx_ref, q_ref, kn_ref, vn_ref, ck_ref, cv_ref, o_ref,
                          kbuf, vbuf, ksem, vsem):
    b = pl.program_id(0)
    per_blk = MOBA_BLOCK // PAGE_SIZE

    def page_copies(seq_i, slot):
        out = []
        for h in range(A_HEADS):
            for j in range(MOBA_TOPK):
                blk = idx_ref[(seq_i * MOBA_TOPK + j) * A_HEADS + h]
                for p in range(per_blk):
                    page = pt_ref[seq_i * pages_per_seq + blk * per_blk + p]
                    dst = (slot, h, j * per_blk + p)
                    out.append(pltpu.make_async_copy(ck_ref.at[page, :, h, :], kbuf.at[dst], ksem.at[slot]))
                    out.append(pltpu.make_async_copy(cv_ref.at[page, :, h, :], vbuf.at[dst], vsem.at[slot]))
        return out

    slot = b % 2

    @pl.when(b == 0)
    def _():
        for c in page_copies(0, 0):
            c.start()

    @pl.when(b + 1 < pl.num_programs(0))
    def _():
        for c in page_copies(b + 1, 1 - slot):
            c.start()

    for c in page_copies(b, slot):
        c.wait()

    for h in range(A_HEADS):
        q = q_ref[0, h:h + 1, :]
        s_own = jnp.sum(q * kn_ref[0, h:h + 1, :], axis=-1, keepdims=True)
        scores = [jnp.sum(kbuf[slot, h, g] * q, axis=-1, keepdims=True) for g in range(PICK_PAGES)]
        m = s_own
        for s in scores:
            m = jnp.maximum(m, jnp.max(s, axis=0, keepdims=True))
        p_own = jnp.exp(s_own - m)
        l = p_own
        acc = p_own * vn_ref[0, h:h + 1, :]
        for g, s in enumerate(scores):
            p = jnp.exp(s - m)
            l = l + jnp.sum(p, axis=0, keepdims=True)
            acc = acc + jnp.sum(p * vbuf[slot, h, g], axis=0, keepdims=True)
        o_ref[0, h:h + 1, :] = acc / l


def _attend_sample(q, k_new, v_new, cache_k, cache_v, pt_flat, picks_flat, pages_per_seq):
    nb = q.shape[0]
    vec = pl.BlockSpec((1, A_HEADS, A_DH), lambda b, pt, idx: (b, 0, 0))
    hbm = pl.BlockSpec(memory_space=pl.ANY)
    buf = pltpu.VMEM((2, A_HEADS, PICK_PAGES, PAGE_SIZE, A_DH), F32)
    grid_spec = pltpu.PrefetchScalarGridSpec(
        num_scalar_prefetch=2,
        grid=(nb,),
        in_specs=[vec, vec, vec, hbm, hbm],
        out_specs=vec,
        scratch_shapes=[buf, buf, pltpu.SemaphoreType.DMA((2,)), pltpu.SemaphoreType.DMA((2,))],
    )
    return pl.pallas_call(
        functools.partial(_attend_sample_kernel, pages_per_seq),
        grid_spec=grid_spec,
        out_shape=jax.ShapeDtypeStruct((nb, A_HEADS, A_DH), F32),
        compiler_params=_params(("arbitrary",)),
        name="attend_sample",
    )(pt_flat, picks_flat, q, k_new, v_new, cache_k, cache_v)


def _rope_tables(seq):
    half = A_DH // 2
    inv = ROPE_THETA ** (-jnp.arange(half, dtype=F32) / half)
    ang = jnp.arange(seq, dtype=jnp.int32).astype(F32)[:, None] * inv[None, :]
    cos, sin = jnp.cos(ang), jnp.sin(ang)
    return jnp.concatenate([cos, cos], axis=-1), jnp.concatenate([-sin, sin], axis=-1)


def _router_weights(w_rg, b_rg, w_re, b_re):
    pad = ROUTER_LANES - N_GROUPS - N_EXPERTS
    w = jnp.concatenate([w_rg, w_re.reshape(D_MODEL, N_EXPERTS), jnp.zeros((D_MODEL, pad), F32)], axis=-1)
    b = jnp.concatenate([b_rg, b_re.reshape(N_EXPERTS), jnp.zeros((pad,), F32)])[None, :]
    return w, b


def kernel(x_prompt, x_sample, cache_k, cache_v, state_ret, page_table, norm_mix, w_in, w_ret_o, w_att_o, w_o,
           norm_ffn, w_router_group, b_router_group, w_router_expert, b_router_expert, w_gate, w_up, w_down,
           norm_final):
    nb, seq, _ = x_prompt.shape
    db, ds, _ = x_sample.shape
    depth = norm_mix.shape[0]
    past_len = page_table.shape[1] * PAGE_SIZE
    assert depth == 1 and ds == 1
    assert seq % MOBA_BLOCK == 0 and past_len % (PAGES_PER_STEP * PAGE_SIZE) == 0
    assert past_len // MOBA_BLOCK >= MOBA_TOPK
    l = 0
    t = nb * seq

    g_mix = norm_mix[l][None, :]
    g_ffn = norm_ffn[l][None, :]
    g_fin = norm_final[None, :]
    w_router, b_router = _router_weights(w_router_group[l], b_router_group[l], w_router_expert[l], b_router_expert[l])
    w_in_bf = w_in[l].astype(BF16)
    wg_bf, wu_bf, wd_bf = w_gate[l].astype(BF16), w_up[l].astype(BF16), w_down[l].astype(BF16)
    wro_bf, wao_bf, wo_bf = w_ret_o[l].astype(BF16), w_att_o[l].astype(BF16), w_o[l].astype(BF16)

    xp = x_prompt.reshape(t, D_MODEL)
    cs, sn = _rope_tables(seq)
    rq, rk, rv, rg, aq, kp, vp, gr, ga = _inproj(xp, g_mix, w_in_bf, cs, sn, seq)
    ret, ret_state = _retention(rq, rk, rv, rg, _retention_tables(), nb, seq)
    ck, cv = cache_k[l], cache_v[l]
    n_pages = page_table.shape[1]
    pt_flat = page_table.reshape(-1)
    blocks_per_seq = n_pages * PAGE_SIZE // MOBA_BLOCK
    moe_tb = MOE_BLOCK if t % MOE_BLOCK == 0 else MOE_BLOCK // 2
    moe_steps = (t // moe_tb) * N_GROUPS * EXPERT_SPLIT
    moe_pages, moe_seqs = _rider_plan(moe_steps, MOE_RIDER_PAGES, db, n_pages)
    moba_pages, moba_seqs = _rider_plan(nb * A_HEADS, MOBA_RIDER_PAGES, db - moe_seqs, n_pages)
    ksum_parts = []

    moba_out = _moba(aq, kp, vp, nb, seq, cache_k=ck, pt_flat=pt_flat[moe_seqs * n_pages:], n_page=moba_pages)
    att, ks_moba = moba_out if moba_pages else (moba_out, None)
    y1, h2, comb = _outproj(ret, att, gr, ga, xp, wro_bf, wao_bf, wo_bf, g_ffn, w_router.astype(BF16), b_router,
                            precise=False, tm=OUTPROJ_TM)
    moe_out = _moe_sorted(h2, comb, y1, wg_bf, wu_bf, wd_bf, g_fin, moe_tb,
                          cache_k=ck, pt_flat=pt_flat, n_page=moe_pages)
    y_prompt, ks_moe = moe_out if moe_pages else (moe_out, None)
    y_prompt = y_prompt.reshape(nb, seq, D_MODEL)
    if moe_pages:
        ksum_parts.append(ks_moe.reshape(moe_seqs, blocks_per_seq, A_W))
    if moba_pages:
        ksum_parts.append(ks_moba.reshape(moba_seqs, blocks_per_seq, A_W))
    done_seqs = moe_seqs + moba_seqs
    if done_seqs < db:
        ksum_parts.append(_block_sums(ck, pt_flat[done_seqs * n_pages:], db - done_seqs, n_pages))
    ksum = jnp.concatenate(ksum_parts, axis=0) if len(ksum_parts) > 1 else ksum_parts[0]

    xs = x_sample.reshape(db, D_MODEL)
    ta, tb = _sample_tables(db, past_len)
    zs = _inproj_sample(xs, g_mix, w_in[l], ta, tb)
    seg = lambda off, width: zs[:, off:off + width]
    ld = jnp.log(1.0 - 2.0 ** (-5.0 - jnp.arange(R_HEADS, dtype=F32)))
    dec = jnp.broadcast_to(jnp.exp(ld * 1.0)[:, None, None], (R_HEADS, 1, R_DV))
    ret_s, state_s = _retention_step(
        seg(OFF_RQ, R_QK_W).reshape(db, R_HEADS, R_DK), seg(OFF_RK, R_QK_W).reshape(db, R_HEADS, R_DK),
        seg(OFF_RV, R_V_W).reshape(db, R_HEADS, 1, R_DV), seg(OFF_RG, R_V_W).reshape(db, R_HEADS, 1, R_DV),
        state_ret[l].astype(F32), dec)
    aq_s, ak_s, av_s = seg(OFF_AQ, A_W), seg(OFF_AK, A_W), seg(OFF_AV, A_W)
    picks = _block_pick(aq_s.reshape(db, 1, A_W), ksum)[:, :, :A_HEADS].reshape(-1)
    heads = lambda z: z.reshape(db, A_HEADS, A_DH)
    att_s = _attend_sample(heads(aq_s), heads(ak_s), heads(av_s), ck, cv, pt_flat, picks, n_pages)
    y1_s, h2_s, comb_s = _outproj(ret_s.reshape(db, R_V_W), att_s.reshape(db, A_W), seg(OFF_GR, D_MODEL),
                                  seg(OFF_GA, D_MODEL), xs, w_ret_o[l], w_att_o[l], w_o[l], g_ffn,
                                  w_router, b_router, precise=True, tm=db)
    y_sample = _moe(h2_s, comb_s, y1_s, wg_bf, wu_bf, wd_bf, g_fin, tm=db).reshape(db, ds, D_MODEL)

    k_prompt = kp.reshape(1, nb, seq, A_HEADS, A_DH)
    v_prompt = vp.reshape(1, nb, seq, A_HEADS, A_DH)
    ret_prompt = ret_state[None]
    k_sample = ak_s.reshape(1, db, ds, A_HEADS, A_DH).astype(cache_k.dtype)
    v_sample = av_s.reshape(1, db, ds, A_HEADS, A_DH).astype(cache_v.dtype)
    ret_sample = state_s[None].astype(state_ret.dtype)
    return (y_prompt, y_sample, k_prompt, v_prompt, ret_prompt, k_sample, v_sample, ret_sample)
```

```python
import functools

import jax
import jax.numpy as jnp
from jax import lax
from jax.experimental import pallas as pl
from jax.experimental.pallas import tpu as pltpu

D_MODEL = 1024
PAGE_SIZE = 128
R_HEADS = 4
R_DK = 128
R_DV = 256
R_CHUNK = 128
A_HEADS = 8
A_DH = 128
MOBA_BLOCK = 256
MOBA_TOPK = 3
ROPE_THETA = 10000.0
N_GROUPS = 4
EXPERTS_PER_GROUP = 8
N_EXPERTS = N_GROUPS * EXPERTS_PER_GROUP
D_EXPERT = 256
EPS = 1e-6

R_QK_W = R_HEADS * R_DK
R_V_W = R_HEADS * R_DV
A_W = A_HEADS * A_DH
D_IN = 2 * R_QK_W + 2 * R_V_W + 3 * A_W + 2 * D_MODEL
OFF_RQ = 0
OFF_RK = OFF_RQ + R_QK_W
OFF_RV = OFF_RK + R_QK_W
OFF_RG = OFF_RV + R_V_W
OFF_AQ = OFF_RG + R_V_W
OFF_AK = OFF_AQ + A_W
OFF_AV = OFF_AK + A_W
OFF_GR = OFF_AV + A_W
OFF_GA = OFF_GR + D_MODEL

LANES = 128
ROUTER_LANES = LANES
EXPERT_LANE0 = N_GROUPS
GROUP_ID_LANE = 0
PAGES_PER_STEP = 8
INPROJ_TM = 256
OUTPROJ_TM = 512
MOE_BLOCK = 1024
MOE_RIDER_PAGES = 8
MOBA_RIDER_PAGES = 16
SAMPLE_SEQS_PER_STEP = 4
VMEM_LIMIT = 56 * 1024 * 1024

F32 = jnp.float32
BF16 = jnp.bfloat16
HI = lax.Precision.HIGHEST
NT = (((1,), (1,)), ((), ()))


def _params(sem, vmem=VMEM_LIMIT):
    return pltpu.CompilerParams(dimension_semantics=sem, vmem_limit_bytes=vmem)


def _rmsnorm(x, g):
    return x * lax.rsqrt(jnp.mean(x * x, axis=-1, keepdims=True) + EPS) * g


def _rope_head(z, cs, sn):
    return z * cs + pltpu.roll(z, A_DH // 2, 1) * sn


def _inproj_kernel(n_cast, x_ref, g_ref, w_ref, cs_ref, sn_ref, *refs):
    cast_in, refs = refs[:n_cast], refs[n_cast:]
    rq_ref, rk_ref, rv_ref, rg_ref, aq_ref, k_ref, v_ref, gr_ref, ga_ref = refs[:9]
    for src, dst in zip(cast_in, refs[9:]):
        dst[...] = src[...].astype(BF16)
    h = _rmsnorm(x_ref[...], g_ref[...]).astype(BF16)
    cs = cs_ref[...]
    sn = sn_ref[...]

    def z_cols(c0):
        return jnp.dot(h, w_ref[:, c0:c0 + 2 * LANES], preferred_element_type=F32)

    def plain(off, width, out_ref):
        for c in range(0, width, 2 * LANES):
            out_ref[:, c:c + 2 * LANES] = z_cols(off + c).astype(out_ref.dtype)

    def roped(off, width, out_ref, scale):
        for c in range(0, width, 2 * LANES):
            z = z_cols(off + c)
            for hh in range(2):
                r = _rope_head(z[:, hh * LANES:(hh + 1) * LANES], cs, sn)
                if scale is not None:
                    r = r * scale
                out_ref[:, c + hh * LANES:c + (hh + 1) * LANES] = r.astype(out_ref.dtype)

    roped(OFF_RQ, R_QK_W, rq_ref, R_DK ** -0.5)
    roped(OFF_RK, R_QK_W, rk_ref, None)
    plain(OFF_RV, R_V_W, rv_ref)
    plain(OFF_RG, R_V_W, rg_ref)
    roped(OFF_AQ, A_W, aq_ref, A_DH ** -0.5)
    roped(OFF_AK, A_W, k_ref, None)
    plain(OFF_AV, A_W, v_ref)
    plain(OFF_GR, D_MODEL, gr_ref)
    plain(OFF_GA, D_MODEL, ga_ref)


def _inproj(x, g, w_bf, cs, sn, seq, cast_f32=(), tm=256):
    t = x.shape[0]
    steps = t // tm
    nseq = seq // tm
    row = lambda i: (i, 0)
    tab = lambda i: (i % nseq, 0)
    outs = [(R_QK_W, BF16), (R_QK_W, BF16), (R_V_W, BF16), (R_V_W, F32), (A_W, BF16),
            (A_W, F32), (A_W, F32), (D_MODEL, F32), (D_MODEL, F32)]
    cast_specs = [pl.BlockSpec((a.shape[0] // steps, a.shape[1]), row) for a in cast_f32]
    return pl.pallas_call(
        functools.partial(_inproj_kernel, len(cast_f32)),
        grid=(steps,),
        in_specs=[pl.BlockSpec((tm, D_MODEL), row),
                  pl.BlockSpec((1, D_MODEL), lambda i: (0, 0)),
                  pl.BlockSpec((D_MODEL, D_IN), lambda i: (0, 0), pipeline_mode=pl.Buffered(1)),
                  pl.BlockSpec((tm, LANES), tab),
                  pl.BlockSpec((tm, LANES), tab)] + cast_specs,
        out_specs=[pl.BlockSpec((tm, w), row) for w, _ in outs] + cast_specs,
        out_shape=([jax.ShapeDtypeStruct((t, w), d) for w, d in outs]
                   + [jax.ShapeDtypeStruct(a.shape, BF16) for a in cast_f32]),
        compiler_params=_params(("parallel",)),
        name="inproj",
    )(x, g, w_bf, cs, sn, *cast_f32)


def _castable_in_steps(a, steps):
    return a.shape[0] % steps == 0 and (a.shape[0] // steps) % 16 == 0


def _retention_kernel(q_ref, k_ref, v_ref, g_ref, dm_ref, cd_ref, kd_ref, sd_ref, o_ref, st_ref, s_scr):
    nchunk = q_ref.shape[0] // R_CHUNK
    s_scr[...] = jnp.zeros_like(s_scr)
    for c in range(nchunk):
        rows = slice(c * R_CHUNK, (c + 1) * R_CHUNK)
        q = q_ref[rows, :]
        k = k_ref[rows, :]
        v = v_ref[rows, :]
        s = s_scr[...]
        scores = lax.dot_general(q, k, NT, preferred_element_type=F32) * dm_ref[0]
        inner = jnp.dot(scores.astype(BF16), v, preferred_element_type=F32)
        cross = jnp.dot(q, s.astype(BF16), preferred_element_type=F32) * cd_ref[0]
        k_dec_t = (k.astype(F32) * kd_ref[0]).T.astype(BF16)
        s_scr[...] = s * sd_ref[0] + jnp.dot(k_dec_t, v, preferred_element_type=F32)
        o = inner + cross
        mu = jnp.mean(o, axis=-1, keepdims=True)
        d = o - mu
        var = jnp.mean(d * d, axis=-1, keepdims=True)
        y = d * lax.rsqrt(var + EPS)
        gate = g_ref[rows, :]
        o_ref[rows, :] = (gate * jax.nn.sigmoid(gate) * y).astype(o_ref.dtype)
    st_ref[0, 0] = s_scr[...]


def _retention(rq, rk, rv, rg, tabs, nb, seq):
    dm, cd, kd, sd = tabs
    head_tab = lambda shp: pl.BlockSpec((1,) + shp, lambda b, h: (h, 0, 0))
    return pl.pallas_call(
        _retention_kernel,
        grid=(nb, R_HEADS),
        in_specs=[pl.BlockSpec((seq, R_DK), lambda b, h: (b, h)),
                  pl.BlockSpec((seq, R_DK), lambda b, h: (b, h)),
                  pl.BlockSpec((seq, R_DV), lambda b, h: (b, h)),
                  pl.BlockSpec((seq, R_DV), lambda b, h: (b, h)),
                  head_tab((R_CHUNK, R_CHUNK)), head_tab((R_CHUNK, R_DV)),
                  head_tab((R_CHUNK, R_DK)), head_tab((R_DK, R_DV))],
        out_specs=[pl.BlockSpec((seq, R_DV), lambda b, h: (b, h)),
                   pl.BlockSpec((1, 1, R_DK, R_DV), lambda b, h: (b, h, 0, 0))],
        out_shape=[jax.ShapeDtypeStruct((nb * seq, R_V_W), BF16),
                   jax.ShapeDtypeStruct((nb, R_HEADS, R_DK, R_DV), F32)],
        scratch_shapes=[pltpu.VMEM((R_DK, R_DV), F32)],
        compiler_params=_params(("parallel", "parallel")),
        name="retention",
    )(rq, rk, rv, rg, dm, cd, kd, sd)


def _retention_tables():
    ld = jnp.log(1.0 - 2.0 ** (-5.0 - jnp.arange(R_HEADS, dtype=F32)))[:, None, None]
    i = jnp.arange(R_CHUNK, dtype=F32)
    diff = i[:, None] - i[None, :]
    dmask = jnp.where(diff >= 0, jnp.exp(ld * jnp.maximum(diff, 0.0)), 0.0)
    cdec = jnp.broadcast_to(jnp.exp(ld * (i + 1.0)[None, :, None]), (R_HEADS, R_CHUNK, R_DV))
    kdec = jnp.broadcast_to(jnp.exp(ld * (R_CHUNK - 1.0 - i)[None, :, None]), (R_HEADS, R_CHUNK, R_DK))
    sdec = jnp.broadcast_to(jnp.exp(ld * R_CHUNK), (R_HEADS, R_DK, R_DV))
    return dmask, cdec, kdec, sdec


MASK_BIAS = -1e30
MOBA_SLOTS = 2


def _moba_kernel(n_page, *refs):
    if n_page:
        refs = refs[1:]
    q_ref, k_ref, v_ref = refs[:3]
    pages = refs[3:3 + n_page]
    if n_page:
        o_ref, ks_ref, kb_scr, vb_scr, km_scr, s_scr, p_scr = refs[3 + n_page:]
        _sum_pages(pages, ks_ref.at[0])
    else:
        o_ref, kb_scr, vb_scr, km_scr, s_scr, p_scr = refs[3:]
    blk_sz = MOBA_BLOCK
    nblk = k_ref.shape[0] // blk_sz
    lane = lax.broadcasted_iota(jnp.int32, (blk_sz, A_DH), 1)
    for n in range(nblk):
        rows = slice(n * blk_sz, (n + 1) * blk_sz)
        kn = k_ref[rows, :]
        kb_scr[rows, 0:A_DH] = kn.astype(BF16)
        kb_scr[rows, A_DH:2 * A_DH] = jnp.where(lane == n, 1.0, 0.0).astype(BF16)
        km_scr[n:n + 1, :] = jnp.mean(kn, axis=0, keepdims=True)
        vb_scr[rows, 0:A_DH] = v_ref[rows, :].astype(BF16)
        vb_scr[rows, A_DH:2 * A_DH] = jnp.ones((blk_sz, A_DH), BF16)
    kmean = km_scr[...].astype(BF16)

    for i in range(nblk):
        q = q_ref[i * blk_sz:(i + 1) * blk_sz, :]
        gate = lax.dot_general(kmean, q, NT, preferred_element_type=F32)
        blk = lax.broadcasted_iota(jnp.int32, gate.shape, 0)
        rank = jnp.zeros(gate.shape, jnp.int32)
        for m in range(i):
            gm = gate[m:m + 1, :]
            ahead = (gm > gate) | ((gm == gate) & (m < blk))
            rank = rank + ahead.astype(jnp.int32)
        bias_t = jnp.where((blk < i) & (rank >= MOBA_TOPK), MASK_BIAS, 0.0)
        bias_t = jnp.concatenate([bias_t, jnp.zeros((A_DH - nblk, blk_sz), F32)], axis=0)
        q_aug = jnp.concatenate([q, bias_t.T.astype(BF16)], axis=1)

        nk = (i + 1) * blk_sz
        half = blk_sz // 2
        m_run = jnp.full((blk_sz, half), -jnp.inf, F32)
        for n in range(i + 1):
            cols = slice(n * blk_sz, (n + 1) * blk_sz)
            s = lax.dot_general(q_aug, kb_scr[cols, :], NT, preferred_element_type=F32)
            if n == i:
                qpos = lax.broadcasted_iota(jnp.int32, s.shape, 0)
                kpos = lax.broadcasted_iota(jnp.int32, s.shape, 1)
                s = jnp.where(kpos <= qpos, s, -jnp.inf)
            s_scr[i % MOBA_SLOTS, :, cols] = s
            m_run = jnp.maximum(m_run, jnp.maximum(s[:, :half], s[:, half:]))
        m_row = jnp.max(m_run, axis=-1, keepdims=True)
        for n in range(i + 1):
            cols = slice(n * blk_sz, (n + 1) * blk_sz)
            p_scr[i % MOBA_SLOTS, :, cols] = jnp.exp(s_scr[i % MOBA_SLOTS, :, cols] - m_row).astype(BF16)
        o = jnp.dot(p_scr[i % MOBA_SLOTS, :, 0:nk], vb_scr[0:nk, :], preferred_element_type=F32)
        o_ref[i * blk_sz:(i + 1) * blk_sz, :] = (o[:, :A_DH] / o[:, A_DH:]).astype(o_ref.dtype)


def _moba(aq, k, v, nb, seq, cache_k=None, pt_flat=None, n_page=0):
    nblk = seq // MOBA_BLOCK
    assert nblk <= A_DH
    blk = lambda b, h, *_: (b, h)
    in_specs = [pl.BlockSpec((seq, A_DH), blk)] * 3
    out_specs = [pl.BlockSpec((seq, A_DH), blk)]
    out_shape = [jax.ShapeDtypeStruct((nb * seq, A_W), BF16)]
    args = [aq, k, v]
    if n_page:
        blk_per_step = n_page * PAGE_SIZE // MOBA_BLOCK
        for q in range(n_page):
            in_specs.append(pl.BlockSpec((1, PAGE_SIZE, A_HEADS, A_DH),
                                         lambda b, h, pt, q=q: (pt[(b * A_HEADS + h) * n_page + q], 0, 0, 0)))
        out_specs.append(pl.BlockSpec((1, blk_per_step, A_HEADS, A_DH), lambda b, h, pt: (b * A_HEADS + h, 0, 0, 0)))
        out_shape.append(jax.ShapeDtypeStruct((nb * A_HEADS, blk_per_step, A_HEADS, A_DH), F32))
        args = [pt_flat] + args + [cache_k] * n_page
    grid_spec = pltpu.PrefetchScalarGridSpec(
        num_scalar_prefetch=1 if n_page else 0,
        grid=(nb, A_HEADS),
        in_specs=in_specs,
        out_specs=out_specs,
        scratch_shapes=[pltpu.VMEM((seq, 2 * A_DH), BF16),
                        pltpu.VMEM((seq, 2 * A_DH), BF16),
                        pltpu.VMEM((nblk, A_DH), F32),
                        pltpu.VMEM((MOBA_SLOTS, MOBA_BLOCK, seq), F32),
                        pltpu.VMEM((MOBA_SLOTS, MOBA_BLOCK, seq), BF16)],
    )
    out = pl.pallas_call(
        functools.partial(_moba_kernel, n_page),
        grid_spec=grid_spec,
        out_shape=out_shape,
        compiler_params=_params(("parallel", "parallel")),
        name="moba",
    )(*args)
    return (out[0], out[1].reshape(-1, A_HEADS, A_DH)) if n_page else out[0]


def _route(logits):
    lane = lax.broadcasted_iota(jnp.int32, logits.shape, 1)
    ninf = -jnp.inf
    big = jnp.int32(ROUTER_LANES)
    glog = jnp.where(lane < N_GROUPS, logits, ninf)
    gmax = jnp.max(glog, axis=-1, keepdims=True)
    g_idx = jnp.min(jnp.where(glog == gmax, lane, big), axis=-1, keepdims=True)
    g_w = 1.0 / jnp.sum(jnp.exp(glog - gmax), axis=-1, keepdims=True)
    e_lo = EXPERT_LANE0 + g_idx * EXPERTS_PER_GROUP
    in_group = (lane >= e_lo) & (lane < e_lo + EXPERTS_PER_GROUP)
    e1 = jnp.where(in_group, logits, ninf)
    v1 = jnp.max(e1, axis=-1, keepdims=True)
    i1 = jnp.min(jnp.where(e1 == v1, lane, big), axis=-1, keepdims=True)
    e2 = jnp.where(lane == i1, ninf, e1)
    v2 = jnp.max(e2, axis=-1, keepdims=True)
    i2 = jnp.min(jnp.where(e2 == v2, lane, big), axis=-1, keepdims=True)
    x2 = jnp.exp(v2 - v1)
    den = 1.0 + x2
    comb = jnp.where(lane == i1, g_w * (1.0 / den), jnp.where(lane == i2, g_w * (x2 / den), 0.0))
    return jnp.where(lane == GROUP_ID_LANE, g_idx.astype(F32), comb)


def _outproj_kernel(precise, ret_ref, att_ref, gr_ref, ga_ref, x_ref, wr_ref, wa_ref, wo_ref,
                    gf_ref, wrt_ref, brt_ref, y1_ref, h2_ref, comb_ref):
    prec = HI if precise else None
    mm = F32 if precise else BF16
    r = jnp.dot(ret_ref[...].astype(mm), wr_ref[...], precision=prec, preferred_element_type=F32)
    a = jnp.dot(att_ref[...].astype(mm), wa_ref[...], precision=prec, preferred_element_type=F32)
    merged = jax.nn.sigmoid(gr_ref[...]) * r + jax.nn.sigmoid(ga_ref[...]) * a
    m = jnp.dot(merged.astype(mm), wo_ref[...], precision=prec, preferred_element_type=F32)
    y1 = x_ref[...] + m
    y1_ref[...] = y1
    h2 = _rmsnorm(y1, gf_ref[...])
    h2_ref[...] = h2.astype(h2_ref.dtype)
    logits = jnp.dot(h2.astype(mm), wrt_ref[...], precision=prec, preferred_element_type=F32) + brt_ref[...]
    comb_ref[...] = _route(logits)


def _outproj(ret, att, gr, ga, x, w_ret_o, w_att_o, w_o, g_ffn, w_router, b_router, precise, tm):
    t = x.shape[0]
    row = lambda i: (i, 0)
    full = lambda i: (0, 0)
    wspec = lambda: pl.BlockSpec((D_MODEL, D_MODEL), full, pipeline_mode=pl.Buffered(1))
    return pl.pallas_call(
        functools.partial(_outproj_kernel, precise),
        grid=(t // tm,),
        in_specs=[pl.BlockSpec((tm, D_MODEL), row)] * 5 + [wspec(), wspec(), wspec(),
                  pl.BlockSpec((1, D_MODEL), full),
                  pl.BlockSpec(w_router.shape, full),
                  pl.BlockSpec((1, ROUTER_LANES), full)],
        out_specs=[pl.BlockSpec((tm, D_MODEL), row), pl.BlockSpec((tm, D_MODEL), row),
                   pl.BlockSpec((tm, ROUTER_LANES), row)],
        out_shape=[jax.ShapeDtypeStruct((t, D_MODEL), F32), jax.ShapeDtypeStruct((t, D_MODEL), BF16),
                   jax.ShapeDtypeStruct((t, ROUTER_LANES), F32)],
        compiler_params=_params(("parallel",)),
        name="outproj_precise" if precise else "outproj",
    )(ret, att, gr, ga, x, w_ret_o, w_att_o, w_o, g_ffn, w_router, b_router)


def _sum_pages(pages, out_block):
    per_blk = MOBA_BLOCK // PAGE_SIZE
    for n in range(len(pages) // per_blk):
        acc = jnp.sum(pages[n * per_blk][0], axis=0)
        for p in range(1, per_blk):
            acc = acc + jnp.sum(pages[n * per_blk + p][0], axis=0)
        out_block[n] = acc


def _moe_kernel(epb, h_ref, comb_ref, y1_ref, wg_ref, wu_ref, wd_ref, gf_ref, o_ref, acc_ref):
    e = pl.program_id(1)

    @pl.when(e == 0)
    def _():
        acc_ref[...] = jnp.zeros_like(acc_ref)

    h = h_ref[...]
    comb = comb_ref[...]
    lane = lax.broadcasted_iota(jnp.int32, comb.shape, 1)
    for j in range(epb):
        c = jnp.sum(jnp.where(lane == EXPERT_LANE0 + e * epb + j, comb, 0.0), axis=-1, keepdims=True)
        a = jnp.dot(h, wg_ref[j], preferred_element_type=F32)
        u = jnp.dot(h, wu_ref[j], preferred_element_type=F32)
        act = (a * jax.nn.sigmoid(a)) * u * c
        acc_ref[...] += jnp.dot(act.astype(BF16), wd_ref[j], preferred_element_type=F32)

    @pl.when(e == pl.num_programs(1) - 1)
    def _():
        o_ref[...] = _rmsnorm(y1_ref[...] + acc_ref[...], gf_ref[...])


def _moe(h2, comb, y1, wg, wu, wd, g_final, tm, epb=4):
    t = h2.shape[0]
    row = lambda i, e: (i, 0)
    wblk = lambda i, e: (e, 0, 0)
    return pl.pallas_call(
        functools.partial(_moe_kernel, epb),
        grid=(t // tm, N_EXPERTS // epb),
        in_specs=[pl.BlockSpec((tm, D_MODEL), row), pl.BlockSpec((tm, ROUTER_LANES), row),
                  pl.BlockSpec((tm, D_MODEL), row),
                  pl.BlockSpec((epb, D_MODEL, D_EXPERT), wblk),
                  pl.BlockSpec((epb, D_MODEL, D_EXPERT), wblk),
                  pl.BlockSpec((epb, D_EXPERT, D_MODEL), wblk),
                  pl.BlockSpec((1, D_MODEL), lambda i, e: (0, 0))],
        out_specs=pl.BlockSpec((tm, D_MODEL), row),
        out_shape=jax.ShapeDtypeStruct((t, D_MODEL), F32),
        scratch_shapes=[pltpu.VMEM((tm, D_MODEL), F32)],
        compiler_params=_params(("parallel", "arbitrary")),
        name="moe",
    )(h2, comb, y1, wg, wu, wd, g_final)


SORT_TILE = 128
EXPERT_SPLIT = 2
COMB_LANES = 40
assert EXPERT_LANE0 + N_EXPERTS <= COMB_LANES and 3 * COMB_LANES <= LANES


def _moe_sorted_kernel(n_page, *refs):
    if n_page:
        refs = refs[1:]
    h_ref, comb_ref, y1_ref, wg_ref, wu_ref, wd_ref, gf_ref, tri_ref = refs[:8]
    pages = refs[8:8 + n_page]
    rest = refs[8 + n_page:]
    if n_page:
        o_ref, ks_ref = rest[:2]
        _sum_pages(pages, ks_ref.at[0])
        rest = rest[2:]
    else:
        o_ref, rest = rest[0], rest[1:]
    xs_scr, cs_scr, out_scr, dest_scr, run_smem = rest
    g = pl.program_id(1)
    part = pl.program_id(2)
    tb = h_ref.shape[0]
    cap = xs_scr.shape[0]
    epb = wg_ref.shape[0]

    @pl.when((g == 0) & (part == 0))
    def _sort_block():
        comb = comb_ref[...]
        lane = lax.broadcasted_iota(jnp.int32, comb.shape, 1)
        gcol = comb[:, GROUP_ID_LANE:GROUP_ID_LANE + 1]
        ind = jnp.where((lane < N_GROUPS) & (lane.astype(F32) == gcol), 1.0, 0.0)
        pos = jnp.dot(tri_ref[...], ind.astype(BF16), preferred_element_type=F32)
        dest = jnp.zeros((tb, 1), F32)
        start = jnp.int32(0)
        for gg in range(N_GROUPS):
            col = ind[:, gg:gg + 1]
            n_tiles = (jnp.sum(col).astype(jnp.int32) + SORT_TILE - 1) // SORT_TILE
            run_smem[gg] = start
            run_smem[N_GROUPS + gg] = n_tiles
            dest = dest + col * (pos[:, gg:gg + 1] + start.astype(F32))
            start = start + n_tiles * SORT_TILE
        dest_b = jnp.broadcast_to(dest, (tb, LANES))
        dest_scr[...] = dest_b
        dest_row = dest_b.T[0:1, :].astype(jnp.int32)
        c_hi = comb.astype(BF16).astype(F32)
        r1 = comb - c_hi
        c_mid = r1.astype(BF16).astype(F32)
        c_lo = r1 - c_mid
        packed = (c_hi + pltpu.roll(c_mid, COMB_LANES, 1) + pltpu.roll(c_lo, 2 * COMB_LANES, 1)).astype(BF16)
        lane_t = lax.broadcasted_iota(jnp.int32, (SORT_TILE, ROUTER_LANES), 1)
        h = h_ref[...]
        for c in range(cap // SORT_TILE):
            rows = slice(c * SORT_TILE, (c + 1) * SORT_TILE)

            def sort_rows(c=c, rows=rows):
                riota = lax.broadcasted_iota(jnp.int32, (SORT_TILE, tb), 0) + c * SORT_TILE
                perm = jnp.where(riota == dest_row, 1.0, 0.0).astype(BF16)
                xs_scr[rows, :] = jnp.dot(perm, h, preferred_element_type=F32).astype(BF16)
                pk = jnp.dot(perm, packed, preferred_element_type=F32)
                terms = (pk + pltpu.roll(pk, ROUTER_LANES - COMB_LANES, 1)
                         + pltpu.roll(pk, ROUTER_LANES - 2 * COMB_LANES, 1))
                cs_scr[rows, :] = jnp.where(lane_t < COMB_LANES, terms, 0.0)

            if c * SORT_TILE < tb:
                sort_rows()
            else:
                pl.when(c * SORT_TILE < start)(sort_rows)
        out_scr[...] = jnp.zeros_like(out_scr)

    start = run_smem[g]
    n_tiles = run_smem[N_GROUPS + g]
    lane_c = lax.broadcasted_iota(jnp.int32, (SORT_TILE, ROUTER_LANES), 1)
    lane0 = EXPERT_LANE0 + g * EXPERTS_PER_GROUP + part * epb

    def run_tile(t, carry):
        r0 = pl.multiple_of(start + t * SORT_TILE, SORT_TILE)
        x = xs_scr[pl.ds(r0, SORT_TILE), :]
        cs = cs_scr[pl.ds(r0, SORT_TILE), :]
        acc = jnp.zeros((SORT_TILE, D_MODEL), F32)
        for j in range(epb):
            c = jnp.sum(jnp.where(lane_c == lane0 + j, cs, 0.0), axis=-1, keepdims=True)
            a = jnp.dot(x, wg_ref[j], preferred_element_type=F32)
            u = jnp.dot(x, wu_ref[j], preferred_element_type=F32)
            act = (a * jax.nn.sigmoid(a)) * u * c
            acc = acc + jnp.dot(act.astype(BF16), wd_ref[j], preferred_element_type=F32)
        out_scr[pl.ds(r0, SORT_TILE), :] += acc
        return carry

    lax.fori_loop(0, n_tiles, run_tile, 0)

    @pl.when((g == pl.num_programs(1) - 1) & (part == pl.num_programs(2) - 1))
    def _unsort_block():
        chunk = 2 * SORT_TILE
        lane_r = lax.broadcasted_iota(jnp.int32, (chunk, cap), 1)

        def perm_t(rows):
            return jnp.where(lane_r == dest_scr[rows, 0:1].astype(jnp.int32), 1.0, 0.0).astype(BF16)

        xs_scr[...] = out_scr[...].astype(BF16)
        for c in range(tb // chunk):
            rows = slice(c * chunk, (c + 1) * chunk)
            o_ref[rows, :] = jnp.dot(perm_t(rows), xs_scr[...], preferred_element_type=F32)
        xs_scr[...] = (out_scr[...] - xs_scr[...].astype(F32)).astype(BF16)
        for c in range(tb // chunk):
            rows = slice(c * chunk, (c + 1) * chunk)
            moe = o_ref[rows, :] + jnp.dot(perm_t(rows), xs_scr[...], preferred_element_type=F32)
            o_ref[rows, :] = _rmsnorm(y1_ref[rows, :] + moe, gf_ref[...])


def _rider_plan(steps, max_pages, n_seq, pages_per_seq):
    per_blk = MOBA_BLOCK // PAGE_SIZE
    n_page = min(max_pages, n_seq * pages_per_seq // steps) // per_blk * per_blk
    if n_page == 0 or pages_per_seq % n_page or (steps * n_page) % pages_per_seq:
        return 0, 0
    return n_page, steps * n_page // pages_per_seq


def _moe_sorted(h2, comb, y1, wg, wu, wd, g_final, tb, cache_k=None, pt_flat=None, n_page=0):
    t = h2.shape[0]
    epb = EXPERTS_PER_GROUP // EXPERT_SPLIT
    cap = tb + N_GROUPS * SORT_TILE
    grid = (t // tb, N_GROUPS, EXPERT_SPLIT)
    row = lambda b, g, p, *_: (b, 0)
    wblk = lambda b, g, p, *_: (g * EXPERT_SPLIT + p, 0, 0)
    const = lambda b, g, p, *_: (0, 0)
    tri = jnp.tril(jnp.ones((tb, tb), BF16), -1)
    in_specs = [pl.BlockSpec((tb, D_MODEL), row, pipeline_mode=pl.Buffered(1)),
                pl.BlockSpec((tb, ROUTER_LANES), row, pipeline_mode=pl.Buffered(1)),
                pl.BlockSpec((tb, D_MODEL), row, pipeline_mode=pl.Buffered(1)),
                pl.BlockSpec((epb, D_MODEL, D_EXPERT), wblk),
                pl.BlockSpec((epb, D_MODEL, D_EXPERT), wblk),
                pl.BlockSpec((epb, D_EXPERT, D_MODEL), wblk),
                pl.BlockSpec((1, D_MODEL), const),
                pl.BlockSpec((tb, tb), const, pipeline_mode=pl.Buffered(1))]
    out_specs = [pl.BlockSpec((tb, D_MODEL), row)]
    out_shape = [jax.ShapeDtypeStruct((t, D_MODEL), F32)]
    args = [h2, comb, y1, wg, wu, wd, g_final, tri]
    if n_page:
        blk_per_step = n_page * PAGE_SIZE // MOBA_BLOCK
        steps = grid[0] * grid[1] * grid[2]
        step = lambda b, g, p: (b * N_GROUPS + g) * EXPERT_SPLIT + p
        for q in range(n_page):
            in_specs.append(pl.BlockSpec((1, PAGE_SIZE, A_HEADS, A_DH),
                                         lambda b, g, p, pt, q=q: (pt[step(b, g, p) * n_page + q], 0, 0, 0)))
        out_specs.append(pl.BlockSpec((1, blk_per_step, A_HEADS, A_DH), lambda b, g, p, pt: (step(b, g, p), 0, 0, 0)))
        out_shape.append(jax.ShapeDtypeStruct((steps, blk_per_step, A_HEADS, A_DH), F32))
        args = [pt_flat] + args + [cache_k] * n_page
    grid_spec = pltpu.PrefetchScalarGridSpec(
        num_scalar_prefetch=1 if n_page else 0,
        grid=grid,
        in_specs=in_specs,
        out_specs=out_specs,
        scratch_shapes=[pltpu.VMEM((cap, D_MODEL), BF16),
                        pltpu.VMEM((cap, ROUTER_LANES), F32),
                        pltpu.VMEM((cap, D_MODEL), F32),
                        pltpu.VMEM((tb, LANES), F32),
                        pltpu.SMEM((2 * N_GROUPS,), jnp.int32)],
    )
    out = pl.pallas_call(
        functools.partial(_moe_sorted_kernel, n_page),
        grid_spec=grid_spec,
        out_shape=out_shape,
        compiler_params=_params(("parallel", "arbitrary", "arbitrary")),
        name="moe_sorted",
    )(*args)
    return (out[0], out[1].reshape(-1, A_HEADS, A_DH)) if n_page else out[0]


def _inproj_sample_kernel(x_ref, g_ref, w_ref, ta_ref, tb_ref, o_ref, wbf_ref):
    w = w_ref[...]
    wbf_ref[...] = w.astype(BF16)
    h = _rmsnorm(x_ref[...], g_ref[...])
    z = jnp.dot(h, w, precision=HI, preferred_element_type=F32)
    for c in range(0, z.shape[1], LANES):
        zh = z[:, c:c + LANES]
        o_ref[:, c:c + LANES] = zh * ta_ref[:, c:c + LANES] + pltpu.roll(zh, A_DH // 2, 1) * tb_ref[:, c:c + LANES]


def _inproj_sample(x, g, w_in, ta, tb, tn=512):
    rows = x.shape[0]
    return pl.pallas_call(
        _inproj_sample_kernel,
        grid=(D_IN // tn,),
        in_specs=[pl.BlockSpec((rows, D_MODEL), lambda j: (0, 0)),
                  pl.BlockSpec((1, D_MODEL), lambda j: (0, 0)),
                  pl.BlockSpec((D_MODEL, tn), lambda j: (0, j)),
                  pl.BlockSpec((rows, tn), lambda j: (0, j)),
                  pl.BlockSpec((rows, tn), lambda j: (0, j))],
        out_specs=[pl.BlockSpec((rows, tn), lambda j: (0, j)), pl.BlockSpec((D_MODEL, tn), lambda j: (0, j))],
        out_shape=[jax.ShapeDtypeStruct((rows, D_IN), F32), jax.ShapeDtypeStruct((D_MODEL, D_IN), BF16)],
        compiler_params=_params(("parallel",)),
        name="inproj_sample",
    )(x, g, w_in, ta, tb)


def _sample_tables(rows, pos):
    half = A_DH // 2
    inv = ROPE_THETA ** (-jnp.arange(half, dtype=F32) / half)
    ang = jnp.full((rows, 1), pos, F32) * inv[None, :]
    cs = jnp.concatenate([jnp.cos(ang), jnp.cos(ang)], axis=-1)
    sn = jnp.concatenate([-jnp.sin(ang), jnp.sin(ang)], axis=-1)
    ones = jnp.ones((rows, LANES), F32)
    zeros = jnp.zeros((rows, LANES), F32)
    ta, tb = [], []
    for off, width, kind in ((OFF_RQ, R_QK_W, R_DK ** -0.5), (OFF_RK, R_QK_W, 1.0), (OFF_RV, 2 * R_V_W, None),
                             (OFF_AQ, A_W, A_DH ** -0.5), (OFF_AK, A_W, 1.0), (OFF_AV, A_W + 2 * D_MODEL, None)):
        for _ in range(width // LANES):
            ta.append(ones if kind is None else cs * kind)
            tb.append(zeros if kind is None else sn * kind)
    return jnp.concatenate(ta, axis=-1), jnp.concatenate(tb, axis=-1)


def _retention_step_kernel(q_ref, k_ref, v_ref, g_ref, s_ref, dec_ref, o_ref, sn_ref):
    for i in range(q_ref.shape[0]):
        for h in range(R_HEADS):
            col = lambda r: jnp.broadcast_to(r[i, h:h + 1, :], (R_DK, R_DK)).T[:, 0:1]
            q = col(q_ref)
            k = col(k_ref)
            v = v_ref[i, h]
            s = s_ref[i, h]
            dec = dec_ref[h]
            cross = jnp.sum(q * s, axis=0, keepdims=True) * dec
            inner = jnp.sum(q * k, axis=0, keepdims=True) * v
            sn_ref[i, h] = s * dec + k * v
            o = inner + cross
            mu = jnp.mean(o, axis=-1, keepdims=True)
            d = o - mu
            var = jnp.mean(d * d, axis=-1, keepdims=True)
            gate = g_ref[i, h]
            o_ref[i, h] = gate * jax.nn.sigmoid(gate) * (d * lax.rsqrt(var + EPS))


def _retention_step(q, k, v, g, state, dec):
    nb = q.shape[0]
    per = SAMPLE_SEQS_PER_STEP if nb % SAMPLE_SEQS_PER_STEP == 0 else 1
    col = pl.BlockSpec((per, R_HEADS, R_DK), lambda b: (b, 0, 0))
    rowv = pl.BlockSpec((per, R_HEADS, 1, R_DV), lambda b: (b, 0, 0, 0))
    st = pl.BlockSpec((per, R_HEADS, R_DK, R_DV), lambda b: (b, 0, 0, 0))
    return pl.pallas_call(
        _retention_step_kernel,
        grid=(nb // per,),
        in_specs=[col, col, rowv, rowv, st, pl.BlockSpec((R_HEADS, 1, R_DV), lambda b: (0, 0, 0))],
        out_specs=[rowv, st],
        out_shape=[jax.ShapeDtypeStruct((nb, R_HEADS, 1, R_DV), F32),
                   jax.ShapeDtypeStruct((nb, R_HEADS, R_DK, R_DV), F32)],
        compiler_params=_params(("parallel",)),
        name="retention_step",
    )(q, k, v, g, state, dec)


def _block_sum_kernel(pt_ref, *refs):
    _sum_pages(refs[:-1], refs[-1].at[0, 0])


def _block_sums(cache_k, pt_flat, nb, n_pages):
    steps = n_pages // PAGES_PER_STEP
    blk_per_step = PAGES_PER_STEP * PAGE_SIZE // MOBA_BLOCK

    def page_spec(p):
        return pl.BlockSpec((1, PAGE_SIZE, A_HEADS, A_DH),
                            lambda b, j, pt: (pt[b * n_pages + j * PAGES_PER_STEP + p], 0, 0, 0))

    grid_spec = pltpu.PrefetchScalarGridSpec(
        num_scalar_prefetch=1,
        grid=(nb, steps),
        in_specs=[page_spec(p) for p in range(PAGES_PER_STEP)],
        out_specs=pl.BlockSpec((1, 1, blk_per_step, A_HEADS, A_DH), lambda b, j, pt: (b, j, 0, 0, 0)),
    )
    out = pl.pallas_call(
        _block_sum_kernel,
        grid_spec=grid_spec,
        out_shape=jax.ShapeDtypeStruct((nb, steps, blk_per_step, A_HEADS, A_DH), F32),
        compiler_params=_params(("parallel", "parallel")),
        name="cache_block_sums",
    )(pt_flat, *([cache_k] * PAGES_PER_STEP))
    return out.reshape(nb, steps * blk_per_step, A_W)


def _block_pick_kernel(q_ref, ks_ref, o_ref):
    for i in range(q_ref.shape[0]):
        ks = ks_ref[i] * (1.0 / MOBA_BLOCK)
        prod = ks * q_ref[i]
        nblk = prod.shape[0]
        lane = lax.broadcasted_iota(jnp.int32, (nblk, LANES), 1)
        gate = jnp.full((nblk, LANES), -jnp.inf, F32)
        for h in range(A_HEADS):
            gate = jnp.where(lane == h, jnp.sum(prod[:, h * A_DH:(h + 1) * A_DH], axis=-1, keepdims=True), gate)
        blk = lax.broadcasted_iota(jnp.int32, gate.shape, 0)
        rank = jnp.zeros(gate.shape, jnp.int32)
        for m in range(nblk):
            gm = gate[m:m + 1, :]
            ahead = (gm > gate) | ((gm == gate) & (m < blk))
            rank = rank + ahead.astype(jnp.int32)
        for j in range(MOBA_TOPK):
            o_ref[i, j:j + 1, :] = jnp.sum(jnp.where(rank == j, blk, 0), axis=0, keepdims=True)


def _block_pick(q, ksum):
    nb, nblk, _ = ksum.shape
    per = SAMPLE_SEQS_PER_STEP if nb % SAMPLE_SEQS_PER_STEP == 0 else 1
    return pl.pallas_call(
        _block_pick_kernel,
        grid=(nb // per,),
        in_specs=[pl.BlockSpec((per, 1, A_W), lambda b: (b, 0, 0)),
                  pl.BlockSpec((per, nblk, A_W), lambda b: (b, 0, 0))],
        out_specs=pl.BlockSpec((per, MOBA_TOPK, LANES), lambda b: (b, 0, 0)),
        out_shape=jax.ShapeDtypeStruct((nb, MOBA_TOPK, LANES), jnp.int32),
        compiler_params=_params(("parallel",)),
        name="block_pick",
    )(q, ksum)


PICK_PAGES = MOBA_TOPK * (MOBA_BLOCK // PAGE_SIZE)


def _attend_sample_kernel(pages_per_seq, pt_ref, idx_ref, q_ref, kn_ref, vn_ref, ck_ref, cv_ref, o_ref,
                          kbuf, vbuf, ksem, vsem):
    b = pl.program_id(0)
    per_blk = MOBA_BLOCK // PAGE_SIZE

    def page_copies(seq_i, slot):
        out = []
        for h in range(A_HEADS):
            for j in range(MOBA_TOPK):
                blk = idx_ref[(seq_i * MOBA_TOPK + j) * A_HEADS + h]
                for p in range(per_blk):
                    page = pt_ref[seq_i * pages_per_seq + blk * per_blk + p]
                    dst = (slot, h, j * per_blk + p)
                    out.append(pltpu.make_async_copy(ck_ref.at[page, :, h, :], kbuf.at[dst], ksem.at[slot]))
                    out.append(pltpu.make_async_copy(cv_ref.at[page, :, h, :], vbuf.at[dst], vsem.at[slot]))
        return out

    slot = b % 2

    @pl.when(b == 0)
    def _():
        for c in page_copies(0, 0):
            c.start()

    @pl.when(b + 1 < pl.num_programs(0))
    def _():
        for c in page_copies(b + 1, 1 - slot):
            c.start()

    for c in page_copies(b, slot):
        c.wait()

    for h in range(A_HEADS):
        q = q_ref[0, h:h + 1, :]
        s_own = jnp.sum(q * kn_ref[0, h:h + 1, :], axis=-1, keepdims=True)
        scores = [jnp.sum(kbuf[slot, h, g] * q, axis=-1, keepdims=True) for g in range(PICK_PAGES)]
        m = s_own
        for s in scores:
            m = jnp.maximum(m, jnp.max(s, axis=0, keepdims=True))
        p_own = jnp.exp(s_own - m)
        l = p_own
        acc = p_own * vn_ref[0, h:h + 1, :]
        for g, s in enumerate(scores):
            p = jnp.exp(s - m)
            l = l + jnp.sum(p, axis=0, keepdims=True)
            acc = acc + jnp.sum(p * vbuf[slot, h, g], axis=0, keepdims=True)
        o_ref[0, h:h + 1, :] = acc / l


def _attend_sample(q, k_new, v_new, cache_k, cache_v, pt_flat, picks_flat, pages_per_seq):
    nb = q.shape[0]
    vec = pl.BlockSpec((1, A_HEADS, A_DH), lambda b, pt, idx: (b, 0, 0))
    hbm = pl.BlockSpec(memory_space=pl.ANY)
    buf = pltpu.VMEM((2, A_HEADS, PICK_PAGES, PAGE_SIZE, A_DH), F32)
    grid_spec = pltpu.PrefetchScalarGridSpec(
        num_scalar_prefetch=2,
        grid=(nb,),
        in_specs=[vec, vec, vec, hbm, hbm],
        out_specs=vec,
        scratch_shapes=[buf, buf, pltpu.SemaphoreType.DMA((2,)), pltpu.SemaphoreType.DMA((2,))],
    )
    return pl.pallas_call(
        functools.partial(_attend_sample_kernel, pages_per_seq),
        grid_spec=grid_spec,
        out_shape=jax.ShapeDtypeStruct((nb, A_HEADS, A_DH), F32),
        compiler_params=_params(("arbitrary",)),
        name="attend_sample",
    )(pt_flat, picks_flat, q, k_new, v_new, cache_k, cache_v)


def _rope_tables(seq):
    half = A_DH // 2
    inv = ROPE_THETA ** (-jnp.arange(half, dtype=F32) / half)
    ang = jnp.arange(seq, dtype=jnp.int32).astype(F32)[:, None] * inv[None, :]
    cos, sin = jnp.cos(ang), jnp.sin(ang)
    return jnp.concatenate([cos, cos], axis=-1), jnp.concatenate([-sin, sin], axis=-1)


def _router_weights(w_rg, b_rg, w_re, b_re):
    pad = ROUTER_LANES - N_GROUPS - N_EXPERTS
    w = jnp.concatenate([w_rg, w_re.reshape(D_MODEL, N_EXPERTS), jnp.zeros((D_MODEL, pad), F32)], axis=-1)
    b = jnp.concatenate([b_rg, b_re.reshape(N_EXPERTS), jnp.zeros((pad,), F32)])[None, :]
    return w, b


def kernel(x_prompt, x_sample, cache_k, cache_v, state_ret, page_table, norm_mix, w_in, w_ret_o, w_att_o, w_o,
           norm_ffn, w_router_group, b_router_group, w_router_expert, b_router_expert, w_gate, w_up, w_down,
           norm_final):
    nb, seq, _ = x_prompt.shape
    db, ds, _ = x_sample.shape
    depth = norm_mix.shape[0]
    past_len = page_table.shape[1] * PAGE_SIZE
    assert depth == 1 and ds == 1
    assert seq % MOBA_BLOCK == 0 and past_len % (PAGES_PER_STEP * PAGE_SIZE) == 0
    assert past_len // MOBA_BLOCK >= MOBA_TOPK
    l = 0
    t = nb * seq

    g_mix = norm_mix[l][None, :]
    g_ffn = norm_ffn[l][None, :]
    g_fin = norm_final[None, :]
    w_router, b_router = _router_weights(w_router_group[l], b_router_group[l], w_router_expert[l], b_router_expert[l])
    xs = x_sample.reshape(db, D_MODEL)
    ta, tb = _sample_tables(db, past_len)
    zs, w_in_bf = _inproj_sample(xs, g_mix, w_in[l], ta, tb)
    wro_bf, wao_bf, wo_bf = w_ret_o[l].astype(BF16), w_att_o[l].astype(BF16), w_o[l].astype(BF16)

    xp = x_prompt.reshape(t, D_MODEL)
    cs, sn = _rope_tables(seq)
    moe_w = [w_gate[l].reshape(-1, D_EXPERT), w_up[l].reshape(-1, D_EXPERT), w_down[l].reshape(-1, D_MODEL)]
    ride_casts = all(_castable_in_steps(a, t // INPROJ_TM) for a in moe_w)
    proj = _inproj(xp, g_mix, w_in_bf, cs, sn, seq, cast_f32=moe_w if ride_casts else (), tm=INPROJ_TM)
    rq, rk, rv, rg, aq, kp, vp, gr, ga = proj[:9]
    moe_w_bf = proj[9:] if ride_casts else [a.astype(BF16) for a in moe_w]
    wg_bf, wu_bf, wd_bf = (a.reshape(s.shape[1:]) for a, s in zip(moe_w_bf, (w_gate, w_up, w_down)))
    ret, ret_state = _retention(rq, rk, rv, rg, _retention_tables(), nb, seq)
    ck, cv = cache_k[l], cache_v[l]
    n_pages = page_table.shape[1]
    pt_flat = page_table.reshape(-1)
    blocks_per_seq = n_pages * PAGE_SIZE // MOBA_BLOCK
    moe_tb = MOE_BLOCK if t % MOE_BLOCK == 0 else MOE_BLOCK // 2
    moe_steps = (t // moe_tb) * N_GROUPS * EXPERT_SPLIT
    moe_pages, moe_seqs = _rider_plan(moe_steps, MOE_RIDER_PAGES, db, n_pages)
    moba_pages, moba_seqs = _rider_plan(nb * A_HEADS, MOBA_RIDER_PAGES, db - moe_seqs, n_pages)
    ksum_parts = []

    moba_out = _moba(aq, kp, vp, nb, seq, cache_k=ck, pt_flat=pt_flat[moe_seqs * n_pages:], n_page=moba_pages)
    att, ks_moba = moba_out if moba_pages else (moba_out, None)
    y1, h2, comb = _outproj(ret, att, gr, ga, xp, wro_bf, wao_bf, wo_bf, g_ffn, w_router.astype(BF16), b_router,
                            precise=False, tm=OUTPROJ_TM)
    moe_out = _moe_sorted(h2, comb, y1, wg_bf, wu_bf, wd_bf, g_fin, moe_tb,
                          cache_k=ck, pt_flat=pt_flat, n_page=moe_pages)
    y_prompt, ks_moe = moe_out if moe_pages else (moe_out, None)
    y_prompt = y_prompt.reshape(nb, seq, D_MODEL)
    if moe_pages:
        ksum_parts.append(ks_moe.reshape(moe_seqs, blocks_per_seq, A_W))
    if moba_pages:
        ksum_parts.append(ks_moba.reshape(moba_seqs, blocks_per_seq, A_W))
    done_seqs = moe_seqs + moba_seqs
    if done_seqs < db:
        ksum_parts.append(_block_sums(ck, pt_flat[done_seqs * n_pages:], db - done_seqs, n_pages))
    ksum = jnp.concatenate(ksum_parts, axis=0) if len(ksum_parts) > 1 else ksum_parts[0]

    seg = lambda off, width: zs[:, off:off + width]
    ld = jnp.log(1.0 - 2.0 ** (-5.0 - jnp.arange(R_HEADS, dtype=F32)))
    dec = jnp.broadcast_to(jnp.exp(ld * 1.0)[:, None, None], (R_HEADS, 1, R_DV))
    ret_s, state_s = _retention_step(
        seg(OFF_RQ, R_QK_W).reshape(db, R_HEADS, R_DK), seg(OFF_RK, R_QK_W).reshape(db, R_HEADS, R_DK),
        seg(OFF_RV, R_V_W).reshape(db, R_HEADS, 1, R_DV), seg(OFF_RG, R_V_W).reshape(db, R_HEADS, 1, R_DV),
        state_ret[l].astype(F32), dec)
    aq_s, ak_s, av_s = seg(OFF_AQ, A_W), seg(OFF_AK, A_W), seg(OFF_AV, A_W)
    picks = _block_pick(aq_s.reshape(db, 1, A_W), ksum)[:, :, :A_HEADS].reshape(-1)
    heads = lambda z: z.reshape(db, A_HEADS, A_DH)
    att_s = _attend_sample(heads(aq_s), heads(ak_s), heads(av_s), ck, cv, pt_flat, picks, n_pages)
    y1_s, h2_s, comb_s = _outproj(ret_s.reshape(db, R_V_W), att_s.reshape(db, A_W), seg(OFF_GR, D_MODEL),
                                  seg(OFF_GA, D_MODEL), xs, w_ret_o[l], w_att_o[l], w_o[l], g_ffn,
                                  w_router, b_router, precise=True, tm=db)
    y_sample = _moe(h2_s, comb_s, y1_s, wg_bf, wu_bf, wd_bf, g_fin, tm=db).reshape(db, ds, D_MODEL)

    k_prompt = kp.reshape(1, nb, seq, A_HEADS, A_DH)
    v_prompt = vp.reshape(1, nb, seq, A_HEADS, A_DH)
    ret_prompt = ret_state[None]
    k_sample = ak_s.reshape(1, db, ds, A_HEADS, A_DH).astype(cache_k.dtype)
    v_sample = av_s.reshape(1, db, ds, A_HEADS, A_DH).astype(cache_v.dtype)
    ret_sample = state_s[None].astype(state_ret.dtype)
    return (y_prompt, y_sample, k_prompt, v_prompt, ret_prompt, k_sample, v_sample, ret_sample)
```

```python
import functools

import jax
import jax.numpy as jnp
from jax import lax
from jax.experimental import pallas as pl
from jax.experimental.pallas import tpu as pltpu

D_MODEL = 1024
PAGE_SIZE = 128
R_HEADS = 4
R_DK = 128
R_DV = 256
R_CHUNK = 128
A_HEADS = 8
A_DH = 128
MOBA_BLOCK = 256
MOBA_TOPK = 3
ROPE_THETA = 10000.0
N_GROUPS = 4
EXPERTS_PER_GROUP = 8
N_EXPERTS = N_GROUPS * EXPERTS_PER_GROUP
D_EXPERT = 256
EPS = 1e-6

R_QK_W = R_HEADS * R_DK
R_V_W = R_HEADS * R_DV
A_W = A_HEADS * A_DH
D_IN = 2 * R_QK_W + 2 * R_V_W + 3 * A_W + 2 * D_MODEL
OFF_RQ = 0
OFF_RK = OFF_RQ + R_QK_W
OFF_RV = OFF_RK + R_QK_W
OFF_RG = OFF_RV + R_V_W
OFF_AQ = OFF_RG + R_V_W
OFF_AK = OFF_AQ + A_W
OFF_AV = OFF_AK + A_W
OFF_GR = OFF_AV + A_W
OFF_GA = OFF_GR + D_MODEL

LANES = 128
ROUTER_LANES = LANES
EXPERT_LANE0 = N_GROUPS
GROUP_ID_LANE = 0
PAGES_PER_STEP = 8
INPROJ_TM = 256
OUTPROJ_TM = 512
MOE_BLOCK = 1024
MOE_RIDER_PAGES = 8
MOBA_RIDER_PAGES = 16
SAMPLE_SEQS_PER_STEP = 4
VMEM_LIMIT = 56 * 1024 * 1024

F32 = jnp.float32
BF16 = jnp.bfloat16
HI = lax.Precision.HIGHEST
NT = (((1,), (1,)), ((), ()))


def _params(sem, vmem=VMEM_LIMIT):
    return pltpu.CompilerParams(dimension_semantics=sem, vmem_limit_bytes=vmem)


def _rmsnorm(x, g):
    return x * lax.rsqrt(jnp.mean(x * x, axis=-1, keepdims=True) + EPS) * g


def _rope_head(z, cs, sn):
    return z * cs + pltpu.roll(z, A_DH // 2, 1) * sn


def _inproj_kernel(n_cast, x_ref, g_ref, w_ref, cs_ref, sn_ref, *refs):
    cast_in, refs = refs[:n_cast], refs[n_cast:]
    rq_ref, rk_ref, rv_ref, rg_ref, aq_ref, k_ref, v_ref, gr_ref, ga_ref = refs[:9]
    for src, dst in zip(cast_in, refs[9:]):
        dst[...] = src[...].astype(BF16)
    h = _rmsnorm(x_ref[...], g_ref[...]).astype(BF16)
    cs = cs_ref[...]
    sn = sn_ref[...]

    def z_cols(c0):
        return jnp.dot(h, w_ref[:, c0:c0 + 2 * LANES], preferred_element_type=F32)

    def plain(off, width, out_ref):
        for c in range(0, width, 2 * LANES):
            out_ref[:, c:c + 2 * LANES] = z_cols(off + c).astype(out_ref.dtype)

    def roped(off, width, out_ref, scale):
        for c in range(0, width, 2 * LANES):
            z = z_cols(off + c)
            for hh in range(2):
                r = _rope_head(z[:, hh * LANES:(hh + 1) * LANES], cs, sn)
                if scale is not None:
                    r = r * scale
                out_ref[:, c + hh * LANES:c + (hh + 1) * LANES] = r.astype(out_ref.dtype)

    roped(OFF_RQ, R_QK_W, rq_ref, R_DK ** -0.5)
    roped(OFF_RK, R_QK_W, rk_ref, None)
    plain(OFF_RV, R_V_W, rv_ref)
    plain(OFF_RG, R_V_W, rg_ref)
    roped(OFF_AQ, A_W, aq_ref, A_DH ** -0.5)
    roped(OFF_AK, A_W, k_ref, None)
    plain(OFF_AV, A_W, v_ref)
    plain(OFF_GR, D_MODEL, gr_ref)
    plain(OFF_GA, D_MODEL, ga_ref)


def _inproj(x, g, w_bf, cs, sn, seq, cast_f32=(), tm=256):
    t = x.shape[0]
    steps = t // tm
    nseq = seq // tm
    row = lambda i: (i, 0)
    tab = lambda i: (i % nseq, 0)
    outs = [(R_QK_W, BF16), (R_QK_W, BF16), (R_V_W, BF16), (R_V_W, F32), (A_W, BF16),
            (A_W, F32), (A_W, F32), (D_MODEL, F32), (D_MODEL, F32)]
    cast_specs = [pl.BlockSpec((a.shape[0] // steps, a.shape[1]), row) for a in cast_f32]
    return pl.pallas_call(
        functools.partial(_inproj_kernel, len(cast_f32)),
        grid=(steps,),
        in_specs=[pl.BlockSpec((tm, D_MODEL), row),
                  pl.BlockSpec((1, D_MODEL), lambda i: (0, 0)),
                  pl.BlockSpec((D_MODEL, D_IN), lambda i: (0, 0), pipeline_mode=pl.Buffered(1)),
                  pl.BlockSpec((tm, LANES), tab),
                  pl.BlockSpec((tm, LANES), tab)] + cast_specs,
        out_specs=[pl.BlockSpec((tm, w), row) for w, _ in outs] + cast_specs,
        out_shape=([jax.ShapeDtypeStruct((t, w), d) for w, d in outs]
                   + [jax.ShapeDtypeStruct(a.shape, BF16) for a in cast_f32]),
        compiler_params=_params(("parallel",)),
        name="inproj",
    )(x, g, w_bf, cs, sn, *cast_f32)


def _castable_in_steps(a, steps):
    return a.shape[0] % steps == 0 and (a.shape[0] // steps) % 16 == 0


def _retention_kernel(q_ref, k_ref, v_ref, g_ref, dm_ref, cd_ref, kd_ref, sd_ref, o_ref, st_ref, s_scr):
    nchunk = q_ref.shape[0] // R_CHUNK
    s_scr[...] = jnp.zeros_like(s_scr)
    for c in range(nchunk):
        rows = slice(c * R_CHUNK, (c + 1) * R_CHUNK)
        q = q_ref[rows, :]
        k = k_ref[rows, :]
        v = v_ref[rows, :]
        s = s_scr[...]
        scores = lax.dot_general(q, k, NT, preferred_element_type=F32) * dm_ref[0]
        inner = jnp.dot(scores.astype(BF16), v, preferred_element_type=F32)
        cross = jnp.dot(q, s.astype(BF16), preferred_element_type=F32) * cd_ref[0]
        k_dec_t = (k.astype(F32) * kd_ref[0]).T.astype(BF16)
        s_scr[...] = s * sd_ref[0] + jnp.dot(k_dec_t, v, preferred_element_type=F32)
        o = inner + cross
        mu = jnp.mean(o, axis=-1, keepdims=True)
        d = o - mu
        var = jnp.mean(d * d, axis=-1, keepdims=True)
        y = d * lax.rsqrt(var + EPS)
        gate = g_ref[rows, :]
        o_ref[rows, :] = (gate * jax.nn.sigmoid(gate) * y).astype(o_ref.dtype)
    st_ref[0, 0] = s_scr[...]


def _retention(rq, rk, rv, rg, tabs, nb, seq):
    dm, cd, kd, sd = tabs
    head_tab = lambda shp: pl.BlockSpec((1,) + shp, lambda b, h: (h, 0, 0))
    return pl.pallas_call(
        _retention_kernel,
        grid=(nb, R_HEADS),
        in_specs=[pl.BlockSpec((seq, R_DK), lambda b, h: (b, h)),
                  pl.BlockSpec((seq, R_DK), lambda b, h: (b, h)),
                  pl.BlockSpec((seq, R_DV), lambda b, h: (b, h)),
                  pl.BlockSpec((seq, R_DV), lambda b, h: (b, h)),
                  head_tab((R_CHUNK, R_CHUNK)), head_tab((R_CHUNK, R_DV)),
                  head_tab((R_CHUNK, R_DK)), head_tab((R_DK, R_DV))],
        out_specs=[pl.BlockSpec((seq, R_DV), lambda b, h: (b, h)),
                   pl.BlockSpec((1, 1, R_DK, R_DV), lambda b, h: (b, h, 0, 0))],
        out_shape=[jax.ShapeDtypeStruct((nb * seq, R_V_W), BF16),
                   jax.ShapeDtypeStruct((nb, R_HEADS, R_DK, R_DV), F32)],
        scratch_shapes=[pltpu.VMEM((R_DK, R_DV), F32)],
        compiler_params=_params(("parallel", "parallel")),
        name="retention",
    )(rq, rk, rv, rg, dm, cd, kd, sd)


def _retention_tables():
    ld = jnp.log(1.0 - 2.0 ** (-5.0 - jnp.arange(R_HEADS, dtype=F32)))[:, None, None]
    i = jnp.arange(R_CHUNK, dtype=F32)
    diff = i[:, None] - i[None, :]
    dmask = jnp.where(diff >= 0, jnp.exp(ld * jnp.maximum(diff, 0.0)), 0.0)
    cdec = jnp.broadcast_to(jnp.exp(ld * (i + 1.0)[None, :, None]), (R_HEADS, R_CHUNK, R_DV))
    kdec = jnp.broadcast_to(jnp.exp(ld * (R_CHUNK - 1.0 - i)[None, :, None]), (R_HEADS, R_CHUNK, R_DK))
    sdec = jnp.broadcast_to(jnp.exp(ld * R_CHUNK), (R_HEADS, R_DK, R_DV))
    return dmask, cdec, kdec, sdec


MASK_BIAS = -1e30
MOBA_SLOTS = 2


def _moba_kernel(n_page, *refs):
    if n_page:
        refs = refs[1:]
    q_ref, k_ref, v_ref = refs[:3]
    pages = refs[3:3 + n_page]
    if n_page:
        o_ref, ks_ref, kb_scr, vb_scr, km_scr, s_scr, p_scr = refs[3 + n_page:]
        _sum_pages(pages, ks_ref.at[0])
    else:
        o_ref, kb_scr, vb_scr, km_scr, s_scr, p_scr = refs[3:]
    blk_sz = MOBA_BLOCK
    nblk = k_ref.shape[0] // blk_sz
    lane = lax.broadcasted_iota(jnp.int32, (blk_sz, A_DH), 1)
    for n in range(nblk):
        rows = slice(n * blk_sz, (n + 1) * blk_sz)
        kn = k_ref[rows, :]
        kb_scr[rows, 0:A_DH] = kn.astype(BF16)
        kb_scr[rows, A_DH:2 * A_DH] = jnp.where(lane == n, 1.0, 0.0).astype(BF16)
        km_scr[n:n + 1, :] = jnp.mean(kn, axis=0, keepdims=True)
        vb_scr[rows, 0:A_DH] = v_ref[rows, :].astype(BF16)
        vb_scr[rows, A_DH:2 * A_DH] = jnp.ones((blk_sz, A_DH), BF16)
    kmean = km_scr[...].astype(BF16)

    for i in range(nblk):
        q = q_ref[i * blk_sz:(i + 1) * blk_sz, :]
        gate = lax.dot_general(kmean, q, NT, preferred_element_type=F32)
        blk = lax.broadcasted_iota(jnp.int32, gate.shape, 0)
        rank = jnp.zeros(gate.shape, jnp.int32)
        for m in range(i):
            gm = gate[m:m + 1, :]
            ahead = (gm > gate) | ((gm == gate) & (m < blk))
            rank = rank + ahead.astype(jnp.int32)
        bias_t = jnp.where((blk < i) & (rank >= MOBA_TOPK), MASK_BIAS, 0.0)
        bias_t = jnp.concatenate([bias_t, jnp.zeros((A_DH - nblk, blk_sz), F32)], axis=0)
        q_aug = jnp.concatenate([q, bias_t.T.astype(BF16)], axis=1)

        nk = (i + 1) * blk_sz
        half = blk_sz // 2
        m_run = jnp.full((blk_sz, half), -jnp.inf, F32)
        for n in range(i + 1):
            cols = slice(n * blk_sz, (n + 1) * blk_sz)
            s = lax.dot_general(q_aug, kb_scr[cols, :], NT, preferred_element_type=F32)
            if n == i:
                qpos = lax.broadcasted_iota(jnp.int32, s.shape, 0)
                kpos = lax.broadcasted_iota(jnp.int32, s.shape, 1)
                s = jnp.where(kpos <= qpos, s, -jnp.inf)
            s_scr[i % MOBA_SLOTS, :, cols] = s
            m_run = jnp.maximum(m_run, jnp.maximum(s[:, :half], s[:, half:]))
        m_row = jnp.max(m_run, axis=-1, keepdims=True)
        for n in range(i + 1):
            cols = slice(n * blk_sz, (n + 1) * blk_sz)
            p_scr[i % MOBA_SLOTS, :, cols] = jnp.exp(s_scr[i % MOBA_SLOTS, :, cols] - m_row).astype(BF16)
        o = jnp.dot(p_scr[i % MOBA_SLOTS, :, 0:nk], vb_scr[0:nk, :], preferred_element_type=F32)
        o_ref[i * blk_sz:(i + 1) * blk_sz, :] = (o[:, :A_DH] / o[:, A_DH:]).astype(o_ref.dtype)


def _moba(aq, k, v, nb, seq, cache_k=None, pt_flat=None, n_page=0):
    nblk = seq // MOBA_BLOCK
    assert nblk <= A_DH
    blk = lambda b, h, *_: (b, h)
    in_specs = [pl.BlockSpec((seq, A_DH), blk)] * 3
    out_specs = [pl.BlockSpec((seq, A_DH), blk)]
    out_shape = [jax.ShapeDtypeStruct((nb * seq, A_W), BF16)]
    args = [aq, k, v]
    if n_page:
        blk_per_step = n_page * PAGE_SIZE // MOBA_BLOCK
        for q in range(n_page):
            in_specs.append(pl.BlockSpec((1, PAGE_SIZE, A_HEADS, A_DH),
                                         lambda b, h, pt, q=q: (pt[(b * A_HEADS + h) * n_page + q], 0, 0, 0)))
        out_specs.append(pl.BlockSpec((1, blk_per_step, A_HEADS, A_DH), lambda b, h, pt: (b * A_HEADS + h, 0, 0, 0)))
        out_shape.append(jax.ShapeDtypeStruct((nb * A_HEADS, blk_per_step, A_HEADS, A_DH), F32))
        args = [pt_flat] + args + [cache_k] * n_page
    grid_spec = pltpu.PrefetchScalarGridSpec(
        num_scalar_prefetch=1 if n_page else 0,
        grid=(nb, A_HEADS),
        in_specs=in_specs,
        out_specs=out_specs,
        scratch_shapes=[pltpu.VMEM((seq, 2 * A_DH), BF16),
                        pltpu.VMEM((seq, 2 * A_DH), BF16),
                        pltpu.VMEM((nblk, A_DH), F32),
                        pltpu.VMEM((MOBA_SLOTS, MOBA_BLOCK, seq), F32),
                        pltpu.VMEM((MOBA_SLOTS, MOBA_BLOCK, seq), BF16)],
    )
    out = pl.pallas_call(
        functools.partial(_moba_kernel, n_page),
        grid_spec=grid_spec,
        out_shape=out_shape,
        compiler_params=_params(("parallel", "parallel")),
        name="moba",
    )(*args)
    return (out[0], out[1].reshape(-1, A_HEADS, A_DH)) if n_page else out[0]


def _route(logits):
    lane = lax.broadcasted_iota(jnp.int32, logits.shape, 1)
    ninf = -jnp.inf
    big = jnp.int32(ROUTER_LANES)
    glog = jnp.where(lane < N_GROUPS, logits, ninf)
    gmax = jnp.max(glog, axis=-1, keepdims=True)
    g_idx = jnp.min(jnp.where(glog == gmax, lane, big), axis=-1, keepdims=True)
    g_w = 1.0 / jnp.sum(jnp.exp(glog - gmax), axis=-1, keepdims=True)
    e_lo = EXPERT_LANE0 + g_idx * EXPERTS_PER_GROUP
    in_group = (lane >= e_lo) & (lane < e_lo + EXPERTS_PER_GROUP)
    e1 = jnp.where(in_group, logits, ninf)
    v1 = jnp.max(e1, axis=-1, keepdims=True)
    i1 = jnp.min(jnp.where(e1 == v1, lane, big), axis=-1, keepdims=True)
    e2 = jnp.where(lane == i1, ninf, e1)
    v2 = jnp.max(e2, axis=-1, keepdims=True)
    i2 = jnp.min(jnp.where(e2 == v2, lane, big), axis=-1, keepdims=True)
    x2 = jnp.exp(v2 - v1)
    den = 1.0 + x2
    comb = jnp.where(lane == i1, g_w * (1.0 / den), jnp.where(lane == i2, g_w * (x2 / den), 0.0))
    return jnp.where(lane == GROUP_ID_LANE, g_idx.astype(F32), comb)


def _outproj_kernel(precise, ret_ref, att_ref, gr_ref, ga_ref, x_ref, wr_ref, wa_ref, wo_ref,
                    gf_ref, wrt_ref, brt_ref, y1_ref, h2_ref, comb_ref):
    prec = HI if precise else None
    mm = F32 if precise else BF16
    r = jnp.dot(ret_ref[...].astype(mm), wr_ref[...], precision=prec, preferred_element_type=F32)
    a = jnp.dot(att_ref[...].astype(mm), wa_ref[...], precision=prec, preferred_element_type=F32)
    merged = jax.nn.sigmoid(gr_ref[...]) * r + jax.nn.sigmoid(ga_ref[...]) * a
    m = jnp.dot(merged.astype(mm), wo_ref[...], precision=prec, preferred_element_type=F32)
    y1 = x_ref[...] + m
    y1_ref[...] = y1
    h2 = _rmsnorm(y1, gf_ref[...])
    h2_ref[...] = h2.astype(h2_ref.dtype)
    logits = jnp.dot(h2.astype(mm), wrt_ref[...], precision=prec, preferred_element_type=F32) + brt_ref[...]
    comb_ref[...] = _route(logits)


def _outproj(ret, att, gr, ga, x, w_ret_o, w_att_o, w_o, g_ffn, w_router, b_router, precise, tm):
    t = x.shape[0]
    row = lambda i: (i, 0)
    full = lambda i: (0, 0)
    wspec = lambda: pl.BlockSpec((D_MODEL, D_MODEL), full, pipeline_mode=pl.Buffered(1))
    return pl.pallas_call(
        functools.partial(_outproj_kernel, precise),
        grid=(t // tm,),
        in_specs=[pl.BlockSpec((tm, D_MODEL), row)] * 5 + [wspec(), wspec(), wspec(),
                  pl.BlockSpec((1, D_MODEL), full),
                  pl.BlockSpec(w_router.shape, full),
                  pl.BlockSpec((1, ROUTER_LANES), full)],
        out_specs=[pl.BlockSpec((tm, D_MODEL), row), pl.BlockSpec((tm, D_MODEL), row),
                   pl.BlockSpec((tm, ROUTER_LANES), row)],
        out_shape=[jax.ShapeDtypeStruct((t, D_MODEL), F32), jax.ShapeDtypeStruct((t, D_MODEL), BF16),
                   jax.ShapeDtypeStruct((t, ROUTER_LANES), F32)],
        compiler_params=_params(("parallel",)),
        name="outproj_precise" if precise else "outproj",
    )(ret, att, gr, ga, x, w_ret_o, w_att_o, w_o, g_ffn, w_router, b_router)


def _sum_pages(pages, out_block):
    per_blk = MOBA_BLOCK // PAGE_SIZE
    for n in range(len(pages) // per_blk):
        acc = jnp.sum(pages[n * per_blk][0], axis=0)
        for p in range(1, per_blk):
            acc = acc + jnp.sum(pages[n * per_blk + p][0], axis=0)
        out_block[n] = acc


def _moe_kernel(epb, h_ref, comb_ref, y1_ref, wg_ref, wu_ref, wd_ref, gf_ref, o_ref, acc_ref):
    e = pl.program_id(1)

    @pl.when(e == 0)
    def _():
        acc_ref[...] = jnp.zeros_like(acc_ref)

    h = h_ref[...]
    comb = comb_ref[...]
    lane = lax.broadcasted_iota(jnp.int32, comb.shape, 1)
    for j in range(epb):
        c = jnp.sum(jnp.where(lane == EXPERT_LANE0 + e * epb + j, comb, 0.0), axis=-1, keepdims=True)
        a = jnp.dot(h, wg_ref[j], preferred_element_type=F32)
        u = jnp.dot(h, wu_ref[j], preferred_element_type=F32)
        act = (a * jax.nn.sigmoid(a)) * u * c
        acc_ref[...] += jnp.dot(act.astype(BF16), wd_ref[j], preferred_element_type=F32)

    @pl.when(e == pl.num_programs(1) - 1)
    def _():
        o_ref[...] = _rmsnorm(y1_ref[...] + acc_ref[...], gf_ref[...])


def _moe(h2, comb, y1, wg, wu, wd, g_final, tm, epb=4):
    t = h2.shape[0]
    row = lambda i, e: (i, 0)
    wblk = lambda i, e: (e, 0, 0)
    return pl.pallas_call(
        functools.partial(_moe_kernel, epb),
        grid=(t // tm, N_EXPERTS // epb),
        in_specs=[pl.BlockSpec((tm, D_MODEL), row), pl.BlockSpec((tm, ROUTER_LANES), row),
                  pl.BlockSpec((tm, D_MODEL), row),
                  pl.BlockSpec((epb, D_MODEL, D_EXPERT), wblk),
                  pl.BlockSpec((epb, D_MODEL, D_EXPERT), wblk),
                  pl.BlockSpec((epb, D_EXPERT, D_MODEL), wblk),
                  pl.BlockSpec((1, D_MODEL), lambda i, e: (0, 0))],
        out_specs=pl.BlockSpec((tm, D_MODEL), row),
        out_shape=jax.ShapeDtypeStruct((t, D_MODEL), F32),
        scratch_shapes=[pltpu.VMEM((tm, D_MODEL), F32)],
        compiler_params=_params(("parallel", "arbitrary")),
        name="moe",
    )(h2, comb, y1, wg, wu, wd, g_final)


SORT_TILE = 128
EXPERT_SPLIT = 2
COMB_LANES = 40
assert EXPERT_LANE0 + N_EXPERTS <= COMB_LANES and 3 * COMB_LANES <= LANES


def _moe_sorted_kernel(n_page, *refs):
    if n_page:
        refs = refs[1:]
    h_ref, comb_ref, y1_ref, wg_ref, wu_ref, wd_ref, gf_ref, tri_ref = refs[:8]
    pages = refs[8:8 + n_page]
    rest = refs[8 + n_page:]
    if n_page:
        o_ref, ks_ref = rest[:2]
        _sum_pages(pages, ks_ref.at[0])
        rest = rest[2:]
    else:
        o_ref, rest = rest[0], rest[1:]
    xs_scr, cs_scr, out_scr, dest_scr, run_smem = rest
    g = pl.program_id(1)
    part = pl.program_id(2)
    tb = h_ref.shape[0]
    cap = xs_scr.shape[0]
    epb = wg_ref.shape[0]

    @pl.when((g == 0) & (part == 0))
    def _sort_block():
        comb = comb_ref[...]
        lane = lax.broadcasted_iota(jnp.int32, comb.shape, 1)
        gcol = comb[:, GROUP_ID_LANE:GROUP_ID_LANE + 1]
        ind = jnp.where((lane < N_GROUPS) & (lane.astype(F32) == gcol), 1.0, 0.0)
        pos = jnp.dot(tri_ref[...], ind.astype(BF16), preferred_element_type=F32)
        dest = jnp.zeros((tb, 1), F32)
        start = jnp.int32(0)
        for gg in range(N_GROUPS):
            col = ind[:, gg:gg + 1]
            n_tiles = (jnp.sum(col).astype(jnp.int32) + SORT_TILE - 1) // SORT_TILE
            run_smem[gg] = start
            run_smem[N_GROUPS + gg] = n_tiles
            dest = dest + col * (pos[:, gg:gg + 1] + start.astype(F32))
            start = start + n_tiles * SORT_TILE
        dest_b = jnp.broadcast_to(dest, (tb, LANES))
        dest_scr[...] = dest_b
        dest_row = dest_b.T[0:1, :].astype(jnp.int32)
        c_hi = comb.astype(BF16).astype(F32)
        r1 = comb - c_hi
        c_mid = r1.astype(BF16).astype(F32)
        c_lo = r1 - c_mid
        packed = (c_hi + pltpu.roll(c_mid, COMB_LANES, 1) + pltpu.roll(c_lo, 2 * COMB_LANES, 1)).astype(BF16)
        lane_t = lax.broadcasted_iota(jnp.int32, (SORT_TILE, ROUTER_LANES), 1)
        h = h_ref[...]
        for c in range(cap // SORT_TILE):
            rows = slice(c * SORT_TILE, (c + 1) * SORT_TILE)

            def sort_rows(c=c, rows=rows):
                riota = lax.broadcasted_iota(jnp.int32, (SORT_TILE, tb), 0) + c * SORT_TILE
                perm = jnp.where(riota == dest_row, 1.0, 0.0).astype(BF16)
                xs_scr[rows, :] = jnp.dot(perm, h, preferred_element_type=F32).astype(BF16)
                pk = jnp.dot(perm, packed, preferred_element_type=F32)
                terms = (pk + pltpu.roll(pk, ROUTER_LANES - COMB_LANES, 1)
                         + pltpu.roll(pk, ROUTER_LANES - 2 * COMB_LANES, 1))
                cs_scr[rows, :] = jnp.where(lane_t < COMB_LANES, terms, 0.0)

            if c * SORT_TILE < tb:
                sort_rows()
            else:
                pl.when(c * SORT_TILE < start)(sort_rows)
        out_scr[...] = jnp.zeros_like(out_scr)

    start = run_smem[g]
    n_tiles = run_smem[N_GROUPS + g]
    lane_c = lax.broadcasted_iota(jnp.int32, (SORT_TILE, ROUTER_LANES), 1)
    lane0 = EXPERT_LANE0 + g * EXPERTS_PER_GROUP + part * epb

    def run_tile(t, carry):
        r0 = pl.multiple_of(start + t * SORT_TILE, SORT_TILE)
        x = xs_scr[pl.ds(r0, SORT_TILE), :]
        cs = cs_scr[pl.ds(r0, SORT_TILE), :]
        acc = jnp.zeros((SORT_TILE, D_MODEL), F32)
        for j in range(epb):
            c = jnp.sum(jnp.where(lane_c == lane0 + j, cs, 0.0), axis=-1, keepdims=True)
            a = jnp.dot(x, wg_ref[j], preferred_element_type=F32)
            u = jnp.dot(x, wu_ref[j], preferred_element_type=F32)
            act = (a * jax.nn.sigmoid(a)) * u * c
            acc = acc + jnp.dot(act.astype(BF16), wd_ref[j], preferred_element_type=F32)
        out_scr[pl.ds(r0, SORT_TILE), :] += acc
        return carry

    lax.fori_loop(0, n_tiles, run_tile, 0)

    @pl.when((g == pl.num_programs(1) - 1) & (part == pl.num_programs(2) - 1))
    def _unsort_block():
        chunk = 2 * SORT_TILE
        lane_r = lax.broadcasted_iota(jnp.int32, (chunk, cap), 1)
        xs_scr[...] = out_scr[...].astype(BF16)
        for c in range(tb // chunk):
            rows = slice(c * chunk, (c + 1) * chunk)
            perm_t = jnp.where(lane_r == dest_scr[rows, 0:1].astype(jnp.int32), 1.0, 0.0).astype(BF16)
            moe = jnp.dot(perm_t, xs_scr[...], preferred_element_type=F32)
            o_ref[rows, :] = _rmsnorm(y1_ref[rows, :] + moe, gf_ref[...])


def _rider_plan(steps, max_pages, n_seq, pages_per_seq):
    per_blk = MOBA_BLOCK // PAGE_SIZE
    n_page = min(max_pages, n_seq * pages_per_seq // steps) // per_blk * per_blk
    if n_page == 0 or pages_per_seq % n_page or (steps * n_page) % pages_per_seq:
        return 0, 0
    return n_page, steps * n_page // pages_per_seq


def _moe_sorted(h2, comb, y1, wg, wu, wd, g_final, tb, cache_k=None, pt_flat=None, n_page=0):
    t = h2.shape[0]
    epb = EXPERTS_PER_GROUP // EXPERT_SPLIT
    cap = tb + N_GROUPS * SORT_TILE
    grid = (t // tb, N_GROUPS, EXPERT_SPLIT)
    row = lambda b, g, p, *_: (b, 0)
    wblk = lambda b, g, p, *_: (g * EXPERT_SPLIT + p, 0, 0)
    const = lambda b, g, p, *_: (0, 0)
    tri = jnp.tril(jnp.ones((tb, tb), BF16), -1)
    in_specs = [pl.BlockSpec((tb, D_MODEL), row, pipeline_mode=pl.Buffered(1)),
                pl.BlockSpec((tb, ROUTER_LANES), row, pipeline_mode=pl.Buffered(1)),
                pl.BlockSpec((tb, D_MODEL), row, pipeline_mode=pl.Buffered(1)),
                pl.BlockSpec((epb, D_MODEL, D_EXPERT), wblk),
                pl.BlockSpec((epb, D_MODEL, D_EXPERT), wblk),
                pl.BlockSpec((epb, D_EXPERT, D_MODEL), wblk),
                pl.BlockSpec((1, D_MODEL), const),
                pl.BlockSpec((tb, tb), const, pipeline_mode=pl.Buffered(1))]
    out_specs = [pl.BlockSpec((tb, D_MODEL), row)]
    out_shape = [jax.ShapeDtypeStruct((t, D_MODEL), F32)]
    args = [h2, comb, y1, wg, wu, wd, g_final, tri]
    if n_page:
        blk_per_step = n_page * PAGE_SIZE // MOBA_BLOCK
        steps = grid[0] * grid[1] * grid[2]
        step = lambda b, g, p: (b * N_GROUPS + g) * EXPERT_SPLIT + p
        for q in range(n_page):
            in_specs.append(pl.BlockSpec((1, PAGE_SIZE, A_HEADS, A_DH),
                                         lambda b, g, p, pt, q=q: (pt[step(b, g, p) * n_page + q], 0, 0, 0)))
        out_specs.append(pl.BlockSpec((1, blk_per_step, A_HEADS, A_DH), lambda b, g, p, pt: (step(b, g, p), 0, 0, 0)))
        out_shape.append(jax.ShapeDtypeStruct((steps, blk_per_step, A_HEADS, A_DH), F32))
        args = [pt_flat] + args + [cache_k] * n_page
    grid_spec = pltpu.PrefetchScalarGridSpec(
        num_scalar_prefetch=1 if n_page else 0,
        grid=grid,
        in_specs=in_specs,
        out_specs=out_specs,
        scratch_shapes=[pltpu.VMEM((cap, D_MODEL), BF16),
                        pltpu.VMEM((cap, ROUTER_LANES), F32),
                        pltpu.VMEM((cap, D_MODEL), F32),
                        pltpu.VMEM((tb, LANES), F32),
                        pltpu.SMEM((2 * N_GROUPS,), jnp.int32)],
    )
    out = pl.pallas_call(
        functools.partial(_moe_sorted_kernel, n_page),
        grid_spec=grid_spec,
        out_shape=out_shape,
        compiler_params=_params(("parallel", "arbitrary", "arbitrary")),
        name="moe_sorted",
    )(*args)
    return (out[0], out[1].reshape(-1, A_HEADS, A_DH)) if n_page else out[0]


def _inproj_sample_kernel(x_ref, g_ref, w_ref, ta_ref, tb_ref, o_ref, wbf_ref):
    w = w_ref[...]
    wbf_ref[...] = w.astype(BF16)
    h = _rmsnorm(x_ref[...], g_ref[...])
    z = jnp.dot(h, w, precision=HI, preferred_element_type=F32)
    for c in range(0, z.shape[1], LANES):
        zh = z[:, c:c + LANES]
        o_ref[:, c:c + LANES] = zh * ta_ref[:, c:c + LANES] + pltpu.roll(zh, A_DH // 2, 1) * tb_ref[:, c:c + LANES]


def _inproj_sample(x, g, w_in, ta, tb, tn=512):
    rows = x.shape[0]
    return pl.pallas_call(
        _inproj_sample_kernel,
        grid=(D_IN // tn,),
        in_specs=[pl.BlockSpec((rows, D_MODEL), lambda j: (0, 0)),
                  pl.BlockSpec((1, D_MODEL), lambda j: (0, 0)),
                  pl.BlockSpec((D_MODEL, tn), lambda j: (0, j)),
                  pl.BlockSpec((rows, tn), lambda j: (0, j)),
                  pl.BlockSpec((rows, tn), lambda j: (0, j))],
        out_specs=[pl.BlockSpec((rows, tn), lambda j: (0, j)), pl.BlockSpec((D_MODEL, tn), lambda j: (0, j))],
        out_shape=[jax.ShapeDtypeStruct((rows, D_IN), F32), jax.ShapeDtypeStruct((D_MODEL, D_IN), BF16)],
        compiler_params=_params(("parallel",)),
        name="inproj_sample",
    )(x, g, w_in, ta, tb)


def _sample_tables(rows, pos):
    half = A_DH // 2
    inv = ROPE_THETA ** (-jnp.arange(half, dtype=F32) / half)
    ang = jnp.full((rows, 1), pos, F32) * inv[None, :]
    cs = jnp.concatenate([jnp.cos(ang), jnp.cos(ang)], axis=-1)
    sn = jnp.concatenate([-jnp.sin(ang), jnp.sin(ang)], axis=-1)
    ones = jnp.ones((rows, LANES), F32)
    zeros = jnp.zeros((rows, LANES), F32)
    ta, tb = [], []
    for off, width, kind in ((OFF_RQ, R_QK_W, R_DK ** -0.5), (OFF_RK, R_QK_W, 1.0), (OFF_RV, 2 * R_V_W, None),
                             (OFF_AQ, A_W, A_DH ** -0.5), (OFF_AK, A_W, 1.0), (OFF_AV, A_W + 2 * D_MODEL, None)):
        for _ in range(width // LANES):
            ta.append(ones if kind is None else cs * kind)
            tb.append(zeros if kind is None else sn * kind)
    return jnp.concatenate(ta, axis=-1), jnp.concatenate(tb, axis=-1)


def _retention_step_kernel(q_ref, k_ref, v_ref, g_ref, s_ref, dec_ref, o_ref, sn_ref):
    for i in range(q_ref.shape[0]):
        for h in range(R_HEADS):
            col = lambda r: jnp.broadcast_to(r[i, h:h + 1, :], (R_DK, R_DK)).T[:, 0:1]
            q = col(q_ref)
            k = col(k_ref)
            v = v_ref[i, h]
            s = s_ref[i, h]
            dec = dec_ref[h]
            cross = jnp.sum(q * s, axis=0, keepdims=True) * dec
            inner = jnp.sum(q * k, axis=0, keepdims=True) * v
            sn_ref[i, h] = s * dec + k * v
            o = inner + cross
            mu = jnp.mean(o, axis=-1, keepdims=True)
            d = o - mu
            var = jnp.mean(d * d, axis=-1, keepdims=True)
            gate = g_ref[i, h]
            o_ref[i, h] = gate * jax.nn.sigmoid(gate) * (d * lax.rsqrt(var + EPS))


def _retention_step(q, k, v, g, state, dec):
    nb = q.shape[0]
    per = SAMPLE_SEQS_PER_STEP if nb % SAMPLE_SEQS_PER_STEP == 0 else 1
    col = pl.BlockSpec((per, R_HEADS, R_DK), lambda b: (b, 0, 0))
    rowv = pl.BlockSpec((per, R_HEADS, 1, R_DV), lambda b: (b, 0, 0, 0))
    st = pl.BlockSpec((per, R_HEADS, R_DK, R_DV), lambda b: (b, 0, 0, 0))
    return pl.pallas_call(
        _retention_step_kernel,
        grid=(nb // per,),
        in_specs=[col, col, rowv, rowv, st, pl.BlockSpec((R_HEADS, 1, R_DV), lambda b: (0, 0, 0))],
        out_specs=[rowv, st],
        out_shape=[jax.ShapeDtypeStruct((nb, R_HEADS, 1, R_DV), F32),
                   jax.ShapeDtypeStruct((nb, R_HEADS, R_DK, R_DV), F32)],
        compiler_params=_params(("parallel",)),
        name="retention_step",
    )(q, k, v, g, state, dec)


def _block_sum_kernel(pt_ref, *refs):
    _sum_pages(refs[:-1], refs[-1].at[0, 0])


def _block_sums(cache_k, pt_flat, nb, n_pages):
    steps = n_pages // PAGES_PER_STEP
    blk_per_step = PAGES_PER_STEP * PAGE_SIZE // MOBA_BLOCK

    def page_spec(p):
        return pl.BlockSpec((1, PAGE_SIZE, A_HEADS, A_DH),
                            lambda b, j, pt: (pt[b * n_pages + j * PAGES_PER_STEP + p], 0, 0, 0))

    grid_spec = pltpu.PrefetchScalarGridSpec(
        num_scalar_prefetch=1,
        grid=(nb, steps),
        in_specs=[page_spec(p) for p in range(PAGES_PER_STEP)],
        out_specs=pl.BlockSpec((1, 1, blk_per_step, A_HEADS, A_DH), lambda b, j, pt: (b, j, 0, 0, 0)),
    )
    out = pl.pallas_call(
        _block_sum_kernel,
        grid_spec=grid_spec,
        out_shape=jax.ShapeDtypeStruct((nb, steps, blk_per_step, A_HEADS, A_DH), F32),
        compiler_params=_params(("parallel", "parallel")),
        name="cache_block_sums",
    )(pt_flat, *([cache_k] * PAGES_PER_STEP))
    return out.reshape(nb, steps * blk_per_step, A_W)


def _block_pick_kernel(q_ref, ks_ref, o_ref):
    for i in range(q_ref.shape[0]):
        ks = ks_ref[i] * (1.0 / MOBA_BLOCK)
        prod = ks * q_ref[i]
        nblk = prod.shape[0]
        lane = lax.broadcasted_iota(jnp.int32, (nblk, LANES), 1)
        gate = jnp.full((nblk, LANES), -jnp.inf, F32)
        for h in range(A_HEADS):
            gate = jnp.where(lane == h, jnp.sum(prod[:, h * A_DH:(h + 1) * A_DH], axis=-1, keepdims=True), gate)
        blk = lax.broadcasted_iota(jnp.int32, gate.shape, 0)
        rank = jnp.zeros(gate.shape, jnp.int32)
        for m in range(nblk):
            gm = gate[m:m + 1, :]
            ahead = (gm > gate) | ((gm == gate) & (m < blk))
            rank = rank + ahead.astype(jnp.int32)
        for j in range(MOBA_TOPK):
            o_ref[i, j:j + 1, :] = jnp.sum(jnp.where(rank == j, blk, 0), axis=0, keepdims=True)


def _block_pick(q, ksum):
    nb, nblk, _ = ksum.shape
    per = SAMPLE_SEQS_PER_STEP if nb % SAMPLE_SEQS_PER_STEP == 0 else 1
    return pl.pallas_call(
        _block_pick_kernel,
        grid=(nb // per,),
        in_specs=[pl.BlockSpec((per, 1, A_W), lambda b: (b, 0, 0)),
                  pl.BlockSpec((per, nblk, A_W), lambda b: (b, 0, 0))],
        out_specs=pl.BlockSpec((per, MOBA_TOPK, LANES), lambda b: (b, 0, 0)),
        out_shape=jax.ShapeDtypeStruct((nb, MOBA_TOPK, LANES), jnp.int32),
        compiler_params=_params(("parallel",)),
        name="block_pick",
    )(q, ksum)


PICK_PAGES = MOBA_TOPK * (MOBA_BLOCK // PAGE_SIZE)


def _attend_sample_kernel(pages_per_seq, pt_ref, idx_ref, q_ref, kn_ref, vn_ref, ck_ref, cv_ref, o_ref,
                          kbuf, vbuf, ksem, vsem):
    b = pl.program_id(0)
    per_blk = MOBA_BLOCK // PAGE_SIZE

    def page_copies(seq_i, slot):
        out = []
        for h in range(A_HEADS):
            for j in range(MOBA_TOPK):
                blk = idx_ref[(seq_i * MOBA_TOPK + j) * A_HEADS + h]
                for p in range(per_blk):
                    page = pt_ref[seq_i * pages_per_seq + blk * per_blk + p]
                    dst = (slot, h, j * per_blk + p)
                    out.append(pltpu.make_async_copy(ck_ref.at[page, :, h, :], kbuf.at[dst], ksem.at[slot]))
                    out.append(pltpu.make_async_copy(cv_ref.at[page, :, h, :], vbuf.at[dst], vsem.at[slot]))
        return out

    slot = b % 2

    @pl.when(b == 0)
    def _():
        for c in page_copies(0, 0):
            c.start()

    @pl.when(b + 1 < pl.num_programs(0))
    def _():
        for c in page_copies(b + 1, 1 - slot):
            c.start()

    for c in page_copies(b, slot):
        c.wait()

    for h in range(A_HEADS):
        q = q_ref[0, h:h + 1, :]
        s_own = jnp.sum(q * kn_ref[0, h:h + 1, :], axis=-1, keepdims=True)
        scores = [jnp.sum(kbuf[slot, h, g] * q, axis=-1, keepdims=True) for g in range(PICK_PAGES)]
        m = s_own
        for s in scores:
            m = jnp.maximum(m, jnp.max(s, axis=0, keepdims=True))
        p_own = jnp.exp(s_own - m)
        l = p_own
        acc = p_own * vn_ref[0, h:h + 1, :]
        for g, s in enumerate(scores):
            p = jnp.exp(s - m)
            l = l + jnp.sum(p, axis=0, keepdims=True)
            acc = acc + jnp.sum(p * vbuf[slot, h, g], axis=0, keepdims=True)
        o_ref[0, h:h + 1, :] = acc / l


def _attend_sample(q, k_new, v_new, cache_k, cache_v, pt_flat, picks_flat, pages_per_seq):
    nb = q.shape[0]
    vec = pl.BlockSpec((1, A_HEADS, A_DH), lambda b, pt, idx: (b, 0, 0))
    hbm = pl.BlockSpec(memory_space=pl.ANY)
    buf = pltpu.VMEM((2, A_HEADS, PICK_PAGES, PAGE_SIZE, A_DH), F32)
    grid_spec = pltpu.PrefetchScalarGridSpec(
        num_scalar_prefetch=2,
        grid=(nb,),
        in_specs=[vec, vec, vec, hbm, hbm],
        out_specs=vec,
        scratch_shapes=[buf, buf, pltpu.SemaphoreType.DMA((2,)), pltpu.SemaphoreType.DMA((2,))],
    )
    return pl.pallas_call(
        functools.partial(_attend_sample_kernel, pages_per_seq),
        grid_spec=grid_spec,
        out_shape=jax.ShapeDtypeStruct((nb, A_HEADS, A_DH), F32),
        compiler_params=_params(("arbitrary",)),
        name="attend_sample",
    )(pt_flat, picks_flat, q, k_new, v_new, cache_k, cache_v)


def _rope_tables(seq):
    half = A_DH // 2
    inv = ROPE_THETA ** (-jnp.arange(half, dtype=F32) / half)
    ang = jnp.arange(seq, dtype=jnp.int32).astype(F32)[:, None] * inv[None, :]
    cos, sin = jnp.cos(ang), jnp.sin(ang)
    return jnp.concatenate([cos, cos], axis=-1), jnp.concatenate([-sin, sin], axis=-1)


def _router_weights(w_rg, b_rg, w_re, b_re):
    pad = ROUTER_LANES - N_GROUPS - N_EXPERTS
    w = jnp.concatenate([w_rg, w_re.reshape(D_MODEL, N_EXPERTS), jnp.zeros((D_MODEL, pad), F32)], axis=-1)
    b = jnp.concatenate([b_rg, b_re.reshape(N_EXPERTS), jnp.zeros((pad,), F32)])[None, :]
    return w, b


def kernel(x_prompt, x_sample, cache_k, cache_v, state_ret, page_table, norm_mix, w_in, w_ret_o, w_att_o, w_o,
           norm_ffn, w_router_group, b_router_group, w_router_expert, b_router_expert, w_gate, w_up, w_down,
           norm_final):
    nb, seq, _ = x_prompt.shape
    db, ds, _ = x_sample.shape
    depth = norm_mix.shape[0]
    past_len = page_table.shape[1] * PAGE_SIZE
    assert depth == 1 and ds == 1
    assert seq % MOBA_BLOCK == 0 and past_len % (PAGES_PER_STEP * PAGE_SIZE) == 0
    assert past_len // MOBA_BLOCK >= MOBA_TOPK
    l = 0
    t = nb * seq

    g_mix = norm_mix[l][None, :]
    g_ffn = norm_ffn[l][None, :]
    g_fin = norm_final[None, :]
    w_router, b_router = _router_weights(w_router_group[l], b_router_group[l], w_router_expert[l], b_router_expert[l])
    xs = x_sample.reshape(db, D_MODEL)
    ta, tb = _sample_tables(db, past_len)
    zs, w_in_bf = _inproj_sample(xs, g_mix, w_in[l], ta, tb)
    wro_bf, wao_bf, wo_bf = w_ret_o[l].astype(BF16), w_att_o[l].astype(BF16), w_o[l].astype(BF16)

    xp = x_prompt.reshape(t, D_MODEL)
    cs, sn = _rope_tables(seq)
    moe_w = [w_gate[l].reshape(-1, D_EXPERT), w_up[l].reshape(-1, D_EXPERT), w_down[l].reshape(-1, D_MODEL)]
    ride_casts = all(_castable_in_steps(a, t // INPROJ_TM) for a in moe_w)
    proj = _inproj(xp, g_mix, w_in_bf, cs, sn, seq, cast_f32=moe_w if ride_casts else (), tm=INPROJ_TM)
    rq, rk, rv, rg, aq, kp, vp, gr, ga = proj[:9]
    moe_w_bf = proj[9:] if ride_casts else [a.astype(BF16) for a in moe_w]
    wg_bf, wu_bf, wd_bf = (a.reshape(s.shape[1:]) for a, s in zip(moe_w_bf, (w_gate, w_up, w_down)))
    ret, ret_state = _retention(rq, rk, rv, rg, _retention_tables(), nb, seq)
    ck, cv = cache_k[l], cache_v[l]
    n_pages = page_table.shape[1]
    pt_flat = page_table.reshape(-1)
    blocks_per_seq = n_pages * PAGE_SIZE // MOBA_BLOCK
    moe_tb = MOE_BLOCK if t % MOE_BLOCK == 0 else MOE_BLOCK // 2
    moe_steps = (t // moe_tb) * N_GROUPS * EXPERT_SPLIT
    moe_pages, moe_seqs = _rider_plan(moe_steps, MOE_RIDER_PAGES, db, n_pages)
    moba_pages, moba_seqs = _rider_plan(nb * A_HEADS, MOBA_RIDER_PAGES, db - moe_seqs, n_pages)
    ksum_parts = []

    moba_out = _moba(aq, kp, vp, nb, seq, cache_k=ck, pt_flat=pt_flat[moe_seqs * n_pages:], n_page=moba_pages)
    att, ks_moba = moba_out if moba_pages else (moba_out, None)
    y1, h2, comb = _outproj(ret, att, gr, ga, xp, wro_bf, wao_bf, wo_bf, g_ffn, w_router.astype(BF16), b_router,
                            precise=False, tm=OUTPROJ_TM)
    moe_out = _moe_sorted(h2, comb, y1, wg_bf, wu_bf, wd_bf, g_fin, moe_tb,
                          cache_k=ck, pt_flat=pt_flat, n_page=moe_pages)
    y_prompt, ks_moe = moe_out if moe_pages else (moe_out, None)
    y_prompt = y_prompt.reshape(nb, seq, D_MODEL)
    if moe_pages:
        ksum_parts.append(ks_moe.reshape(moe_seqs, blocks_per_seq, A_W))
    if moba_pages:
        ksum_parts.append(ks_moba.reshape(moba_seqs, blocks_per_seq, A_W))
    done_seqs = moe_seqs + moba_seqs
    if done_seqs < db:
        ksum_parts.append(_block_sums(ck, pt_flat[done_seqs * n_pages:], db - done_seqs, n_pages))
    ksum = jnp.concatenate(ksum_parts, axis=0) if len(ksum_parts) > 1 else ksum_parts[0]

    seg = lambda off, width: zs[:, off:off + width]
    ld = jnp.log(1.0 - 2.0 ** (-5.0 - jnp.arange(R_HEADS, dtype=F32)))
    dec = jnp.broadcast_to(jnp.exp(ld * 1.0)[:, None, None], (R_HEADS, 1, R_DV))
    ret_s, state_s = _retention_step(
        seg(OFF_RQ, R_QK_W).reshape(db, R_HEADS, R_DK), seg(OFF_RK, R_QK_W).reshape(db, R_HEADS, R_DK),
        seg(OFF_RV, R_V_W).reshape(db, R_HEADS, 1, R_DV), seg(OFF_RG, R_V_W).reshape(db, R_HEADS, 1, R_DV),
        state_ret[l].astype(F32), dec)
    aq_s, ak_s, av_s = seg(OFF_AQ, A_W), seg(OFF_AK, A_W), seg(OFF_AV, A_W)
    picks = _block_pick(aq_s.reshape(db, 1, A_W), ksum)[:, :, :A_HEADS].reshape(-1)
    heads = lambda z: z.reshape(db, A_HEADS, A_DH)
    att_s = _attend_sample(heads(aq_s), heads(ak_s), heads(av_s), ck, cv, pt_flat, picks, n_pages)
    y1_s, h2_s, comb_s = _outproj(ret_s.reshape(db, R_V_W), att_s.reshape(db, A_W), seg(OFF_GR, D_MODEL),
                                  seg(OFF_GA, D_MODEL), xs, w_ret_o[l], w_att_o[l], w_o[l], g_ffn,
                                  w_router, b_router, precise=True, tm=db)
    y_sample = _moe(h2_s, comb_s, y1_s, wg_bf, wu_bf, wd_bf, g_fin, tm=db).reshape(db, ds, D_MODEL)

    k_prompt = kp.reshape(1, nb, seq, A_HEADS, A_DH)
    v_prompt = vp.reshape(1, nb, seq, A_HEADS, A_DH)
    ret_prompt = ret_state[None]
    k_sample = ak_s.reshape(1, db, ds, A_HEADS, A_DH).astype(cache_k.dtype)
    v_sample = av_s.reshape(1, db, ds, A_HEADS, A_DH).astype(cache_v.dtype)
    ret_sample = state_s[None].astype(state_ret.dtype)
    return (y_prompt, y_sample, k_prompt, v_prompt, ret_prompt, k_sample, v_sample, ret_sample)
```

```python
import functools

import jax
import jax.numpy as jnp
from jax import lax
from jax.experimental import pallas as pl
from jax.experimental.pallas import tpu as pltpu

D_MODEL = 1024
PAGE_SIZE = 128
R_HEADS = 4
R_DK = 128
R_DV = 256
R_CHUNK = 128
A_HEADS = 8
A_DH = 128
MOBA_BLOCK = 256
MOBA_TOPK = 3
ROPE_THETA = 10000.0
N_GROUPS = 4
EXPERTS_PER_GROUP = 8
N_EXPERTS = N_GROUPS * EXPERTS_PER_GROUP
D_EXPERT = 256
EPS = 1e-6

R_QK_W = R_HEADS * R_DK
R_V_W = R_HEADS * R_DV
A_W = A_HEADS * A_DH
D_IN = 2 * R_QK_W + 2 * R_V_W + 3 * A_W + 2 * D_MODEL
OFF_RQ = 0
OFF_RK = OFF_RQ + R_QK_W
OFF_RV = OFF_RK + R_QK_W
OFF_RG = OFF_RV + R_V_W
OFF_AQ = OFF_RG + R_V_W
OFF_AK = OFF_AQ + A_W
OFF_AV = OFF_AK + A_W
OFF_GR = OFF_AV + A_W
OFF_GA = OFF_GR + D_MODEL

LANES = 128
ROUTER_LANES = LANES
EXPERT_LANE0 = N_GROUPS
GROUP_ID_LANE = 0
PAGES_PER_STEP = 8
INPROJ_TM = 256
OUTPROJ_TM = 512
MOE_BLOCK = 1024
MOE_RIDER_PAGES = 8
MOBA_RIDER_PAGES = 16
SAMPLE_SEQS_PER_STEP = 4
VMEM_LIMIT = 56 * 1024 * 1024

F32 = jnp.float32
BF16 = jnp.bfloat16
HI = lax.Precision.HIGHEST
NT = (((1,), (1,)), ((), ()))


def _params(sem, vmem=VMEM_LIMIT):
    return pltpu.CompilerParams(dimension_semantics=sem, vmem_limit_bytes=vmem)


def _rmsnorm(x, g):
    return x * lax.rsqrt(jnp.mean(x * x, axis=-1, keepdims=True) + EPS) * g


def _rope_head(z, cs, sn):
    return z * cs + pltpu.roll(z, A_DH // 2, 1) * sn


def _inproj_kernel(n_cast, x_ref, g_ref, w_ref, cs_ref, sn_ref, *refs):
    cast_in, refs = refs[:n_cast], refs[n_cast:]
    rq_ref, rk_ref, rv_ref, rg_ref, aq_ref, k_ref, v_ref, gr_ref, ga_ref = refs[:9]
    for src, dst in zip(cast_in, refs[9:]):
        dst[...] = src[...].astype(BF16)
    h = _rmsnorm(x_ref[...], g_ref[...]).astype(BF16)
    cs = cs_ref[...]
    sn = sn_ref[...]

    def z_cols(c0):
        return jnp.dot(h, w_ref[:, c0:c0 + 2 * LANES], preferred_element_type=F32)

    def plain(off, width, out_ref):
        for c in range(0, width, 2 * LANES):
            out_ref[:, c:c + 2 * LANES] = z_cols(off + c).astype(out_ref.dtype)

    def roped(off, width, out_ref, scale):
        for c in range(0, width, 2 * LANES):
            z = z_cols(off + c)
            for hh in range(2):
                r = _rope_head(z[:, hh * LANES:(hh + 1) * LANES], cs, sn)
                if scale is not None:
                    r = r * scale
                out_ref[:, c + hh * LANES:c + (hh + 1) * LANES] = r.astype(out_ref.dtype)

    roped(OFF_RQ, R_QK_W, rq_ref, R_DK ** -0.5)
    roped(OFF_RK, R_QK_W, rk_ref, None)
    plain(OFF_RV, R_V_W, rv_ref)
    plain(OFF_RG, R_V_W, rg_ref)
    roped(OFF_AQ, A_W, aq_ref, A_DH ** -0.5)
    roped(OFF_AK, A_W, k_ref, None)
    plain(OFF_AV, A_W, v_ref)
    plain(OFF_GR, D_MODEL, gr_ref)
    plain(OFF_GA, D_MODEL, ga_ref)


def _inproj(x, g, w_bf, cs, sn, seq, cast_f32=(), tm=256):
    t = x.shape[0]
    steps = t // tm
    nseq = seq // tm
    row = lambda i: (i, 0)
    tab = lambda i: (i % nseq, 0)
    outs = [(R_QK_W, BF16), (R_QK_W, BF16), (R_V_W, BF16), (R_V_W, F32), (A_W, BF16),
            (A_W, F32), (A_W, F32), (D_MODEL, F32), (D_MODEL, F32)]
    cast_specs = [pl.BlockSpec((a.shape[0] // steps, a.shape[1]), row) for a in cast_f32]
    return pl.pallas_call(
        functools.partial(_inproj_kernel, len(cast_f32)),
        grid=(steps,),
        in_specs=[pl.BlockSpec((tm, D_MODEL), row),
                  pl.BlockSpec((1, D_MODEL), lambda i: (0, 0)),
                  pl.BlockSpec((D_MODEL, D_IN), lambda i: (0, 0), pipeline_mode=pl.Buffered(1)),
                  pl.BlockSpec((tm, LANES), tab),
                  pl.BlockSpec((tm, LANES), tab)] + cast_specs,
        out_specs=[pl.BlockSpec((tm, w), row) for w, _ in outs] + cast_specs,
        out_shape=([jax.ShapeDtypeStruct((t, w), d) for w, d in outs]
                   + [jax.ShapeDtypeStruct(a.shape, BF16) for a in cast_f32]),
        compiler_params=_params(("parallel",)),
        name="inproj",
    )(x, g, w_bf, cs, sn, *cast_f32)


def _castable_in_steps(a, steps):
    return a.shape[0] % steps == 0 and (a.shape[0] // steps) % 16 == 0


def _retention_kernel(q_ref, k_ref, v_ref, g_ref, dm_ref, cd_ref, kd_ref, sd_ref, o_ref, st_ref, s_scr):
    nchunk = q_ref.shape[0] // R_CHUNK
    s_scr[...] = jnp.zeros_like(s_scr)
    for c in range(nchunk):
        rows = slice(c * R_CHUNK, (c + 1) * R_CHUNK)
        q = q_ref[rows, :]
        k = k_ref[rows, :]
        v = v_ref[rows, :]
        s = s_scr[...]
        scores = lax.dot_general(q, k, NT, preferred_element_type=F32) * dm_ref[0]
        inner = jnp.dot(scores.astype(BF16), v, preferred_element_type=F32)
        cross = jnp.dot(q, s.astype(BF16), preferred_element_type=F32) * cd_ref[0]
        k_dec_t = (k.astype(F32) * kd_ref[0]).T.astype(BF16)
        s_scr[...] = s * sd_ref[0] + jnp.dot(k_dec_t, v, preferred_element_type=F32)
        o = inner + cross
        mu = jnp.mean(o, axis=-1, keepdims=True)
        d = o - mu
        var = jnp.mean(d * d, axis=-1, keepdims=True)
        y = d * lax.rsqrt(var + EPS)
        gate = g_ref[rows, :]
        o_ref[rows, :] = (gate * jax.nn.sigmoid(gate) * y).astype(o_ref.dtype)
    st_ref[0, 0] = s_scr[...]


def _retention(rq, rk, rv, rg, tabs, nb, seq):
    dm, cd, kd, sd = tabs
    head_tab = lambda shp: pl.BlockSpec((1,) + shp, lambda b, h: (h, 0, 0))
    return pl.pallas_call(
        _retention_kernel,
        grid=(nb, R_HEADS),
        in_specs=[pl.BlockSpec((seq, R_DK), lambda b, h: (b, h)),
                  pl.BlockSpec((seq, R_DK), lambda b, h: (b, h)),
                  pl.BlockSpec((seq, R_DV), lambda b, h: (b, h)),
                  pl.BlockSpec((seq, R_DV), lambda b, h: (b, h)),
                  head_tab((R_CHUNK, R_CHUNK)), head_tab((R_CHUNK, R_DV)),
                  head_tab((R_CHUNK, R_DK)), head_tab((R_DK, R_DV))],
        out_specs=[pl.BlockSpec((seq, R_DV), lambda b, h: (b, h)),
                   pl.BlockSpec((1, 1, R_DK, R_DV), lambda b, h: (b, h, 0, 0))],
        out_shape=[jax.ShapeDtypeStruct((nb * seq, R_V_W), BF16),
                   jax.ShapeDtypeStruct((nb, R_HEADS, R_DK, R_DV), F32)],
        scratch_shapes=[pltpu.VMEM((R_DK, R_DV), F32)],
        compiler_params=_params(("parallel", "parallel")),
        name="retention",
    )(rq, rk, rv, rg, dm, cd, kd, sd)


def _retention_tables():
    ld = jnp.log(1.0 - 2.0 ** (-5.0 - jnp.arange(R_HEADS, dtype=F32)))[:, None, None]
    i = jnp.arange(R_CHUNK, dtype=F32)
    diff = i[:, None] - i[None, :]
    dmask = jnp.where(diff >= 0, jnp.exp(ld * jnp.maximum(diff, 0.0)), 0.0)
    cdec = jnp.broadcast_to(jnp.exp(ld * (i + 1.0)[None, :, None]), (R_HEADS, R_CHUNK, R_DV))
    kdec = jnp.broadcast_to(jnp.exp(ld * (R_CHUNK - 1.0 - i)[None, :, None]), (R_HEADS, R_CHUNK, R_DK))
    sdec = jnp.broadcast_to(jnp.exp(ld * R_CHUNK), (R_HEADS, R_DK, R_DV))
    return dmask, cdec, kdec, sdec


MASK_BIAS = -1e30
MOBA_SLOTS = 2


def _moba_kernel(n_page, *refs):
    if n_page:
        refs = refs[1:]
    q_ref, k_ref, v_ref = refs[:3]
    pages = refs[3:3 + n_page]
    if n_page:
        o_ref, ks_ref, kb_scr, vb_scr, km_scr, s_scr, p_scr = refs[3 + n_page:]
        _sum_pages(pages, ks_ref.at[0])
    else:
        o_ref, kb_scr, vb_scr, km_scr, s_scr, p_scr = refs[3:]
    blk_sz = MOBA_BLOCK
    nblk = k_ref.shape[0] // blk_sz
    lane = lax.broadcasted_iota(jnp.int32, (blk_sz, A_DH), 1)
    for n in range(nblk):
        rows = slice(n * blk_sz, (n + 1) * blk_sz)
        kn = k_ref[rows, :]
        kb_scr[rows, 0:A_DH] = kn.astype(BF16)
        kb_scr[rows, A_DH:2 * A_DH] = jnp.where(lane == n, 1.0, 0.0).astype(BF16)
        km_scr[n:n + 1, :] = jnp.mean(kn, axis=0, keepdims=True)
        vb_scr[rows, 0:A_DH] = v_ref[rows, :].astype(BF16)
        vb_scr[rows, A_DH:2 * A_DH] = jnp.ones((blk_sz, A_DH), BF16)
    kmean = km_scr[...].astype(BF16)

    for i in range(nblk):
        q = q_ref[i * blk_sz:(i + 1) * blk_sz, :]
        gate = lax.dot_general(kmean, q, NT, preferred_element_type=F32)
        blk = lax.broadcasted_iota(jnp.int32, gate.shape, 0)
        rank = jnp.zeros(gate.shape, jnp.int32)
        for m in range(i):
            gm = gate[m:m + 1, :]
            ahead = (gm > gate) | ((gm == gate) & (m < blk))
            rank = rank + ahead.astype(jnp.int32)
        bias_t = jnp.where((blk < i) & (rank >= MOBA_TOPK), MASK_BIAS, 0.0)
        bias_t = jnp.concatenate([bias_t, jnp.zeros((A_DH - nblk, blk_sz), F32)], axis=0)
        q_aug = jnp.concatenate([q, bias_t.T.astype(BF16)], axis=1)

        nk = (i + 1) * blk_sz
        half = blk_sz // 2
        m_run = jnp.full((blk_sz, half), -jnp.inf, F32)
        for n in range(i + 1):
            cols = slice(n * blk_sz, (n + 1) * blk_sz)
            s = lax.dot_general(q_aug, kb_scr[cols, :], NT, preferred_element_type=F32)
            if n == i:
                qpos = lax.broadcasted_iota(jnp.int32, s.shape, 0)
                kpos = lax.broadcasted_iota(jnp.int32, s.shape, 1)
                s = jnp.where(kpos <= qpos, s, -jnp.inf)
            s_scr[i % MOBA_SLOTS, :, cols] = s
            m_run = jnp.maximum(m_run, jnp.maximum(s[:, :half], s[:, half:]))
        m_row = jnp.max(m_run, axis=-1, keepdims=True)
        for n in range(i + 1):
            cols = slice(n * blk_sz, (n + 1) * blk_sz)
            p_scr[i % MOBA_SLOTS, :, cols] = jnp.exp(s_scr[i % MOBA_SLOTS, :, cols] - m_row).astype(BF16)
        o = jnp.dot(p_scr[i % MOBA_SLOTS, :, 0:nk], vb_scr[0:nk, :], preferred_element_type=F32)
        o_ref[i * blk_sz:(i + 1) * blk_sz, :] = (o[:, :A_DH] / o[:, A_DH:]).astype(o_ref.dtype)


def _moba(aq, k, v, nb, seq, cache_k=None, pt_flat=None, n_page=0):
    nblk = seq // MOBA_BLOCK
    assert nblk <= A_DH
    blk = lambda b, h, *_: (b, h)
    in_specs = [pl.BlockSpec((seq, A_DH), blk)] * 3
    out_specs = [pl.BlockSpec((seq, A_DH), blk)]
    out_shape = [jax.ShapeDtypeStruct((nb * seq, A_W), BF16)]
    args = [aq, k, v]
    if n_page:
        blk_per_step = n_page * PAGE_SIZE // MOBA_BLOCK
        for q in range(n_page):
            in_specs.append(pl.BlockSpec((1, PAGE_SIZE, A_HEADS, A_DH),
                                         lambda b, h, pt, q=q: (pt[(b * A_HEADS + h) * n_page + q], 0, 0, 0)))
        out_specs.append(pl.BlockSpec((1, blk_per_step, A_HEADS, A_DH), lambda b, h, pt: (b * A_HEADS + h, 0, 0, 0)))
        out_shape.append(jax.ShapeDtypeStruct((nb * A_HEADS, blk_per_step, A_HEADS, A_DH), F32))
        args = [pt_flat] + args + [cache_k] * n_page
    grid_spec = pltpu.PrefetchScalarGridSpec(
        num_scalar_prefetch=1 if n_page else 0,
        grid=(nb, A_HEADS),
        in_specs=in_specs,
        out_specs=out_specs,
        scratch_shapes=[pltpu.VMEM((seq, 2 * A_DH), BF16),
                        pltpu.VMEM((seq, 2 * A_DH), BF16),
                        pltpu.VMEM((nblk, A_DH), F32),
                        pltpu.VMEM((MOBA_SLOTS, MOBA_BLOCK, seq), F32),
                        pltpu.VMEM((MOBA_SLOTS, MOBA_BLOCK, seq), BF16)],
    )
    out = pl.pallas_call(
        functools.partial(_moba_kernel, n_page),
        grid_spec=grid_spec,
        out_shape=out_shape,
        compiler_params=_params(("parallel", "parallel")),
        name="moba",
    )(*args)
    return (out[0], out[1].reshape(-1, A_HEADS, A_DH)) if n_page else out[0]


def _route(logits):
    lane = lax.broadcasted_iota(jnp.int32, logits.shape, 1)
    ninf = -jnp.inf
    big = jnp.int32(ROUTER_LANES)
    glog = jnp.where(lane < N_GROUPS, logits, ninf)
    gmax = jnp.max(glog, axis=-1, keepdims=True)
    g_idx = jnp.min(jnp.where(glog == gmax, lane, big), axis=-1, keepdims=True)
    g_w = 1.0 / jnp.sum(jnp.exp(glog - gmax), axis=-1, keepdims=True)
    e_lo = EXPERT_LANE0 + g_idx * EXPERTS_PER_GROUP
    in_group = (lane >= e_lo) & (lane < e_lo + EXPERTS_PER_GROUP)
    e1 = jnp.where(in_group, logits, ninf)
    v1 = jnp.max(e1, axis=-1, keepdims=True)
    i1 = jnp.min(jnp.where(e1 == v1, lane, big), axis=-1, keepdims=True)
    e2 = jnp.where(lane == i1, ninf, e1)
    v2 = jnp.max(e2, axis=-1, keepdims=True)
    i2 = jnp.min(jnp.where(e2 == v2, lane, big), axis=-1, keepdims=True)
    x2 = jnp.exp(v2 - v1)
    den = 1.0 + x2
    comb = jnp.where(lane == i1, g_w * (1.0 / den), jnp.where(lane == i2, g_w * (x2 / den), 0.0))
    return jnp.where(lane == GROUP_ID_LANE, g_idx.astype(F32), comb)


def _outproj_kernel(precise, ret_ref, att_ref, gr_ref, ga_ref, x_ref, wr_ref, wa_ref, wo_ref,
                    gf_ref, wrt_ref, brt_ref, y1_ref, h2_ref, comb_ref):
    prec = HI if precise else None
    mm = F32 if precise else BF16
    r = jnp.dot(ret_ref[...].astype(mm), wr_ref[...], precision=prec, preferred_element_type=F32)
    a = jnp.dot(att_ref[...].astype(mm), wa_ref[...], precision=prec, preferred_element_type=F32)
    merged = jax.nn.sigmoid(gr_ref[...]) * r + jax.nn.sigmoid(ga_ref[...]) * a
    m = jnp.dot(merged.astype(mm), wo_ref[...], precision=prec, preferred_element_type=F32)
    y1 = x_ref[...] + m
    y1_ref[...] = y1
    h2 = _rmsnorm(y1, gf_ref[...])
    h2_ref[...] = h2.astype(h2_ref.dtype)
    logits = jnp.dot(h2.astype(mm), wrt_ref[...], precision=prec, preferred_element_type=F32) + brt_ref[...]
    comb_ref[...] = _route(logits)


def _outproj(ret, att, gr, ga, x, w_ret_o, w_att_o, w_o, g_ffn, w_router, b_router, precise, tm):
    t = x.shape[0]
    row = lambda i: (i, 0)
    full = lambda i: (0, 0)
    wspec = lambda: pl.BlockSpec((D_MODEL, D_MODEL), full, pipeline_mode=pl.Buffered(1))
    return pl.pallas_call(
        functools.partial(_outproj_kernel, precise),
        grid=(t // tm,),
        in_specs=[pl.BlockSpec((tm, D_MODEL), row)] * 5 + [wspec(), wspec(), wspec(),
                  pl.BlockSpec((1, D_MODEL), full),
                  pl.BlockSpec(w_router.shape, full),
                  pl.BlockSpec((1, ROUTER_LANES), full)],
        out_specs=[pl.BlockSpec((tm, D_MODEL), row), pl.BlockSpec((tm, D_MODEL), row),
                   pl.BlockSpec((tm, ROUTER_LANES), row)],
        out_shape=[jax.ShapeDtypeStruct((t, D_MODEL), F32), jax.ShapeDtypeStruct((t, D_MODEL), BF16),
                   jax.ShapeDtypeStruct((t, ROUTER_LANES), F32)],
        compiler_params=_params(("parallel",)),
        name="outproj_precise" if precise else "outproj",
    )(ret, att, gr, ga, x, w_ret_o, w_att_o, w_o, g_ffn, w_router, b_router)


def _sum_pages(pages, out_block):
    per_blk = MOBA_BLOCK // PAGE_SIZE
    for n in range(len(pages) // per_blk):
        acc = jnp.sum(pages[n * per_blk][0], axis=0)
        for p in range(1, per_blk):
            acc = acc + jnp.sum(pages[n * per_blk + p][0], axis=0)
        out_block[n] = acc


def _moe_kernel(epb, h_ref, comb_ref, y1_ref, wg_ref, wu_ref, wd_ref, gf_ref, o_ref, acc_ref):
    e = pl.program_id(1)

    @pl.when(e == 0)
    def _():
        acc_ref[...] = jnp.zeros_like(acc_ref)

    h = h_ref[...]
    comb = comb_ref[...]
    lane = lax.broadcasted_iota(jnp.int32, comb.shape, 1)
    for j in range(epb):
        c = jnp.sum(jnp.where(lane == EXPERT_LANE0 + e * epb + j, comb, 0.0), axis=-1, keepdims=True)
        a = jnp.dot(h, wg_ref[j], preferred_element_type=F32)
        u = jnp.dot(h, wu_ref[j], preferred_element_type=F32)
        act = (a * jax.nn.sigmoid(a)) * u * c
        acc_ref[...] += jnp.dot(act.astype(BF16), wd_ref[j], preferred_element_type=F32)

    @pl.when(e == pl.num_programs(1) - 1)
    def _():
        o_ref[...] = _rmsnorm(y1_ref[...] + acc_ref[...], gf_ref[...])


def _moe(h2, comb, y1, wg, wu, wd, g_final, tm, epb=4):
    t = h2.shape[0]
    row = lambda i, e: (i, 0)
    wblk = lambda i, e: (e, 0, 0)
    return pl.pallas_call(
        functools.partial(_moe_kernel, epb),
        grid=(t // tm, N_EXPERTS // epb),
        in_specs=[pl.BlockSpec((tm, D_MODEL), row), pl.BlockSpec((tm, ROUTER_LANES), row),
                  pl.BlockSpec((tm, D_MODEL), row),
                  pl.BlockSpec((epb, D_MODEL, D_EXPERT), wblk),
                  pl.BlockSpec((epb, D_MODEL, D_EXPERT), wblk),
                  pl.BlockSpec((epb, D_EXPERT, D_MODEL), wblk),
                  pl.BlockSpec((1, D_MODEL), lambda i, e: (0, 0))],
        out_specs=pl.BlockSpec((tm, D_MODEL), row),
        out_shape=jax.ShapeDtypeStruct((t, D_MODEL), F32),
        scratch_shapes=[pltpu.VMEM((tm, D_MODEL), F32)],
        compiler_params=_params(("parallel", "arbitrary")),
        name="moe",
    )(h2, comb, y1, wg, wu, wd, g_final)


SORT_TILE = 128
EXPERT_SPLIT = 2
COMB_LANES = 40
assert EXPERT_LANE0 + N_EXPERTS <= COMB_LANES and 3 * COMB_LANES <= LANES


def _moe_sorted_kernel(n_page, *refs):
    if n_page:
        refs = refs[1:]
    h_ref, comb_ref, y1_ref, wg_ref, wu_ref, wd_ref, gf_ref, tri_ref = refs[:8]
    pages = refs[8:8 + n_page]
    rest = refs[8 + n_page:]
    if n_page:
        o_ref, ks_ref = rest[:2]
        _sum_pages(pages, ks_ref.at[0])
        rest = rest[2:]
    else:
        o_ref, rest = rest[0], rest[1:]
    xs_scr, cs_scr, out_scr, dest_scr, run_smem = rest
    g = pl.program_id(1)
    part = pl.program_id(2)
    tb = h_ref.shape[0]
    cap = xs_scr.shape[0]
    epb = wg_ref.shape[0]

    @pl.when((g == 0) & (part == 0))
    def _sort_block():
        comb = comb_ref[...]
        lane = lax.broadcasted_iota(jnp.int32, comb.shape, 1)
        gcol = comb[:, GROUP_ID_LANE:GROUP_ID_LANE + 1]
        ind = jnp.where((lane < N_GROUPS) & (lane.astype(F32) == gcol), 1.0, 0.0)
        pos = jnp.dot(tri_ref[...], ind.astype(BF16), preferred_element_type=F32)
        dest = jnp.zeros((tb, 1), F32)
        start = jnp.int32(0)
        for gg in range(N_GROUPS):
            col = ind[:, gg:gg + 1]
            n_tiles = (jnp.sum(col).astype(jnp.int32) + SORT_TILE - 1) // SORT_TILE
            run_smem[gg] = start
            run_smem[N_GROUPS + gg] = n_tiles
            dest = dest + col * (pos[:, gg:gg + 1] + start.astype(F32))
            start = start + n_tiles * SORT_TILE
        dest_b = jnp.broadcast_to(dest, (tb, LANES))
        dest_scr[...] = dest_b
        dest_row = dest_b.T[0:1, :].astype(jnp.int32)
        c_hi = comb.astype(BF16).astype(F32)
        r1 = comb - c_hi
        c_mid = r1.astype(BF16).astype(F32)
        c_lo = r1 - c_mid
        packed = (c_hi + pltpu.roll(c_mid, COMB_LANES, 1) + pltpu.roll(c_lo, 2 * COMB_LANES, 1)).astype(BF16)
        lane_t = lax.broadcasted_iota(jnp.int32, (SORT_TILE, ROUTER_LANES), 1)
        h = h_ref[...]
        for c in range(cap // SORT_TILE):
            rows = slice(c * SORT_TILE, (c + 1) * SORT_TILE)

            def sort_rows(c=c, rows=rows):
                riota = lax.broadcasted_iota(jnp.int32, (SORT_TILE, tb), 0) + c * SORT_TILE
                perm = jnp.where(riota == dest_row, 1.0, 0.0).astype(BF16)
                xs_scr[rows, :] = jnp.dot(perm, h, preferred_element_type=F32).astype(BF16)
                pk = jnp.dot(perm, packed, preferred_element_type=F32)
                terms = (pk + pltpu.roll(pk, ROUTER_LANES - COMB_LANES, 1)
                         + pltpu.roll(pk, ROUTER_LANES - 2 * COMB_LANES, 1))
                cs_scr[rows, :] = jnp.where(lane_t < COMB_LANES, terms, 0.0)

            if c * SORT_TILE < tb:
                sort_rows()
            else:
                pl.when(c * SORT_TILE < start)(sort_rows)
        out_scr[...] = jnp.zeros_like(out_scr)

    start = run_smem[g]
    n_tiles = run_smem[N_GROUPS + g]
    lane_c = lax.broadcasted_iota(jnp.int32, (SORT_TILE, ROUTER_LANES), 1)
    lane0 = EXPERT_LANE0 + g * EXPERTS_PER_GROUP + part * epb

    def run_tile(t, carry):
        r0 = pl.multiple_of(start + t * SORT_TILE, SORT_TILE)
        x = xs_scr[pl.ds(r0, SORT_TILE), :]
        cs = cs_scr[pl.ds(r0, SORT_TILE), :]
        acc = jnp.zeros((SORT_TILE, D_MODEL), F32)
        for j in range(epb):
            c = jnp.sum(jnp.where(lane_c == lane0 + j, cs, 0.0), axis=-1, keepdims=True)
            a = jnp.dot(x, wg_ref[j], preferred_element_type=F32)
            u = jnp.dot(x, wu_ref[j], preferred_element_type=F32)
            act = (a * jax.nn.sigmoid(a)) * u * c
            acc = acc + jnp.dot(act.astype(BF16), wd_ref[j], preferred_element_type=F32)
        out_scr[pl.ds(r0, SORT_TILE), :] += acc
        return carry

    lax.fori_loop(0, n_tiles, run_tile, 0)

    @pl.when((g == pl.num_programs(1) - 1) & (part == pl.num_programs(2) - 1))
    def _unsort_block():
        chunk = 2 * SORT_TILE
        lane_r = lax.broadcasted_iota(jnp.int32, (chunk, cap), 1)
        xs_scr[...] = out_scr[...].astype(BF16)
        for c in range(tb // chunk):
            rows = slice(c * chunk, (c + 1) * chunk)
            perm_t = jnp.where(lane_r == dest_scr[rows, 0:1].astype(jnp.int32), 1.0, 0.0).astype(BF16)
            moe = jnp.dot(perm_t, xs_scr[...], preferred_element_type=F32)
            o_ref[rows, :] = _rmsnorm(y1_ref[rows, :] + moe, gf_ref[...])


def _rider_plan(steps, max_pages, n_seq, pages_per_seq):
    per_blk = MOBA_BLOCK // PAGE_SIZE
    n_page = min(max_pages, n_seq * pages_per_seq // steps) // per_blk * per_blk
    if n_page == 0 or pages_per_seq % n_page or (steps * n_page) % pages_per_seq:
        return 0, 0
    return n_page, steps * n_page // pages_per_seq


def _moe_sorted(h2, comb, y1, wg, wu, wd, g_final, tb, cache_k=None, pt_flat=None, n_page=0):
    t = h2.shape[0]
    epb = EXPERTS_PER_GROUP // EXPERT_SPLIT
    cap = tb + N_GROUPS * SORT_TILE
    grid = (t // tb, N_GROUPS, EXPERT_SPLIT)
    row = lambda b, g, p, *_: (b, 0)
    wblk = lambda b, g, p, *_: (g * EXPERT_SPLIT + p, 0, 0)
    const = lambda b, g, p, *_: (0, 0)
    tri = jnp.tril(jnp.ones((tb, tb), BF16), -1)
    in_specs = [pl.BlockSpec((tb, D_MODEL), row, pipeline_mode=pl.Buffered(1)),
                pl.BlockSpec((tb, ROUTER_LANES), row, pipeline_mode=pl.Buffered(1)),
                pl.BlockSpec((tb, D_MODEL), row, pipeline_mode=pl.Buffered(1)),
                pl.BlockSpec((epb, D_MODEL, D_EXPERT), wblk),
                pl.BlockSpec((epb, D_MODEL, D_EXPERT), wblk),
                pl.BlockSpec((epb, D_EXPERT, D_MODEL), wblk),
                pl.BlockSpec((1, D_MODEL), const),
                pl.BlockSpec((tb, tb), const, pipeline_mode=pl.Buffered(1))]
    out_specs = [pl.BlockSpec((tb, D_MODEL), row)]
    out_shape = [jax.ShapeDtypeStruct((t, D_MODEL), F32)]
    args = [h2, comb, y1, wg, wu, wd, g_final, tri]
    if n_page:
        blk_per_step = n_page * PAGE_SIZE // MOBA_BLOCK
        steps = grid[0] * grid[1] * grid[2]
        step = lambda b, g, p: (b * N_GROUPS + g) * EXPERT_SPLIT + p
        for q in range(n_page):
            in_specs.append(pl.BlockSpec((1, PAGE_SIZE, A_HEADS, A_DH),
                                         lambda b, g, p, pt, q=q: (pt[step(b, g, p) * n_page + q], 0, 0, 0)))
        out_specs.append(pl.BlockSpec((1, blk_per_step, A_HEADS, A_DH), lambda b, g, p, pt: (step(b, g, p), 0, 0, 0)))
        out_shape.append(jax.ShapeDtypeStruct((steps, blk_per_step, A_HEADS, A_DH), F32))
        args = [pt_flat] + args + [cache_k] * n_page
    grid_spec = pltpu.PrefetchScalarGridSpec(
        num_scalar_prefetch=1 if n_page else 0,
        grid=grid,
        in_specs=in_specs,
        out_specs=out_specs,
        scratch_shapes=[pltpu.VMEM((cap, D_MODEL), BF16),
                        pltpu.VMEM((cap, ROUTER_LANES), F32),
                        pltpu.VMEM((cap, D_MODEL), F32),
                        pltpu.VMEM((tb, LANES), F32),
                        pltpu.SMEM((2 * N_GROUPS,), jnp.int32)],
    )
    out = pl.pallas_call(
        functools.partial(_moe_sorted_kernel, n_page),
        grid_spec=grid_spec,
        out_shape=out_shape,
        compiler_params=_params(("parallel", "arbitrary", "arbitrary")),
        name="moe_sorted",
    )(*args)
    return (out[0], out[1].reshape(-1, A_HEADS, A_DH)) if n_page else out[0]


def _inproj_sample_kernel(x_ref, g_ref, w_ref, ta_ref, tb_ref, o_ref, wbf_ref):
    w = w_ref[...]
    w_hi = w.astype(BF16)
    wbf_ref[...] = w_hi
    w_lo = (w - w_hi.astype(F32)).astype(BF16)
    h = _rmsnorm(x_ref[...], g_ref[...])
    h_hi = h.astype(BF16)
    h_lo = (h - h_hi.astype(F32)).astype(BF16)
    z = (jnp.dot(h_hi, w_hi, preferred_element_type=F32) + jnp.dot(h_hi, w_lo, preferred_element_type=F32)
         + jnp.dot(h_lo, w_hi, preferred_element_type=F32))
    for c in range(0, z.shape[1], LANES):
        zh = z[:, c:c + LANES]
        o_ref[:, c:c + LANES] = zh * ta_ref[:, c:c + LANES] + pltpu.roll(zh, A_DH // 2, 1) * tb_ref[:, c:c + LANES]


def _inproj_sample(x, g, w_in, ta, tb, tn=512):
    rows = x.shape[0]
    return pl.pallas_call(
        _inproj_sample_kernel,
        grid=(D_IN // tn,),
        in_specs=[pl.BlockSpec((rows, D_MODEL), lambda j: (0, 0)),
                  pl.BlockSpec((1, D_MODEL), lambda j: (0, 0)),
                  pl.BlockSpec((D_MODEL, tn), lambda j: (0, j)),
                  pl.BlockSpec((rows, tn), lambda j: (0, j)),
                  pl.BlockSpec((rows, tn), lambda j: (0, j))],
        out_specs=[pl.BlockSpec((rows, tn), lambda j: (0, j)), pl.BlockSpec((D_MODEL, tn), lambda j: (0, j))],
        out_shape=[jax.ShapeDtypeStruct((rows, D_IN), F32), jax.ShapeDtypeStruct((D_MODEL, D_IN), BF16)],
        compiler_params=_params(("parallel",)),
        name="inproj_sample",
    )(x, g, w_in, ta, tb)


def _sample_tables(rows, pos):
    half = A_DH // 2
    inv = ROPE_THETA ** (-jnp.arange(half, dtype=F32) / half)
    ang = jnp.full((rows, 1), pos, F32) * inv[None, :]
    cs = jnp.concatenate([jnp.cos(ang), jnp.cos(ang)], axis=-1)
    sn = jnp.concatenate([-jnp.sin(ang), jnp.sin(ang)], axis=-1)
    ones = jnp.ones((rows, LANES), F32)
    zeros = jnp.zeros((rows, LANES), F32)
    ta, tb = [], []
    for off, width, kind in ((OFF_RQ, R_QK_W, R_DK ** -0.5), (OFF_RK, R_QK_W, 1.0), (OFF_RV, 2 * R_V_W, None),
                             (OFF_AQ, A_W, A_DH ** -0.5), (OFF_AK, A_W, 1.0), (OFF_AV, A_W + 2 * D_MODEL, None)):
        for _ in range(width // LANES):
            ta.append(ones if kind is None else cs * kind)
            tb.append(zeros if kind is None else sn * kind)
    return jnp.concatenate(ta, axis=-1), jnp.concatenate(tb, axis=-1)


def _retention_step_kernel(q_ref, k_ref, v_ref, g_ref, s_ref, dec_ref, o_ref, sn_ref):
    for i in range(q_ref.shape[0]):
        for h in range(R_HEADS):
            col = lambda r: jnp.broadcast_to(r[i, h:h + 1, :], (R_DK, R_DK)).T[:, 0:1]
            q = col(q_ref)
            k = col(k_ref)
            v = v_ref[i, h]
            s = s_ref[i, h]
            dec = dec_ref[h]
            cross = jnp.sum(q * s, axis=0, keepdims=True) * dec
            inner = jnp.sum(q * k, axis=0, keepdims=True) * v
            sn_ref[i, h] = s * dec + k * v
            o = inner + cross
            mu = jnp.mean(o, axis=-1, keepdims=True)
            d = o - mu
            var = jnp.mean(d * d, axis=-1, keepdims=True)
            gate = g_ref[i, h]
            o_ref[i, h] = gate * jax.nn.sigmoid(gate) * (d * lax.rsqrt(var + EPS))


def _retention_step(q, k, v, g, state, dec):
    nb = q.shape[0]
    per = SAMPLE_SEQS_PER_STEP if nb % SAMPLE_SEQS_PER_STEP == 0 else 1
    col = pl.BlockSpec((per, R_HEADS, R_DK), lambda b: (b, 0, 0))
    rowv = pl.BlockSpec((per, R_HEADS, 1, R_DV), lambda b: (b, 0, 0, 0))
    st = pl.BlockSpec((per, R_HEADS, R_DK, R_DV), lambda b: (b, 0, 0, 0))
    return pl.pallas_call(
        _retention_step_kernel,
        grid=(nb // per,),
        in_specs=[col, col, rowv, rowv, st, pl.BlockSpec((R_HEADS, 1, R_DV), lambda b: (0, 0, 0))],
        out_specs=[rowv, st],
        out_shape=[jax.ShapeDtypeStruct((nb, R_HEADS, 1, R_DV), F32),
                   jax.ShapeDtypeStruct((nb, R_HEADS, R_DK, R_DV), F32)],
        compiler_params=_params(("parallel",)),
        name="retention_step",
    )(q, k, v, g, state, dec)


def _block_sum_kernel(pt_ref, *refs):
    _sum_pages(refs[:-1], refs[-1].at[0, 0])


def _block_sums(cache_k, pt_flat, nb, n_pages):
    steps = n_pages // PAGES_PER_STEP
    blk_per_step = PAGES_PER_STEP * PAGE_SIZE // MOBA_BLOCK

    def page_spec(p):
        return pl.BlockSpec((1, PAGE_SIZE, A_HEADS, A_DH),
                            lambda b, j, pt: (pt[b * n_pages + j * PAGES_PER_STEP + p], 0, 0, 0))

    grid_spec = pltpu.PrefetchScalarGridSpec(
        num_scalar_prefetch=1,
        grid=(nb, steps),
        in_specs=[page_spec(p) for p in range(PAGES_PER_STEP)],
        out_specs=pl.BlockSpec((1, 1, blk_per_step, A_HEADS, A_DH), lambda b, j, pt: (b, j, 0, 0, 0)),
    )
    out = pl.pallas_call(
        _block_sum_kernel,
        grid_spec=grid_spec,
        out_shape=jax.ShapeDtypeStruct((nb, steps, blk_per_step, A_HEADS, A_DH), F32),
        compiler_params=_params(("parallel", "parallel")),
        name="cache_block_sums",
    )(pt_flat, *([cache_k] * PAGES_PER_STEP))
    return out.reshape(nb, steps * blk_per_step, A_W)


def _block_pick_kernel(q_ref, ks_ref, o_ref):
    for i in range(q_ref.shape[0]):
        ks = ks_ref[i] * (1.0 / MOBA_BLOCK)
        prod = ks * q_ref[i]
        nblk = prod.shape[0]
        lane = lax.broadcasted_iota(jnp.int32, (nblk, LANES), 1)
        gate = jnp.full((nblk, LANES), -jnp.inf, F32)
        for h in range(A_HEADS):
            gate = jnp.where(lane == h, jnp.sum(prod[:, h * A_DH:(h + 1) * A_DH], axis=-1, keepdims=True), gate)
        blk = lax.broadcasted_iota(jnp.int32, gate.shape, 0)
        rank = jnp.zeros(gate.shape, jnp.int32)
        for m in range(nblk):
            gm = gate[m:m + 1, :]
            ahead = (gm > gate) | ((gm == gate) & (m < blk))
            rank = rank + ahead.astype(jnp.int32)
        for j in range(MOBA_TOPK):
            o_ref[i, j:j + 1, :] = jnp.sum(jnp.where(rank == j, blk, 0), axis=0, keepdims=True)


def _block_pick(q, ksum):
    nb, nblk, _ = ksum.shape
    per = SAMPLE_SEQS_PER_STEP if nb % SAMPLE_SEQS_PER_STEP == 0 else 1
    return pl.pallas_call(
        _block_pick_kernel,
        grid=(nb // per,),
        in_specs=[pl.BlockSpec((per, 1, A_W), lambda b: (b, 0, 0)),
                  pl.BlockSpec((per, nblk, A_W), lambda b: (b, 0, 0))],
        out_specs=pl.BlockSpec((per, MOBA_TOPK, LANES), lambda b: (b, 0, 0)),
        out_shape=jax.ShapeDtypeStruct((nb, MOBA_TOPK, LANES), jnp.int32),
        compiler_params=_params(("parallel",)),
        name="block_pick",
    )(q, ksum)


PICK_PAGES = MOBA_TOPK * (MOBA_BLOCK // PAGE_SIZE)


def _attend_sample_kernel(pages_per_seq, pt_ref, idx_ref, q_ref, kn_ref, vn_ref, ck_ref, cv_ref, o_ref,
                          kbuf, vbuf, ksem, vsem):
    b = pl.program_id(0)
    per_blk = MOBA_BLOCK // PAGE_SIZE

    def page_copies(seq_i, slot):
        out = []
        for h in range(A_HEADS):
            for j in range(MOBA_TOPK):
                blk = idx_ref[(seq_i * MOBA_TOPK + j) * A_HEADS + h]
                for p in range(per_blk):
                    page = pt_ref[seq_i * pages_per_seq + blk * per_blk + p]
                    dst = (slot, h, j * per_blk + p)
                    out.append(pltpu.make_async_copy(ck_ref.at[page, :, h, :], kbuf.at[dst], ksem.at[slot]))
                    out.append(pltpu.make_async_copy(cv_ref.at[page, :, h, :], vbuf.at[dst], vsem.at[slot]))
        return out

    slot = b % 2

    @pl.when(b == 0)
    def _():
        for c in page_copies(0, 0):
            c.start()

    @pl.when(b + 1 < pl.num_programs(0))
    def _():
        for c in page_copies(b + 1, 1 - slot):
            c.start()

    for c in page_copies(b, slot):
        c.wait()

    for h in range(A_HEADS):
        q = q_ref[0, h:h + 1, :]
        s_own = jnp.sum(q * kn_ref[0, h:h + 1, :], axis=-1, keepdims=True)
        scores = [jnp.sum(kbuf[slot, h, g] * q, axis=-1, keepdims=True) for g in range(PICK_PAGES)]
        m = s_own
        for s in scores:
            m = jnp.maximum(m, jnp.max(s, axis=0, keepdims=True))
        p_own = jnp.exp(s_own - m)
        l = p_own
        acc = p_own * vn_ref[0, h:h + 1, :]
        for g, s in enumerate(scores):
            p = jnp.exp(s - m)
            l = l + jnp.sum(p, axis=0, keepdims=True)
            acc = acc + jnp.sum(p * vbuf[slot, h, g], axis=0, keepdims=True)
        o_ref[0, h:h + 1, :] = acc / l


def _attend_sample(q, k_new, v_new, cache_k, cache_v, pt_flat, picks_flat, pages_per_seq):
    nb = q.shape[0]
    vec = pl.BlockSpec((1, A_HEADS, A_DH), lambda b, pt, idx: (b, 0, 0))
    hbm = pl.BlockSpec(memory_space=pl.ANY)
    buf = pltpu.VMEM((2, A_HEADS, PICK_PAGES, PAGE_SIZE, A_DH), F32)
    grid_spec = pltpu.PrefetchScalarGridSpec(
        num_scalar_prefetch=2,
        grid=(nb,),
        in_specs=[vec, vec, vec, hbm, hbm],
        out_specs=vec,
        scratch_shapes=[buf, buf, pltpu.SemaphoreType.DMA((2,)), pltpu.SemaphoreType.DMA((2,))],
    )
    return pl.pallas_call(
        functools.partial(_attend_sample_kernel, pages_per_seq),
        grid_spec=grid_spec,
        out_shape=jax.ShapeDtypeStruct((nb, A_HEADS, A_DH), F32),
        compiler_params=_params(("arbitrary",)),
        name="attend_sample",
    )(pt_flat, picks_flat, q, k_new, v_new, cache_k, cache_v)


def _rope_tables(seq):
    half = A_DH // 2
    inv = ROPE_THETA ** (-jnp.arange(half, dtype=F32) / half)
    ang = jnp.arange(seq, dtype=jnp.int32).astype(F32)[:, None] * inv[None, :]
    cos, sin = jnp.cos(ang), jnp.sin(ang)
    return jnp.concatenate([cos, cos], axis=-1), jnp.concatenate([-sin, sin], axis=-1)


def _router_weights(w_rg, b_rg, w_re, b_re):
    pad = ROUTER_LANES - N_GROUPS - N_EXPERTS
    w = jnp.concatenate([w_rg, w_re.reshape(D_MODEL, N_EXPERTS), jnp.zeros((D_MODEL, pad), F32)], axis=-1)
    b = jnp.concatenate([b_rg, b_re.reshape(N_EXPERTS), jnp.zeros((pad,), F32)])[None, :]
    return w, b


def kernel(x_prompt, x_sample, cache_k, cache_v, state_ret, page_table, norm_mix, w_in, w_ret_o, w_att_o, w_o,
           norm_ffn, w_router_group, b_router_group, w_router_expert, b_router_expert, w_gate, w_up, w_down,
           norm_final):
    nb, seq, _ = x_prompt.shape
    db, ds, _ = x_sample.shape
    depth = norm_mix.shape[0]
    past_len = page_table.shape[1] * PAGE_SIZE
    assert depth == 1 and ds == 1
    assert seq % MOBA_BLOCK == 0 and past_len % (PAGES_PER_STEP * PAGE_SIZE) == 0
    assert past_len // MOBA_BLOCK >= MOBA_TOPK
    l = 0
    t = nb * seq

    g_mix = norm_mix[l][None, :]
    g_ffn = norm_ffn[l][None, :]
    g_fin = norm_final[None, :]
    w_router, b_router = _router_weights(w_router_group[l], b_router_group[l], w_router_expert[l], b_router_expert[l])
    xs = x_sample.reshape(db, D_MODEL)
    ta, tb = _sample_tables(db, past_len)
    zs, w_in_bf = _inproj_sample(xs, g_mix, w_in[l], ta, tb)
    wro_bf, wao_bf, wo_bf = w_ret_o[l].astype(BF16), w_att_o[l].astype(BF16), w_o[l].astype(BF16)

    xp = x_prompt.reshape(t, D_MODEL)
    cs, sn = _rope_tables(seq)
    moe_w = [w_gate[l].reshape(-1, D_EXPERT), w_up[l].reshape(-1, D_EXPERT), w_down[l].reshape(-1, D_MODEL)]
    ride_casts = all(_castable_in_steps(a, t // INPROJ_TM) for a in moe_w)
    proj = _inproj(xp, g_mix, w_in_bf, cs, sn, seq, cast_f32=moe_w if ride_casts else (), tm=INPROJ_TM)
    rq, rk, rv, rg, aq, kp, vp, gr, ga = proj[:9]
    moe_w_bf = proj[9:] if ride_casts else [a.astype(BF16) for a in moe_w]
    wg_bf, wu_bf, wd_bf = (a.reshape(s.shape[1:]) for a, s in zip(moe_w_bf, (w_gate, w_up, w_down)))
    ret, ret_state = _retention(rq, rk, rv, rg, _retention_tables(), nb, seq)
    ck, cv = cache_k[l], cache_v[l]
    n_pages = page_table.shape[1]
    pt_flat = page_table.reshape(-1)
    blocks_per_seq = n_pages * PAGE_SIZE // MOBA_BLOCK
    moe_tb = MOE_BLOCK if t % MOE_BLOCK == 0 else MOE_BLOCK // 2
    moe_steps = (t // moe_tb) * N_GROUPS * EXPERT_SPLIT
    moe_pages, moe_seqs = _rider_plan(moe_steps, MOE_RIDER_PAGES, db, n_pages)
    moba_pages, moba_seqs = _rider_plan(nb * A_HEADS, MOBA_RIDER_PAGES, db - moe_seqs, n_pages)
    ksum_parts = []

    moba_out = _moba(aq, kp, vp, nb, seq, cache_k=ck, pt_flat=pt_flat[moe_seqs * n_pages:], n_page=moba_pages)
    att, ks_moba = moba_out if moba_pages else (moba_out, None)
    y1, h2, comb = _outproj(ret, att, gr, ga, xp, wro_bf, wao_bf, wo_bf, g_ffn, w_router.astype(BF16), b_router,
                            precise=False, tm=OUTPROJ_TM)
    moe_out = _moe_sorted(h2, comb, y1, wg_bf, wu_bf, wd_bf, g_fin, moe_tb,
                          cache_k=ck, pt_flat=pt_flat, n_page=moe_pages)
    y_prompt, ks_moe = moe_out if moe_pages else (moe_out, None)
    y_prompt = y_prompt.reshape(nb, seq, D_MODEL)
    if moe_pages:
        ksum_parts.append(ks_moe.reshape(moe_seqs, blocks_per_seq, A_W))
    if moba_pages:
        ksum_parts.append(ks_moba.reshape(moba_seqs, blocks_per_seq, A_W))
    done_seqs = moe_seqs + moba_seqs
    if done_seqs < db:
        ksum_parts.append(_block_sums(ck, pt_flat[done_seqs * n_pages:], db - done_seqs, n_pages))
    ksum = jnp.concatenate(ksum_parts, axis=0) if len(ksum_parts) > 1 else ksum_parts[0]

    seg = lambda off, width: zs[:, off:off + width]
    ld = jnp.log(1.0 - 2.0 ** (-5.0 - jnp.arange(R_HEADS, dtype=F32)))
    dec = jnp.broadcast_to(jnp.exp(ld * 1.0)[:, None, None], (R_HEADS, 1, R_DV))
    ret_s, state_s = _retention_step(
        seg(OFF_RQ, R_QK_W).reshape(db, R_HEADS, R_DK), seg(OFF_RK, R_QK_W).reshape(db, R_HEADS, R_DK),
        seg(OFF_RV, R_V_W).reshape(db, R_HEADS, 1, R_DV), seg(OFF_RG, R_V_W).reshape(db, R_HEADS, 1, R_DV),
        state_ret[l].astype(F32), dec)
    aq_s, ak_s, av_s = seg(OFF_AQ, A_W), seg(OFF_AK, A_W), seg(OFF_AV, A_W)
    picks = _block_pick(aq_s.reshape(db, 1, A_W), ksum)[:, :, :A_HEADS].reshape(-1)
    heads = lambda z: z.reshape(db, A_HEADS, A_DH)
    att_s = _attend_sample(heads(aq_s), heads(ak_s), heads(av_s), ck, cv, pt_flat, picks, n_pages)
    y1_s, h2_s, comb_s = _outproj(ret_s.reshape(db, R_V_W), att_s.reshape(db, A_W), seg(OFF_GR, D_MODEL),
                                  seg(OFF_GA, D_MODEL), xs, w_ret_o[l], w_att_o[l], w_o[l], g_ffn,
                                  w_router, b_router, precise=True, tm=db)
    y_sample = _moe(h2_s, comb_s, y1_s, wg_bf, wu_bf, wd_bf, g_fin, tm=db).reshape(db, ds, D_MODEL)

    k_prompt = kp.reshape(1, nb, seq, A_HEADS, A_DH)
    v_prompt = vp.reshape(1, nb, seq, A_HEADS, A_DH)
    ret_prompt = ret_state[None]
    k_sample = ak_s.reshape(1, db, ds, A_HEADS, A_DH).astype(cache_k.dtype)
    v_sample = av_s.reshape(1, db, ds, A_HEADS, A_DH).astype(cache_v.dtype)
    ret_sample = state_s[None].astype(state_ret.dtype)
    return (y_prompt, y_sample, k_prompt, v_prompt, ret_prompt, k_sample, v_sample, ret_sample)
```

```python
import functools

import jax
import jax.numpy as jnp
from jax import lax
from jax.experimental import pallas as pl
from jax.experimental.pallas import tpu as pltpu

D_MODEL = 1024
PAGE_SIZE = 128
R_HEADS = 4
R_DK = 128
R_DV = 256
R_CHUNK = 128
A_HEADS = 8
A_DH = 128
MOBA_BLOCK = 256
MOBA_TOPK = 3
ROPE_THETA = 10000.0
N_GROUPS = 4
EXPERTS_PER_GROUP = 8
N_EXPERTS = N_GROUPS * EXPERTS_PER_GROUP
D_EXPERT = 256
EPS = 1e-6

R_QK_W = R_HEADS * R_DK
R_V_W = R_HEADS * R_DV
A_W = A_HEADS * A_DH
D_IN = 2 * R_QK_W + 2 * R_V_W + 3 * A_W + 2 * D_MODEL
OFF_RQ = 0
OFF_RK = OFF_RQ + R_QK_W
OFF_RV = OFF_RK + R_QK_W
OFF_RG = OFF_RV + R_V_W
OFF_AQ = OFF_RG + R_V_W
OFF_AK = OFF_AQ + A_W
OFF_AV = OFF_AK + A_W
OFF_GR = OFF_AV + A_W
OFF_GA = OFF_GR + D_MODEL

LANES = 128
ROUTER_LANES = LANES
EXPERT_LANE0 = N_GROUPS
GROUP_ID_LANE = 0
PAGES_PER_STEP = 8
INPROJ_TM = 256
OUTPROJ_TM = 512
MOE_BLOCK = 1024
MOE_RIDER_PAGES = 0
MOBA_RIDER_PAGES = 32
SAMPLE_SEQS_PER_STEP = 4
VMEM_LIMIT = 56 * 1024 * 1024

F32 = jnp.float32
BF16 = jnp.bfloat16
HI = lax.Precision.HIGHEST
NT = (((1,), (1,)), ((), ()))


def _params(sem, vmem=VMEM_LIMIT):
    return pltpu.CompilerParams(dimension_semantics=sem, vmem_limit_bytes=vmem)


def _rmsnorm(x, g):
    return x * lax.rsqrt(jnp.mean(x * x, axis=-1, keepdims=True) + EPS) * g


def _rope_head(z, cs, sn):
    return z * cs + pltpu.roll(z, A_DH // 2, 1) * sn


def _inproj_kernel(n_cast, x_ref, g_ref, w_ref, cs_ref, sn_ref, *refs):
    cast_in, refs = refs[:n_cast], refs[n_cast:]
    rq_ref, rk_ref, rv_ref, rg_ref, aq_ref, k_ref, v_ref, gr_ref, ga_ref = refs[:9]
    for src, dst in zip(cast_in, refs[9:]):
        dst[...] = src[...].astype(BF16)
    h = _rmsnorm(x_ref[...], g_ref[...]).astype(BF16)
    cs = cs_ref[...]
    sn = sn_ref[...]

    def z_cols(c0):
        return jnp.dot(h, w_ref[:, c0:c0 + 2 * LANES], preferred_element_type=F32)

    def plain(off, width, out_ref):
        for c in range(0, width, 2 * LANES):
            out_ref[:, c:c + 2 * LANES] = z_cols(off + c).astype(out_ref.dtype)

    def roped(off, width, out_ref, scale):
        for c in range(0, width, 2 * LANES):
            z = z_cols(off + c)
            for hh in range(2):
                r = _rope_head(z[:, hh * LANES:(hh + 1) * LANES], cs, sn)
                if scale is not None:
                    r = r * scale
                out_ref[:, c + hh * LANES:c + (hh + 1) * LANES] = r.astype(out_ref.dtype)

    roped(OFF_RQ, R_QK_W, rq_ref, R_DK ** -0.5)
    roped(OFF_RK, R_QK_W, rk_ref, None)
    plain(OFF_RV, R_V_W, rv_ref)
    plain(OFF_RG, R_V_W, rg_ref)
    roped(OFF_AQ, A_W, aq_ref, A_DH ** -0.5)
    roped(OFF_AK, A_W, k_ref, None)
    plain(OFF_AV, A_W, v_ref)
    plain(OFF_GR, D_MODEL, gr_ref)
    plain(OFF_GA, D_MODEL, ga_ref)


def _inproj(x, g, w_bf, cs, sn, seq, cast_f32=(), tm=256):
    t = x.shape[0]
    steps = t // tm
    nseq = seq // tm
    row = lambda i: (i, 0)
    tab = lambda i: (i % nseq, 0)
    outs = [(R_QK_W, BF16), (R_QK_W, BF16), (R_V_W, BF16), (R_V_W, F32), (A_W, BF16),
            (A_W, F32), (A_W, F32), (D_MODEL, F32), (D_MODEL, F32)]
    cast_specs = [pl.BlockSpec((a.shape[0] // steps, a.shape[1]), row) for a in cast_f32]
    return pl.pallas_call(
        functools.partial(_inproj_kernel, len(cast_f32)),
        grid=(steps,),
        in_specs=[pl.BlockSpec((tm, D_MODEL), row),
                  pl.BlockSpec((1, D_MODEL), lambda i: (0, 0)),
                  pl.BlockSpec((D_MODEL, D_IN), lambda i: (0, 0), pipeline_mode=pl.Buffered(1)),
                  pl.BlockSpec((tm, LANES), tab),
                  pl.BlockSpec((tm, LANES), tab)] + cast_specs,
        out_specs=[pl.BlockSpec((tm, w), row) for w, _ in outs] + cast_specs,
        out_shape=([jax.ShapeDtypeStruct((t, w), d) for w, d in outs]
                   + [jax.ShapeDtypeStruct(a.shape, BF16) for a in cast_f32]),
        compiler_params=_params(("parallel",)),
        name="inproj",
    )(x, g, w_bf, cs, sn, *cast_f32)


def _castable_in_steps(a, steps):
    return a.shape[0] % steps == 0 and (a.shape[0] // steps) % 16 == 0


def _retention_kernel(q_ref, k_ref, v_ref, g_ref, dm_ref, cd_ref, kd_ref, sd_ref, o_ref, st_ref, s_scr):
    nchunk = q_ref.shape[0] // R_CHUNK
    s_scr[...] = jnp.zeros_like(s_scr)
    for c in range(nchunk):
        rows = slice(c * R_CHUNK, (c + 1) * R_CHUNK)
        q = q_ref[rows, :]
        k = k_ref[rows, :]
        v = v_ref[rows, :]
        s = s_scr[...]
        scores = lax.dot_general(q, k, NT, preferred_element_type=F32) * dm_ref[0]
        inner = jnp.dot(scores.astype(BF16), v, preferred_element_type=F32)
        cross = jnp.dot(q, s.astype(BF16), preferred_element_type=F32) * cd_ref[0]
        k_dec_t = (k.astype(F32) * kd_ref[0]).T.astype(BF16)
        s_scr[...] = s * sd_ref[0] + jnp.dot(k_dec_t, v, preferred_element_type=F32)
        o = inner + cross
        mu = jnp.mean(o, axis=-1, keepdims=True)
        d = o - mu
        var = jnp.mean(d * d, axis=-1, keepdims=True)
        y = d * lax.rsqrt(var + EPS)
        gate = g_ref[rows, :]
        o_ref[rows, :] = (gate * jax.nn.sigmoid(gate) * y).astype(o_ref.dtype)
    st_ref[0, 0] = s_scr[...]


def _retention(rq, rk, rv, rg, tabs, nb, seq):
    dm, cd, kd, sd = tabs
    head_tab = lambda shp: pl.BlockSpec((1,) + shp, lambda b, h: (h, 0, 0))
    return pl.pallas_call(
        _retention_kernel,
        grid=(nb, R_HEADS),
        in_specs=[pl.BlockSpec((seq, R_DK), lambda b, h: (b, h)),
                  pl.BlockSpec((seq, R_DK), lambda b, h: (b, h)),
                  pl.BlockSpec((seq, R_DV), lambda b, h: (b, h)),
                  pl.BlockSpec((seq, R_DV), lambda b, h: (b, h)),
                  head_tab((R_CHUNK, R_CHUNK)), head_tab((R_CHUNK, R_DV)),
                  head_tab((R_CHUNK, R_DK)), head_tab((R_DK, R_DV))],
        out_specs=[pl.BlockSpec((seq, R_DV), lambda b, h: (b, h)),
                   pl.BlockSpec((1, 1, R_DK, R_DV), lambda b, h: (b, h, 0, 0))],
        out_shape=[jax.ShapeDtypeStruct((nb * seq, R_V_W), BF16),
                   jax.ShapeDtypeStruct((nb, R_HEADS, R_DK, R_DV), F32)],
        scratch_shapes=[pltpu.VMEM((R_DK, R_DV), F32)],
        compiler_params=_params(("parallel", "parallel")),
        name="retention",
    )(rq, rk, rv, rg, dm, cd, kd, sd)


def _retention_tables():
    ld = jnp.log(1.0 - 2.0 ** (-5.0 - jnp.arange(R_HEADS, dtype=F32)))[:, None, None]
    i = jnp.arange(R_CHUNK, dtype=F32)
    diff = i[:, None] - i[None, :]
    dmask = jnp.where(diff >= 0, jnp.exp(ld * jnp.maximum(diff, 0.0)), 0.0)
    cdec = jnp.broadcast_to(jnp.exp(ld * (i + 1.0)[None, :, None]), (R_HEADS, R_CHUNK, R_DV))
    kdec = jnp.broadcast_to(jnp.exp(ld * (R_CHUNK - 1.0 - i)[None, :, None]), (R_HEADS, R_CHUNK, R_DK))
    sdec = jnp.broadcast_to(jnp.exp(ld * R_CHUNK), (R_HEADS, R_DK, R_DV))
    return dmask, cdec, kdec, sdec


MASK_BIAS = -1e30
MOBA_SLOTS = 2


def _moba_kernel(n_page, *refs):
    if n_page:
        refs = refs[1:]
    q_ref, k_ref, v_ref = refs[:3]
    pages = refs[3:3 + n_page]
    if n_page:
        o_ref, ks_ref, kb_scr, vb_scr, km_scr, s_scr, p_scr = refs[3 + n_page:]
        _sum_pages(pages, ks_ref.at[0])
    else:
        o_ref, kb_scr, vb_scr, km_scr, s_scr, p_scr = refs[3:]
    blk_sz = MOBA_BLOCK
    nblk = k_ref.shape[0] // blk_sz
    lane = lax.broadcasted_iota(jnp.int32, (blk_sz, A_DH), 1)
    for n in range(nblk):
        rows = slice(n * blk_sz, (n + 1) * blk_sz)
        kn = k_ref[rows, :]
        kb_scr[rows, 0:A_DH] = kn.astype(BF16)
        kb_scr[rows, A_DH:2 * A_DH] = jnp.where(lane == n, 1.0, 0.0).astype(BF16)
        km_scr[n:n + 1, :] = jnp.mean(kn, axis=0, keepdims=True)
        vb_scr[rows, 0:A_DH] = v_ref[rows, :].astype(BF16)
        vb_scr[rows, A_DH:2 * A_DH] = jnp.ones((blk_sz, A_DH), BF16)
    kmean = km_scr[...].astype(BF16)

    for i in range(nblk):
        q = q_ref[i * blk_sz:(i + 1) * blk_sz, :]
        gate = lax.dot_general(kmean, q, NT, preferred_element_type=F32)
        blk = lax.broadcasted_iota(jnp.int32, gate.shape, 0)
        rank = jnp.zeros(gate.shape, jnp.int32)
        for m in range(i):
            gm = gate[m:m + 1, :]
            ahead = (gm > gate) | ((gm == gate) & (m < blk))
            rank = rank + ahead.astype(jnp.int32)
        bias_t = jnp.where((blk < i) & (rank >= MOBA_TOPK), MASK_BIAS, 0.0)
        bias_t = jnp.concatenate([bias_t, jnp.zeros((A_DH - nblk, blk_sz), F32)], axis=0)
        q_aug = jnp.concatenate([q, bias_t.T.astype(BF16)], axis=1)

        nk = (i + 1) * blk_sz
        half = blk_sz // 2
        m_run = jnp.full((blk_sz, half), -jnp.inf, F32)
        for n in range(i + 1):
            cols = slice(n * blk_sz, (n + 1) * blk_sz)
            s = lax.dot_general(q_aug, kb_scr[cols, :], NT, preferred_element_type=F32)
            if n == i:
                qpos = lax.broadcasted_iota(jnp.int32, s.shape, 0)
                kpos = lax.broadcasted_iota(jnp.int32, s.shape, 1)
                s = jnp.where(kpos <= qpos, s, -jnp.inf)
            s_scr[i % MOBA_SLOTS, :, cols] = s
            m_run = jnp.maximum(m_run, jnp.maximum(s[:, :half], s[:, half:]))
        m_row = jnp.max(m_run, axis=-1, keepdims=True)
        for n in range(i + 1):
            cols = slice(n * blk_sz, (n + 1) * blk_sz)
            p_scr[i % MOBA_SLOTS, :, cols] = jnp.exp(s_scr[i % MOBA_SLOTS, :, cols] - m_row).astype(BF16)
        o = jnp.dot(p_scr[i % MOBA_SLOTS, :, 0:nk], vb_scr[0:nk, :], preferred_element_type=F32)
        o_ref[i * blk_sz:(i + 1) * blk_sz, :] = (o[:, :A_DH] / o[:, A_DH:]).astype(o_ref.dtype)


def _moba(aq, k, v, nb, seq, cache_k=None, pt_flat=None, n_page=0):
    nblk = seq // MOBA_BLOCK
    assert nblk <= A_DH
    blk = lambda b, h, *_: (b, h)
    in_specs = [pl.BlockSpec((seq, A_DH), blk)] * 3
    out_specs = [pl.BlockSpec((seq, A_DH), blk)]
    out_shape = [jax.ShapeDtypeStruct((nb * seq, A_W), BF16)]
    args = [aq, k, v]
    if n_page:
        blk_per_step = n_page * PAGE_SIZE // MOBA_BLOCK
        for q in range(n_page):
            in_specs.append(pl.BlockSpec((1, PAGE_SIZE, A_HEADS, A_DH),
                                         lambda b, h, pt, q=q: (pt[(b * A_HEADS + h) * n_page + q], 0, 0, 0)))
        out_specs.append(pl.BlockSpec((1, blk_per_step, A_HEADS, A_DH), lambda b, h, pt: (b * A_HEADS + h, 0, 0, 0)))
        out_shape.append(jax.ShapeDtypeStruct((nb * A_HEADS, blk_per_step, A_HEADS, A_DH), F32))
        args = [pt_flat] + args + [cache_k] * n_page
    grid_spec = pltpu.PrefetchScalarGridSpec(
        num_scalar_prefetch=1 if n_page else 0,
        grid=(nb, A_HEADS),
        in_specs=in_specs,
        out_specs=out_specs,
        scratch_shapes=[pltpu.VMEM((seq, 2 * A_DH), BF16),
                        pltpu.VMEM((seq, 2 * A_DH), BF16),
                        pltpu.VMEM((nblk, A_DH), F32),
                        pltpu.VMEM((MOBA_SLOTS, MOBA_BLOCK, seq), F32),
                        pltpu.VMEM((MOBA_SLOTS, MOBA_BLOCK, seq), BF16)],
    )
    out = pl.pallas_call(
        functools.partial(_moba_kernel, n_page),
        grid_spec=grid_spec,
        out_shape=out_shape,
        compiler_params=_params(("parallel", "parallel")),
        name="moba",
    )(*args)
    return (out[0], out[1].reshape(-1, A_HEADS, A_DH)) if n_page else out[0]


def _route(logits):
    lane = lax.broadcasted_iota(jnp.int32, logits.shape, 1)
    ninf = -jnp.inf
    big = jnp.int32(ROUTER_LANES)
    glog = jnp.where(lane < N_GROUPS, logits, ninf)
    gmax = jnp.max(glog, axis=-1, keepdims=True)
    g_idx = jnp.min(jnp.where(glog == gmax, lane, big), axis=-1, keepdims=True)
    g_w = 1.0 / jnp.sum(jnp.exp(glog - gmax), axis=-1, keepdims=True)
    e_lo = EXPERT_LANE0 + g_idx * EXPERTS_PER_GROUP
    in_group = (lane >= e_lo) & (lane < e_lo + EXPERTS_PER_GROUP)
    e1 = jnp.where(in_group, logits, ninf)
    v1 = jnp.max(e1, axis=-1, keepdims=True)
    i1 = jnp.min(jnp.where(e1 == v1, lane, big), axis=-1, keepdims=True)
    e2 = jnp.where(lane == i1, ninf, e1)
    v2 = jnp.max(e2, axis=-1, keepdims=True)
    i2 = jnp.min(jnp.where(e2 == v2, lane, big), axis=-1, keepdims=True)
    x2 = jnp.exp(v2 - v1)
    den = 1.0 + x2
    comb = jnp.where(lane == i1, g_w * (1.0 / den), jnp.where(lane == i2, g_w * (x2 / den), 0.0))
    return jnp.where(lane == GROUP_ID_LANE, g_idx.astype(F32), comb)


def _outproj_kernel(precise, ret_ref, att_ref, gr_ref, ga_ref, x_ref, wr_ref, wa_ref, wo_ref,
                    gf_ref, wrt_ref, brt_ref, y1_ref, h2_ref, comb_ref):
    prec = HI if precise else None
    mm = F32 if precise else BF16
    r = jnp.dot(ret_ref[...].astype(mm), wr_ref[...], precision=prec, preferred_element_type=F32)
    a = jnp.dot(att_ref[...].astype(mm), wa_ref[...], precision=prec, preferred_element_type=F32)
    merged = jax.nn.sigmoid(gr_ref[...]) * r + jax.nn.sigmoid(ga_ref[...]) * a
    m = jnp.dot(merged.astype(mm), wo_ref[...], precision=prec, preferred_element_type=F32)
    y1 = x_ref[...] + m
    y1_ref[...] = y1
    h2 = _rmsnorm(y1, gf_ref[...])
    h2_ref[...] = h2.astype(h2_ref.dtype)
    logits = jnp.dot(h2.astype(mm), wrt_ref[...], precision=prec, preferred_element_type=F32) + brt_ref[...]
    comb_ref[...] = _route(logits)


def _outproj(ret, att, gr, ga, x, w_ret_o, w_att_o, w_o, g_ffn, w_router, b_router, precise, tm):
    t = x.shape[0]
    row = lambda i: (i, 0)
    full = lambda i: (0, 0)
    wspec = lambda: pl.BlockSpec((D_MODEL, D_MODEL), full, pipeline_mode=pl.Buffered(1))
    return pl.pallas_call(
        functools.partial(_outproj_kernel, precise),
        grid=(t // tm,),
        in_specs=[pl.BlockSpec((tm, D_MODEL), row)] * 5 + [wspec(), wspec(), wspec(),
                  pl.BlockSpec((1, D_MODEL), full),
                  pl.BlockSpec(w_router.shape, full),
                  pl.BlockSpec((1, ROUTER_LANES), full)],
        out_specs=[pl.BlockSpec((tm, D_MODEL), row), pl.BlockSpec((tm, D_MODEL), row),
                   pl.BlockSpec((tm, ROUTER_LANES), row)],
        out_shape=[jax.ShapeDtypeStruct((t, D_MODEL), F32), jax.ShapeDtypeStruct((t, D_MODEL), BF16),
                   jax.ShapeDtypeStruct((t, ROUTER_LANES), F32)],
        compiler_params=_params(("parallel",)),
        name="outproj_precise" if precise else "outproj",
    )(ret, att, gr, ga, x, w_ret_o, w_att_o, w_o, g_ffn, w_router, b_router)


def _sum_pages(pages, out_block):
    per_blk = MOBA_BLOCK // PAGE_SIZE
    for n in range(len(pages) // per_blk):
        acc = jnp.sum(pages[n * per_blk][0], axis=0)
        for p in range(1, per_blk):
            acc = acc + jnp.sum(pages[n * per_blk + p][0], axis=0)
        out_block[n] = acc


def _moe_kernel(epb, h_ref, comb_ref, y1_ref, wg_ref, wu_ref, wd_ref, gf_ref, o_ref, acc_ref):
    e = pl.program_id(1)

    @pl.when(e == 0)
    def _():
        acc_ref[...] = jnp.zeros_like(acc_ref)

    h = h_ref[...]
    comb = comb_ref[...]
    lane = lax.broadcasted_iota(jnp.int32, comb.shape, 1)
    for j in range(epb):
        c = jnp.sum(jnp.where(lane == EXPERT_LANE0 + e * epb + j, comb, 0.0), axis=-1, keepdims=True)
        a = jnp.dot(h, wg_ref[j], preferred_element_type=F32)
        u = jnp.dot(h, wu_ref[j], preferred_element_type=F32)
        act = (a * jax.nn.sigmoid(a)) * u * c
        acc_ref[...] += jnp.dot(act.astype(BF16), wd_ref[j], preferred_element_type=F32)

    @pl.when(e == pl.num_programs(1) - 1)
    def _():
        o_ref[...] = _rmsnorm(y1_ref[...] + acc_ref[...], gf_ref[...])


def _moe(h2, comb, y1, wg, wu, wd, g_final, tm, epb=4):
    t = h2.shape[0]
    row = lambda i, e: (i, 0)
    wblk = lambda i, e: (e, 0, 0)
    return pl.pallas_call(
        functools.partial(_moe_kernel, epb),
        grid=(t // tm, N_EXPERTS // epb),
        in_specs=[pl.BlockSpec((tm, D_MODEL), row), pl.BlockSpec((tm, ROUTER_LANES), row),
                  pl.BlockSpec((tm, D_MODEL), row),
                  pl.BlockSpec((epb, D_MODEL, D_EXPERT), wblk),
                  pl.BlockSpec((epb, D_MODEL, D_EXPERT), wblk),
                  pl.BlockSpec((epb, D_EXPERT, D_MODEL), wblk),
                  pl.BlockSpec((1, D_MODEL), lambda i, e: (0, 0))],
        out_specs=pl.BlockSpec((tm, D_MODEL), row),
        out_shape=jax.ShapeDtypeStruct((t, D_MODEL), F32),
        scratch_shapes=[pltpu.VMEM((tm, D_MODEL), F32)],
        compiler_params=_params(("parallel", "arbitrary")),
        name="moe",
    )(h2, comb, y1, wg, wu, wd, g_final)


SORT_TILE = 128
EXPERT_SPLIT = 2
COMB_LANES = 40
assert EXPERT_LANE0 + N_EXPERTS <= COMB_LANES and 3 * COMB_LANES <= LANES


def _moe_sorted_kernel(n_page, *refs):
    if n_page:
        refs = refs[1:]
    h_ref, comb_ref, y1_ref, wg_ref, wu_ref, wd_ref, gf_ref, tri_ref = refs[:8]
    pages = refs[8:8 + n_page]
    rest = refs[8 + n_page:]
    if n_page:
        o_ref, ks_ref = rest[:2]
        _sum_pages(pages, ks_ref.at[0])
        rest = rest[2:]
    else:
        o_ref, rest = rest[0], rest[1:]
    xs_scr, cs_scr, out_scr, dest_scr, run_smem = rest
    g = pl.program_id(1)
    part = pl.program_id(2)
    tb = h_ref.shape[0]
    cap = xs_scr.shape[0]
    epb = wg_ref.shape[0]

    @pl.when((g == 0) & (part == 0))
    def _sort_block():
        comb = comb_ref[...]
        lane = lax.broadcasted_iota(jnp.int32, comb.shape, 1)
        gcol = comb[:, GROUP_ID_LANE:GROUP_ID_LANE + 1]
        ind = jnp.where((lane < N_GROUPS) & (lane.astype(F32) == gcol), 1.0, 0.0)
        pos = jnp.dot(tri_ref[...], ind.astype(BF16), preferred_element_type=F32)
        dest = jnp.zeros((tb, 1), F32)
        start = jnp.int32(0)
        for gg in range(N_GROUPS):
            col = ind[:, gg:gg + 1]
            n_tiles = (jnp.sum(col).astype(jnp.int32) + SORT_TILE - 1) // SORT_TILE
            run_smem[gg] = start
            run_smem[N_GROUPS + gg] = n_tiles
            dest = dest + col * (pos[:, gg:gg + 1] + start.astype(F32))
            start = start + n_tiles * SORT_TILE
        dest_b = jnp.broadcast_to(dest, (tb, LANES))
        dest_scr[...] = dest_b
        dest_row = dest_b.T[0:1, :].astype(jnp.int32)
        c_hi = comb.astype(BF16).astype(F32)
        r1 = comb - c_hi
        c_mid = r1.astype(BF16).astype(F32)
        c_lo = r1 - c_mid
        packed = (c_hi + pltpu.roll(c_mid, COMB_LANES, 1) + pltpu.roll(c_lo, 2 * COMB_LANES, 1)).astype(BF16)
        lane_t = lax.broadcasted_iota(jnp.int32, (SORT_TILE, ROUTER_LANES), 1)
        h = h_ref[...]
        for c in range(cap // SORT_TILE):
            rows = slice(c * SORT_TILE, (c + 1) * SORT_TILE)

            def sort_rows(c=c, rows=rows):
                riota = lax.broadcasted_iota(jnp.int32, (SORT_TILE, tb), 0) + c * SORT_TILE
                perm = jnp.where(riota == dest_row, 1.0, 0.0).astype(BF16)
                xs_scr[rows, :] = jnp.dot(perm, h, preferred_element_type=F32).astype(BF16)
                pk = jnp.dot(perm, packed, preferred_element_type=F32)
                terms = (pk + pltpu.roll(pk, ROUTER_LANES - COMB_LANES, 1)
                         + pltpu.roll(pk, ROUTER_LANES - 2 * COMB_LANES, 1))
                cs_scr[rows, :] = jnp.where(lane_t < COMB_LANES, terms, 0.0)

            if c * SORT_TILE < tb:
                sort_rows()
            else:
                pl.when(c * SORT_TILE < start)(sort_rows)
        out_scr[...] = jnp.zeros_like(out_scr)

    start = run_smem[g]
    n_tiles = run_smem[N_GROUPS + g]
    lane_c = lax.broadcasted_iota(jnp.int32, (SORT_TILE, ROUTER_LANES), 1)
    lane0 = EXPERT_LANE0 + g * EXPERTS_PER_GROUP + part * epb

    def run_tile(t, carry):
        r0 = pl.multiple_of(start + t * SORT_TILE, SORT_TILE)
        x = xs_scr[pl.ds(r0, SORT_TILE), :]
        cs = cs_scr[pl.ds(r0, SORT_TILE), :]
        acc = jnp.zeros((SORT_TILE, D_MODEL), F32)
        for j in range(epb):
            c = jnp.sum(jnp.where(lane_c == lane0 + j, cs, 0.0), axis=-1, keepdims=True)
            a = jnp.dot(x, wg_ref[j], preferred_element_type=F32)
            u = jnp.dot(x, wu_ref[j], preferred_element_type=F32)
            act = (a * jax.nn.sigmoid(a)) * u * c
            acc = acc + jnp.dot(act.astype(BF16), wd_ref[j], preferred_element_type=F32)
        out_scr[pl.ds(r0, SORT_TILE), :] += acc
        return carry

    lax.fori_loop(0, n_tiles, run_tile, 0)

    @pl.when((g == pl.num_programs(1) - 1) & (part == pl.num_programs(2) - 1))
    def _unsort_block():
        chunk = 2 * SORT_TILE
        lane_r = lax.broadcasted_iota(jnp.int32, (chunk, cap), 1)
        xs_scr[...] = out_scr[...].astype(BF16)
        for c in range(tb // chunk):
            rows = slice(c * chunk, (c + 1) * chunk)
            perm_t = jnp.where(lane_r == dest_scr[rows, 0:1].astype(jnp.int32), 1.0, 0.0).astype(BF16)
            moe = jnp.dot(perm_t, xs_scr[...], preferred_element_type=F32)
            o_ref[rows, :] = _rmsnorm(y1_ref[rows, :] + moe, gf_ref[...])


def _rider_plan(steps, max_pages, n_seq, pages_per_seq):
    per_blk = MOBA_BLOCK // PAGE_SIZE
    n_page = min(max_pages, n_seq * pages_per_seq // steps) // per_blk * per_blk
    if n_page == 0 or pages_per_seq % n_page or (steps * n_page) % pages_per_seq:
        return 0, 0
    return n_page, steps * n_page // pages_per_seq


def _moe_sorted(h2, comb, y1, wg, wu, wd, g_final, tb, cache_k=None, pt_flat=None, n_page=0):
    t = h2.shape[0]
    epb = EXPERTS_PER_GROUP // EXPERT_SPLIT
    cap = tb + N_GROUPS * SORT_TILE
    grid = (t // tb, N_GROUPS, EXPERT_SPLIT)
    row = lambda b, g, p, *_: (b, 0)
    wblk = lambda b, g, p, *_: (g * EXPERT_SPLIT + p, 0, 0)
    const = lambda b, g, p, *_: (0, 0)
    tri = jnp.tril(jnp.ones((tb, tb), BF16), -1)
    in_specs = [pl.BlockSpec((tb, D_MODEL), row, pipeline_mode=pl.Buffered(1)),
                pl.BlockSpec((tb, ROUTER_LANES), row, pipeline_mode=pl.Buffered(1)),
                pl.BlockSpec((tb, D_MODEL), row, pipeline_mode=pl.Buffered(1)),
                pl.BlockSpec((epb, D_MODEL, D_EXPERT), wblk),
                pl.BlockSpec((epb, D_MODEL, D_EXPERT), wblk),
                pl.BlockSpec((epb, D_EXPERT, D_MODEL), wblk),
                pl.BlockSpec((1, D_MODEL), const),
                pl.BlockSpec((tb, tb), const, pipeline_mode=pl.Buffered(1))]
    out_specs = [pl.BlockSpec((tb, D_MODEL), row)]
    out_shape = [jax.ShapeDtypeStruct((t, D_MODEL), F32)]
    args = [h2, comb, y1, wg, wu, wd, g_final, tri]
    if n_page:
        blk_per_step = n_page * PAGE_SIZE // MOBA_BLOCK
        steps = grid[0] * grid[1] * grid[2]
        step = lambda b, g, p: (b * N_GROUPS + g) * EXPERT_SPLIT + p
        for q in range(n_page):
            in_specs.append(pl.BlockSpec((1, PAGE_SIZE, A_HEADS, A_DH),
                                         lambda b, g, p, pt, q=q: (pt[step(b, g, p) * n_page + q], 0, 0, 0)))
        out_specs.append(pl.BlockSpec((1, blk_per_step, A_HEADS, A_DH), lambda b, g, p, pt: (step(b, g, p), 0, 0, 0)))
        out_shape.append(jax.ShapeDtypeStruct((steps, blk_per_step, A_HEADS, A_DH), F32))
        args = [pt_flat] + args + [cache_k] * n_page
    grid_spec = pltpu.PrefetchScalarGridSpec(
        num_scalar_prefetch=1 if n_page else 0,
        grid=grid,
        in_specs=in_specs,
        out_specs=out_specs,
        scratch_shapes=[pltpu.VMEM((cap, D_MODEL), BF16),
                        pltpu.VMEM((cap, ROUTER_LANES), F32),
                        pltpu.VMEM((cap, D_MODEL), F32),
                        pltpu.VMEM((tb, LANES), F32),
                        pltpu.SMEM((2 * N_GROUPS,), jnp.int32)],
    )
    out = pl.pallas_call(
        functools.partial(_moe_sorted_kernel, n_page),
        grid_spec=grid_spec,
        out_shape=out_shape,
        compiler_params=_params(("parallel", "arbitrary", "arbitrary")),
        name="moe_sorted",
    )(*args)
    return (out[0], out[1].reshape(-1, A_HEADS, A_DH)) if n_page else out[0]


def _inproj_sample_kernel(x_ref, g_ref, w_ref, ta_ref, tb_ref, o_ref, wbf_ref):
    w = w_ref[...]
    w_hi = w.astype(BF16)
    wbf_ref[...] = w_hi
    w_lo = (w - w_hi.astype(F32)).astype(BF16)
    h = _rmsnorm(x_ref[...], g_ref[...])
    h_hi = h.astype(BF16)
    h_lo = (h - h_hi.astype(F32)).astype(BF16)
    z = (jnp.dot(h_hi, w_hi, preferred_element_type=F32) + jnp.dot(h_hi, w_lo, preferred_element_type=F32)
         + jnp.dot(h_lo, w_hi, preferred_element_type=F32))
    for c in range(0, z.shape[1], LANES):
        zh = z[:, c:c + LANES]
        o_ref[:, c:c + LANES] = zh * ta_ref[:, c:c + LANES] + pltpu.roll(zh, A_DH // 2, 1) * tb_ref[:, c:c + LANES]


def _inproj_sample(x, g, w_in, ta, tb, tn=512):
    rows = x.shape[0]
    return pl.pallas_call(
        _inproj_sample_kernel,
        grid=(D_IN // tn,),
        in_specs=[pl.BlockSpec((rows, D_MODEL), lambda j: (0, 0)),
                  pl.BlockSpec((1, D_MODEL), lambda j: (0, 0)),
                  pl.BlockSpec((D_MODEL, tn), lambda j: (0, j)),
                  pl.BlockSpec((rows, tn), lambda j: (0, j)),
                  pl.BlockSpec((rows, tn), lambda j: (0, j))],
        out_specs=[pl.BlockSpec((rows, tn), lambda j: (0, j)), pl.BlockSpec((D_MODEL, tn), lambda j: (0, j))],
        out_shape=[jax.ShapeDtypeStruct((rows, D_IN), F32), jax.ShapeDtypeStruct((D_MODEL, D_IN), BF16)],
        compiler_params=_params(("parallel",)),
        name="inproj_sample",
    )(x, g, w_in, ta, tb)


def _sample_tables(rows, pos):
    half = A_DH // 2
    inv = ROPE_THETA ** (-jnp.arange(half, dtype=F32) / half)
    ang = jnp.full((rows, 1), pos, F32) * inv[None, :]
    cs = jnp.concatenate([jnp.cos(ang), jnp.cos(ang)], axis=-1)
    sn = jnp.concatenate([-jnp.sin(ang), jnp.sin(ang)], axis=-1)
    ones = jnp.ones((rows, LANES), F32)
    zeros = jnp.zeros((rows, LANES), F32)
    ta, tb = [], []
    for off, width, kind in ((OFF_RQ, R_QK_W, R_DK ** -0.5), (OFF_RK, R_QK_W, 1.0), (OFF_RV, 2 * R_V_W, None),
                             (OFF_AQ, A_W, A_DH ** -0.5), (OFF_AK, A_W, 1.0), (OFF_AV, A_W + 2 * D_MODEL, None)):
        for _ in range(width // LANES):
            ta.append(ones if kind is None else cs * kind)
            tb.append(zeros if kind is None else sn * kind)
    return jnp.concatenate(ta, axis=-1), jnp.concatenate(tb, axis=-1)


def _retention_step_kernel(q_ref, k_ref, v_ref, g_ref, s_ref, dec_ref, o_ref, sn_ref):
    for i in range(q_ref.shape[0]):
        for h in range(R_HEADS):
            col = lambda r: jnp.broadcast_to(r[i, h:h + 1, :], (R_DK, R_DK)).T[:, 0:1]
            q = col(q_ref)
            k = col(k_ref)
            v = v_ref[i, h]
            s = s_ref[i, h]
            dec = dec_ref[h]
            cross = jnp.sum(q * s, axis=0, keepdims=True) * dec
            inner = jnp.sum(q * k, axis=0, keepdims=True) * v
            sn_ref[i, h] = s * dec + k * v
            o = inner + cross
            mu = jnp.mean(o, axis=-1, keepdims=True)
            d = o - mu
            var = jnp.mean(d * d, axis=-1, keepdims=True)
            gate = g_ref[i, h]
            o_ref[i, h] = gate * jax.nn.sigmoid(gate) * (d * lax.rsqrt(var + EPS))


def _retention_step(q, k, v, g, state, dec):
    nb = q.shape[0]
    per = SAMPLE_SEQS_PER_STEP if nb % SAMPLE_SEQS_PER_STEP == 0 else 1
    col = pl.BlockSpec((per, R_HEADS, R_DK), lambda b: (b, 0, 0))
    rowv = pl.BlockSpec((per, R_HEADS, 1, R_DV), lambda b: (b, 0, 0, 0))
    st = pl.BlockSpec((per, R_HEADS, R_DK, R_DV), lambda b: (b, 0, 0, 0))
    return pl.pallas_call(
        _retention_step_kernel,
        grid=(nb // per,),
        in_specs=[col, col, rowv, rowv, st, pl.BlockSpec((R_HEADS, 1, R_DV), lambda b: (0, 0, 0))],
        out_specs=[rowv, st],
        out_shape=[jax.ShapeDtypeStruct((nb, R_HEADS, 1, R_DV), F32),
                   jax.ShapeDtypeStruct((nb, R_HEADS, R_DK, R_DV), F32)],
        compiler_params=_params(("parallel",)),
        name="retention_step",
    )(q, k, v, g, state, dec)


def _block_sum_kernel(pt_ref, *refs):
    _sum_pages(refs[:-1], refs[-1].at[0, 0])


def _block_sums(cache_k, pt_flat, nb, n_pages):
    steps = n_pages // PAGES_PER_STEP
    blk_per_step = PAGES_PER_STEP * PAGE_SIZE // MOBA_BLOCK

    def page_spec(p):
        return pl.BlockSpec((1, PAGE_SIZE, A_HEADS, A_DH),
                            lambda b, j, pt: (pt[b * n_pages + j * PAGES_PER_STEP + p], 0, 0, 0))

    grid_spec = pltpu.PrefetchScalarGridSpec(
        num_scalar_prefetch=1,
        grid=(nb, steps),
        in_specs=[page_spec(p) for p in range(PAGES_PER_STEP)],
        out_specs=pl.BlockSpec((1, 1, blk_per_step, A_HEADS, A_DH), lambda b, j, pt: (b, j, 0, 0, 0)),
    )
    out = pl.pallas_call(
        _block_sum_kernel,
        grid_spec=grid_spec,
        out_shape=jax.ShapeDtypeStruct((nb, steps, blk_per_step, A_HEADS, A_DH), F32),
        compiler_params=_params(("parallel", "parallel")),
        name="cache_block_sums",
    )(pt_flat, *([cache_k] * PAGES_PER_STEP))
    return out.reshape(nb, steps * blk_per_step, A_W)


def _block_pick_kernel(q_ref, ks_ref, o_ref):
    for i in range(q_ref.shape[0]):
        ks = ks_ref[i] * (1.0 / MOBA_BLOCK)
        prod = ks * q_ref[i]
        nblk = prod.shape[0]
        lane = lax.broadcasted_iota(jnp.int32, (nblk, LANES), 1)
        gate = jnp.full((nblk, LANES), -jnp.inf, F32)
        for h in range(A_HEADS):
            gate = jnp.where(lane == h, jnp.sum(prod[:, h * A_DH:(h + 1) * A_DH], axis=-1, keepdims=True), gate)
        blk = lax.broadcasted_iota(jnp.int32, gate.shape, 0)
        rank = jnp.zeros(gate.shape, jnp.int32)
        for m in range(nblk):
            gm = gate[m:m + 1, :]
            ahead = (gm > gate) | ((gm == gate) & (m < blk))
            rank = rank + ahead.astype(jnp.int32)
        for j in range(MOBA_TOPK):
            o_ref[i, j:j + 1, :] = jnp.sum(jnp.where(rank == j, blk, 0), axis=0, keepdims=True)


def _block_pick(q, ksum):
    nb, nblk, _ = ksum.shape
    per = SAMPLE_SEQS_PER_STEP if nb % SAMPLE_SEQS_PER_STEP == 0 else 1
    return pl.pallas_call(
        _block_pick_kernel,
        grid=(nb // per,),
        in_specs=[pl.BlockSpec((per, 1, A_W), lambda b: (b, 0, 0)),
                  pl.BlockSpec((per, nblk, A_W), lambda b: (b, 0, 0))],
        out_specs=pl.BlockSpec((per, MOBA_TOPK, LANES), lambda b: (b, 0, 0)),
        out_shape=jax.ShapeDtypeStruct((nb, MOBA_TOPK, LANES), jnp.int32),
        compiler_params=_params(("parallel",)),
        name="block_pick",
    )(q, ksum)


PICK_PAGES = MOBA_TOPK * (MOBA_BLOCK // PAGE_SIZE)


def _attend_sample_kernel(pages_per_seq, pt_ref, idx_ref, q_ref, kn_ref, vn_ref, ck_ref, cv_ref, o_ref,
                          kbuf, vbuf, ksem, vsem):
    b = pl.program_id(0)
    per_blk = MOBA_BLOCK // PAGE_SIZE

    def page_copies(seq_i, slot):
        out = []
        for h in range(A_HEADS):
            for j in range(MOBA_TOPK):
                blk = idx_ref[(seq_i * MOBA_TOPK + j) * A_HEADS + h]
                for p in range(per_blk):
                    page = pt_ref[seq_i * pages_per_seq + blk * per_blk + p]
                    dst = (slot, h, j * per_blk + p)
                    out.append(pltpu.make_async_copy(ck_ref.at[page, :, h, :], kbuf.at[dst], ksem.at[slot]))
                    out.append(pltpu.make_async_copy(cv_ref.at[page, :, h, :], vbuf.at[dst], vsem.at[slot]))
        return out

    slot = b % 2

    @pl.when(b == 0)
    def _():
        for c in page_copies(0, 0):
            c.start()

    @pl.when(b + 1 < pl.num_programs(0))
    def _():
        for c in page_copies(b + 1, 1 - slot):
            c.start()

    for c in page_copies(b, slot):
        c.wait()

    for h in range(A_HEADS):
        q = q_ref[0, h:h + 1, :]
        s_own = jnp.sum(q * kn_ref[0, h:h + 1, :], axis=-1, keepdims=True)
        scores = [jnp.sum(kbuf[slot, h, g] * q, axis=-1, keepdims=True) for g in range(PICK_PAGES)]
        m = s_own
        for s in scores:
            m = jnp.maximum(m, jnp.max(s, axis=0, keepdims=True))
        p_own = jnp.exp(s_own - m)
        l = p_own
        acc = p_own * vn_ref[0, h:h + 1, :]
        for g, s in enumerate(scores):
            p = jnp.exp(s - m)
            l = l + jnp.sum(p, axis=0, keepdims=True)
            acc = acc + jnp.sum(p * vbuf[slot, h, g], axis=0, keepdims=True)
        o_ref[0, h:h + 1, :] = acc / l


def _attend_sample(q, k_new, v_new, cache_k, cache_v, pt_flat, picks_flat, pages_per_seq):
    nb = q.shape[0]
    vec = pl.BlockSpec((1, A_HEADS, A_DH), lambda b, pt, idx: (b, 0, 0))
    hbm = pl.BlockSpec(memory_space=pl.ANY)
    buf = pltpu.VMEM((2, A_HEADS, PICK_PAGES, PAGE_SIZE, A_DH), F32)
    grid_spec = pltpu.PrefetchScalarGridSpec(
        num_scalar_prefetch=2,
        grid=(nb,),
        in_specs=[vec, vec, vec, hbm, hbm],
        out_specs=vec,
        scratch_shapes=[buf, buf, pltpu.SemaphoreType.DMA((2,)), pltpu.SemaphoreType.DMA((2,))],
    )
    return pl.pallas_call(
        functools.partial(_attend_sample_kernel, pages_per_seq),
        grid_spec=grid_spec,
        out_shape=jax.ShapeDtypeStruct((nb, A_HEADS, A_DH), F32),
        compiler_params=_params(("arbitrary",)),
        name="attend_sample",
    )(pt_flat, picks_flat, q, k_new, v_new, cache_k, cache_v)


def _rope_tables(seq):
    half = A_DH // 2
    inv = ROPE_THETA ** (-jnp.arange(half, dtype=F32) / half)
    ang = jnp.arange(seq, dtype=jnp.int32).astype(F32)[:, None] * inv[None, :]
    cos, sin = jnp.cos(ang), jnp.sin(ang)
    return jnp.concatenate([cos, cos], axis=-1), jnp.concatenate([-sin, sin], axis=-1)


def _router_weights(w_rg, b_rg, w_re, b_re):
    pad = ROUTER_LANES - N_GROUPS - N_EXPERTS
    w = jnp.concatenate([w_rg, w_re.reshape(D_MODEL, N_EXPERTS), jnp.zeros((D_MODEL, pad), F32)], axis=-1)
    b = jnp.concatenate([b_rg, b_re.reshape(N_EXPERTS), jnp.zeros((pad,), F32)])[None, :]
    return w, b


def kernel(x_prompt, x_sample, cache_k, cache_v, state_ret, page_table, norm_mix, w_in, w_ret_o, w_att_o, w_o,
           norm_ffn, w_router_group, b_router_group, w_router_expert, b_router_expert, w_gate, w_up, w_down,
           norm_final):
    nb, seq, _ = x_prompt.shape
    db, ds, _ = x_sample.shape
    depth = norm_mix.shape[0]
    past_len = page_table.shape[1] * PAGE_SIZE
    assert depth == 1 and ds == 1
    assert seq % MOBA_BLOCK == 0 and past_len % (PAGES_PER_STEP * PAGE_SIZE) == 0
    assert past_len // MOBA_BLOCK >= MOBA_TOPK
    l = 0
    t = nb * seq

    g_mix = norm_mix[l][None, :]
    g_ffn = norm_ffn[l][None, :]
    g_fin = norm_final[None, :]
    w_router, b_router = _router_weights(w_router_group[l], b_router_group[l], w_router_expert[l], b_router_expert[l])
    xs = x_sample.reshape(db, D_MODEL)
    ta, tb = _sample_tables(db, past_len)
    zs, w_in_bf = _inproj_sample(xs, g_mix, w_in[l], ta, tb)
    wro_bf, wao_bf, wo_bf = w_ret_o[l].astype(BF16), w_att_o[l].astype(BF16), w_o[l].astype(BF16)

    xp = x_prompt.reshape(t, D_MODEL)
    cs, sn = _rope_tables(seq)
    moe_w = [w_gate[l].reshape(-1, D_EXPERT), w_up[l].reshape(-1, D_EXPERT), w_down[l].reshape(-1, D_MODEL)]
    ride_casts = all(_castable_in_steps(a, t // INPROJ_TM) for a in moe_w)
    proj = _inproj(xp, g_mix, w_in_bf, cs, sn, seq, cast_f32=moe_w if ride_casts else (), tm=INPROJ_TM)
    rq, rk, rv, rg, aq, kp, vp, gr, ga = proj[:9]
    moe_w_bf = proj[9:] if ride_casts else [a.astype(BF16) for a in moe_w]
    wg_bf, wu_bf, wd_bf = (a.reshape(s.shape[1:]) for a, s in zip(moe_w_bf, (w_gate, w_up, w_down)))
    ret, ret_state = _retention(rq, rk, rv, rg, _retention_tables(), nb, seq)
    ck, cv = cache_k[l], cache_v[l]
    n_pages = page_table.shape[1]
    pt_flat = page_table.reshape(-1)
    blocks_per_seq = n_pages * PAGE_SIZE // MOBA_BLOCK
    moe_tb = MOE_BLOCK if t % MOE_BLOCK == 0 else MOE_BLOCK // 2
    moe_steps = (t // moe_tb) * N_GROUPS * EXPERT_SPLIT
    moe_pages, moe_seqs = _rider_plan(moe_steps, MOE_RIDER_PAGES, db, n_pages)
    moba_pages, moba_seqs = _rider_plan(nb * A_HEADS, MOBA_RIDER_PAGES, db - moe_seqs, n_pages)
    ksum_parts = []

    moba_out = _moba(aq, kp, vp, nb, seq, cache_k=ck, pt_flat=pt_flat[moe_seqs * n_pages:], n_page=moba_pages)
    att, ks_moba = moba_out if moba_pages else (moba_out, None)
    y1, h2, comb = _outproj(ret, att, gr, ga, xp, wro_bf, wao_bf, wo_bf, g_ffn, w_router.astype(BF16), b_router,
                            precise=False, tm=OUTPROJ_TM)
    moe_out = _moe_sorted(h2, comb, y1, wg_bf, wu_bf, wd_bf, g_fin, moe_tb,
                          cache_k=ck, pt_flat=pt_flat, n_page=moe_pages)
    y_prompt, ks_moe = moe_out if moe_pages else (moe_out, None)
    y_prompt = y_prompt.reshape(nb, seq, D_MODEL)
    if moe_pages:
        ksum_parts.append(ks_moe.reshape(moe_seqs, blocks_per_seq, A_W))
    if moba_pages:
        ksum_parts.append(ks_moba.reshape(moba_seqs, blocks_per_seq, A_W))
    done_seqs = moe_seqs + moba_seqs
    if done_seqs < db:
        ksum_parts.append(_block_sums(ck, pt_flat[done_seqs * n_pages:], db - done_seqs, n_pages))
    ksum = jnp.concatenate(ksum_parts, axis=0) if len(ksum_parts) > 1 else ksum_parts[0]

    seg = lambda off, width: zs[:, off:off + width]
    ld = jnp.log(1.0 - 2.0 ** (-5.0 - jnp.arange(R_HEADS, dtype=F32)))
    dec = jnp.broadcast_to(jnp.exp(ld * 1.0)[:, None, None], (R_HEADS, 1, R_DV))
    ret_s, state_s = _retention_step(
        seg(OFF_RQ, R_QK_W).reshape(db, R_HEADS, R_DK), seg(OFF_RK, R_QK_W).reshape(db, R_HEADS, R_DK),
        seg(OFF_RV, R_V_W).reshape(db, R_HEADS, 1, R_DV), seg(OFF_RG, R_V_W).reshape(db, R_HEADS, 1, R_DV),
        state_ret[l].astype(F32), dec)
    aq_s, ak_s, av_s = seg(OFF_AQ, A_W), seg(OFF_AK, A_W), seg(OFF_AV, A_W)
    picks = _block_pick(aq_s.reshape(db, 1, A_W), ksum)[:, :, :A_HEADS].reshape(-1)
    heads = lambda z: z.reshape(db, A_HEADS, A_DH)
    att_s = _attend_sample(heads(aq_s), heads(ak_s), heads(av_s), ck, cv, pt_flat, picks, n_pages)
    y1_s, h2_s, comb_s = _outproj(ret_s.reshape(db, R_V_W), att_s.reshape(db, A_W), seg(OFF_GR, D_MODEL),
                                  seg(OFF_GA, D_MODEL), xs, w_ret_o[l], w_att_o[l], w_o[l], g_ffn,
                                  w_router, b_router, precise=True, tm=db)
    y_sample = _moe(h2_s, comb_s, y1_s, wg_bf, wu_bf, wd_bf, g_fin, tm=db).reshape(db, ds, D_MODEL)

    k_prompt = kp.reshape(1, nb, seq, A_HEADS, A_DH)
    v_prompt = vp.reshape(1, nb, seq, A_HEADS, A_DH)
    ret_prompt = ret_state[None]
    k_sample = ak_s.reshape(1, db, ds, A_HEADS, A_DH).astype(cache_k.dtype)
    v_sample = av_s.reshape(1, db, ds, A_HEADS, A_DH).astype(cache_v.dtype)
    ret_sample = state_s[None].astype(state_ret.dtype)
    return (y_prompt, y_sample, k_prompt, v_prompt, ret_prompt, k_sample, v_sample, ret_sample)
```
